```python
import math
import jax
import jax.numpy as jnp
from jax import lax
import numpy as np


D_MODEL = 1024
BATCH = 2
SEQ = 8192
DEPTH = 4
DEC_BATCH = 8
DEC_SEQ = 32
PAST_LEN = 1024

CHUNK = 64
N_MIXERS = 3
N_A = (DEPTH + 2) // 3
N_B = (DEPTH + 1) // 3
N_C = DEPTH // 3
D_FF = 2816
D_RNN = 1280
LRU_BLOCKS = 16
LRU_BS = D_RNN // LRU_BLOCKS
CONV_W = 4
LRU_C = 8.0
POOL_WINDOWS = (2, 4, 8, 16)
POOL_GROUPS = 4
POOL_GW = D_MODEL // POOL_GROUPS
POOL_HIST = 15
N_HEADS = 8
HEAD_DIM = D_MODEL // (2 * N_HEADS)
NUM_BUCKETS = 32
MAX_DISTANCE = 128
QBLOCK = 128
EPS = 1e-6
NEG_INF = -1e30

kernel_name = 'hybrid_streaming_encoder_step'


def rmsnorm(x, g):
    x32 = x.astype(jnp.float32)
    y = x32 * lax.rsqrt(jnp.mean(x32 * x32, axis=-1, keepdims=True) + EPS)
    return (y * g.astype(jnp.float32)).astype(x.dtype)


def modulate(x, g, shift, scale):
    return rmsnorm(x, g) * (1 + scale[:, None, :]) + shift[:, None, :]


def adaln(c, w, b):
    return jnp.split(jax.nn.silu(c) @ w + b, 9, axis=-1)


def swiglu(u, w_in, w_out):
    a, b = jnp.split(u @ w_in, 2, axis=-1)
    return (jax.nn.silu(a) * b) @ w_out


def ffn_half(x, shift, scale, gate, g, w_in, w_out):
    u = modulate(x, g, shift, scale)
    return x + 0.5 * gate[:, None, :] * swiglu(u, w_in, w_out)


def causal_dwconv(x, buf, w, b):
    T = x.shape[1]
    xp = jnp.concatenate([buf.astype(x.dtype), x], axis=1)
    y = b + w[0] * xp[:, 0:T]
    for k in range(1, CONV_W):
        y = y + w[k] * xp[:, k:k + T]
    return y, xp[:, -(CONV_W - 1):]


def rg_lru_mixer(u, conv_buf, h0, w_in, conv_w, conv_b, ga_w, ga_b, gx_w, gx_b, lam, w_out):
    B, T, _ = u.shape
    gate_br, x_br = jnp.split(u @ w_in, 2, axis=-1)
    xc, new_buf = causal_dwconv(x_br, conv_buf, conv_w, conv_b)
    xb = xc.reshape(B, T, LRU_BLOCKS, LRU_BS)
    r = jax.nn.sigmoid(jnp.einsum('bthi,hij->bthj', xb, ga_w).reshape(B, T, D_RNN) + ga_b)
    i = jax.nn.sigmoid(jnp.einsum('bthi,hij->bthj', xb, gx_w).reshape(B, T, D_RNN) + gx_b)
    log_a = (-LRU_C * r.astype(jnp.float32)) * jax.nn.softplus(-lam.astype(jnp.float32))
    a = jnp.exp(log_a)
    b_in = jnp.sqrt(-jnp.expm1(2.0 * log_a)) * (i * xc).astype(jnp.float32)
    b_in = b_in.at[:, 0].add(a[:, 0] * h0.astype(jnp.float32))

    def combine(left, right):
        a1, b1 = left
        a2, b2 = right
        return a1 * a2, a2 * b1 + b2

    _, h = lax.associative_scan(combine, (a, b_in), axis=1)
    y = (jax.nn.gelu(gate_br) * h.astype(u.dtype)) @ w_out
    return y, new_buf, h[:, -1].astype(h0.dtype)


def pool_mixer(u_ext, n_new, w, b, scale):
    B, L, _ = u_ext.shape
    u32 = u_ext.astype(jnp.float32)
    cs = jnp.concatenate([jnp.zeros((B, 1, D_MODEL), jnp.float32), jnp.cumsum(u32, axis=1)], axis=1)
    t = jnp.arange(L - n_new, L)
    outs = []
    for g, wnd in enumerate(POOL_WINDOWS):
        sl = slice(g * POOL_GW, (g + 1) * POOL_GW)
        lo = jnp.maximum(t + 1 - wnd, 0)
        cnt = jnp.minimum(t + 1, wnd).astype(jnp.float32)
        mean = (cs[:, t + 1, sl] - cs[:, lo, sl]) / cnt[None, :, None]
        d = (mean - u32[:, t, sl]).astype(u_ext.dtype)
        outs.append(jnp.einsum('btc,cd->btd', d, w[g]))
    y = jnp.concatenate(outs, axis=-1) + b
    return (y * scale).astype(u_ext.dtype)


def t5_bucket(rel):
    nb = NUM_BUCKETS // 2
    max_exact = nb // 2
    ret = jnp.where(rel > 0, nb, 0)
    n = jnp.abs(rel)
    nf = jnp.maximum(n, 1).astype(jnp.float32)
    large = max_exact + (jnp.log(nf / max_exact) / math.log(MAX_DISTANCE / max_exact)
                         * (nb - max_exact)).astype(jnp.int32)
    large = jnp.minimum(large, nb - 1)
    return ret + jnp.where(n < max_exact, n, large)


def diff_lambda(lam_p, lam_init):
    lp = lam_p.astype(jnp.float32)
    return jnp.exp(jnp.sum(lp[0] * lp[1])) - jnp.exp(jnp.sum(lp[2] * lp[3])) + lam_init


def diff_attn_qkv(u, w_in, q_g, k_g):
    B, T, _ = u.shape
    q, k, v = jnp.split(u @ w_in, 3, axis=-1)
    q = rmsnorm(q.reshape(B, T, N_HEADS, 2, HEAD_DIM), q_g)
    k = rmsnorm(k.reshape(B, T, N_HEADS, 2, HEAD_DIM), k_g)
    v = v.reshape(B, T, N_HEADS, 2 * HEAD_DIM)
    return q, k, v


def diff_attn_core(q, k, v, q_pos, k_pos, rel_bias, lam):
    s = jnp.einsum('bqhcd,bkhcd->bchqk', q.astype(jnp.float32), k.astype(jnp.float32)) * (HEAD_DIM ** -0.5)
    bias = jnp.transpose(rel_bias.astype(jnp.float32)[t5_bucket(k_pos[None, :] - q_pos[:, None])], (2, 0, 1))
    mask = (k_pos[None, :] // CHUNK) <= (q_pos[:, None] // CHUNK)
    s = jnp.where(mask, s + bias, NEG_INF)
    p = jax.nn.softmax(s, axis=-1)
    pd = p[:, 0] - lam * p[:, 1]
    return jnp.einsum('bhqk,bkhd->bqhd', pd.astype(v.dtype), v)


def diff_attn_out(o, sub_g, lam_init, w_out):
    B, T = o.shape[:2]
    o = rmsnorm(o, sub_g) * (1.0 - lam_init)
    return o.reshape(B, T, D_MODEL) @ w_out


def diff_attn_prompt(u, w_in, q_g, k_g, sub_g, w_out, rel_bias, lam, lam_init):
    B, S, _ = u.shape
    q, k, v = diff_attn_qkv(u, w_in, q_g, k_g)
    pos = jnp.arange(S)

    def block(i):
        start = i * QBLOCK
        qb = lax.dynamic_slice_in_dim(q, start, QBLOCK, axis=1)
        return diff_attn_core(qb, k, v, start + jnp.arange(QBLOCK), pos, rel_bias, lam)

    o = lax.map(block, jnp.arange(S // QBLOCK))
    o = jnp.moveaxis(o, 0, 1).reshape(B, S, N_HEADS, 2 * HEAD_DIM)
    y = diff_attn_out(o, sub_g, lam_init, w_out)
    return y, k.reshape(B, S, N_HEADS, 2 * HEAD_DIM), v


def diff_attn_sample(u, ck, cv, w_in, q_g, k_g, sub_g, w_out, rel_bias, lam, lam_init):
    B, T, _ = u.shape
    P = ck.shape[1]
    q, k, v = diff_attn_qkv(u, w_in, q_g, k_g)
    kf = jnp.concatenate([ck.astype(k.dtype).reshape(B, P, N_HEADS, 2, HEAD_DIM), k], axis=1)
    vf = jnp.concatenate([cv.astype(v.dtype), v], axis=1)
    o = diff_attn_core(q, kf, vf, P + jnp.arange(T), jnp.arange(P + T), rel_bias, lam)
    y = diff_attn_out(o, sub_g, lam_init, w_out)
    return y, k.reshape(B, T, N_HEADS, 2 * HEAD_DIM), v


def setup_inputs(seed: int = 0) -> dict:
    key = jax.random.key(seed)
    ks = iter(jax.random.split(key, 48))
    nrm = lambda shape, s: jax.random.normal(next(ks), shape, jnp.float32) * s
    u_a = jax.random.uniform(next(ks), (N_A, D_RNN), jnp.float32, minval=0.9, maxval=0.999)
    a0 = u_a ** (1.0 / LRU_C)
    return {
        'x_prompt': nrm((BATCH, SEQ, D_MODEL), 1.0),
        'x_sample': nrm((DEC_BATCH, DEC_SEQ, D_MODEL), 1.0),
        'c_prompt': nrm((BATCH, D_MODEL), 1.0),
        'c_sample': nrm((DEC_BATCH, D_MODEL), 1.0),
        'state_lru_h': nrm((N_A, DEC_BATCH, D_RNN), 0.5),
        'state_lru_conv': nrm((N_A, DEC_BATCH, CONV_W - 1, D_RNN), 1.0),
        'state_pool': nrm((N_B, DEC_BATCH, POOL_HIST, D_MODEL), 1.0),
        'cache_k': nrm((N_C, DEC_BATCH, PAST_LEN, N_HEADS, 2 * HEAD_DIM), 1.0),
        'cache_v': nrm((N_C, DEC_BATCH, PAST_LEN, N_HEADS, 2 * HEAD_DIM), 1.0),
        'ada_w': nrm((DEPTH, D_MODEL, 9 * D_MODEL), D_MODEL ** -0.5),
        'ada_b': nrm((DEPTH, 9 * D_MODEL), 0.02),
        'norm_g': 1.0 + nrm((DEPTH, 3, D_MODEL), 0.02),
        'ffn_w_in': nrm((DEPTH, 2, D_MODEL, 2 * D_FF), D_MODEL ** -0.5),
        'ffn_w_out': nrm((DEPTH, 2, D_FF, D_MODEL), D_FF ** -0.5),
        'lru_w_in': nrm((N_A, D_MODEL, 2 * D_RNN), D_MODEL ** -0.5),
        'lru_conv_w': nrm((N_A, CONV_W, D_RNN), CONV_W ** -0.5),
        'lru_conv_b': nrm((N_A, D_RNN), 0.02),
        'lru_ga_w': nrm((N_A, LRU_BLOCKS, LRU_BS, LRU_BS), LRU_BS ** -0.5),
        'lru_ga_b': nrm((N_A, D_RNN), 0.02),
        'lru_gx_w': nrm((N_A, LRU_BLOCKS, LRU_BS, LRU_BS), LRU_BS ** -0.5),
        'lru_gx_b': nrm((N_A, D_RNN), 0.02),
        'lru_lambda': jnp.log(a0) - jnp.log1p(-a0),
        'lru_w_out': nrm((N_A, D_RNN, D_MODEL), D_RNN ** -0.5),
        'pool_w': nrm((N_B, POOL_GROUPS, POOL_GW, POOL_GW), POOL_GW ** -0.5),
        'pool_b': nrm((N_B, D_MODEL), 0.02),
        'pool_scale': 1.0 + nrm((N_B, D_MODEL), 0.02),
        'attn_w_in': nrm((N_C, D_MODEL, 3 * D_MODEL), D_MODEL ** -0.5),
        'attn_q_g': 1.0 + nrm((N_C, HEAD_DIM), 0.02),
        'attn_k_g': 1.0 + nrm((N_C, HEAD_DIM), 0.02),
        'attn_lambda': nrm((N_C, 4, HEAD_DIM), 0.1),
        'attn_sub_g': 1.0 + nrm((N_C, 2 * HEAD_DIM), 0.02),
        'attn_w_out': nrm((N_C, D_MODEL, D_MODEL), D_MODEL ** -0.5),
        'rel_bias': nrm((NUM_BUCKETS, N_HEADS), 0.5),
    }


def reference(x_prompt, x_sample, c_prompt, c_sample, state_lru_h, state_lru_conv, state_pool,
              cache_k, cache_v, ada_w, ada_b, norm_g, ffn_w_in, ffn_w_out, lru_w_in, lru_conv_w,
              lru_conv_b, lru_ga_w, lru_ga_b, lru_gx_w, lru_gx_b, lru_lambda, lru_w_out, pool_w,
              pool_b, pool_scale, attn_w_in, attn_q_g, attn_k_g, attn_lambda, attn_sub_g,
              attn_w_out, rel_bias):
    B, S, _ = x_prompt.shape
    T = x_sample.shape[1]
    xp, xs = x_prompt, x_sample
    h_p, h_s, cv_p, cv_s, pl_p, pl_s, k_p, v_p, k_s, v_s = [], [], [], [], [], [], [], [], [], []
    for l in range(DEPTH):
        kind = l % N_MIXERS
        j = l // N_MIXERS
        mp = adaln(c_prompt, ada_w[l], ada_b[l])
        ms = adaln(c_sample, ada_w[l], ada_b[l])
        xp = ffn_half(xp, mp[0], mp[1], mp[2], norm_g[l, 0], ffn_w_in[l, 0], ffn_w_out[l, 0])
        xs = ffn_half(xs, ms[0], ms[1], ms[2], norm_g[l, 0], ffn_w_in[l, 0], ffn_w_out[l, 0])
        up = modulate(xp, norm_g[l, 1], mp[3], mp[4])
        us = modulate(xs, norm_g[l, 1], ms[3], ms[4])
        if kind == 0:
            prm = (lru_w_in[j], lru_conv_w[j], lru_conv_b[j], lru_ga_w[j], lru_ga_b[j],
                   lru_gx_w[j], lru_gx_b[j], lru_lambda[j], lru_w_out[j])
            yp, bp, hp = rg_lru_mixer(up, jnp.zeros((B, CONV_W - 1, D_RNN), up.dtype),
                                      jnp.zeros((B, D_RNN), up.dtype), *prm)
            ys, bs, hs = rg_lru_mixer(us, state_lru_conv[j], state_lru_h[j], *prm)
            h_p.append(hp)
            h_s.append(hs)
            cv_p.append(bp)
            cv_s.append(bs)
        elif kind == 1:
            yp = pool_mixer(up, S, pool_w[j], pool_b[j], pool_scale[j])
            ext = jnp.concatenate([state_pool[j].astype(us.dtype), us], axis=1)
            ys = pool_mixer(ext, T, pool_w[j], pool_b[j], pool_scale[j])
            pl_p.append(up[:, -POOL_HIST:])
            pl_s.append(ext[:, -POOL_HIST:])
        else:
            lam_init = 0.8 - 0.6 * math.exp(-0.3 * l)
            lam = diff_lambda(attn_lambda[j], lam_init)
            prm = (attn_w_in[j], attn_q_g[j], attn_k_g[j], attn_sub_g[j], attn_w_out[j],
                   rel_bias, lam, lam_init)
            yp, kp, vp = diff_attn_prompt(up, *prm)
            ys, kn, vn = diff_attn_sample(us, cache_k[j], cache_v[j], *prm)
            k_p.append(kp)
            v_p.append(vp)
            k_s.append(kn)
            v_s.append(vn)
        xp = xp + mp[5][:, None, :] * yp
        xs = xs + ms[5][:, None, :] * ys
        xp = ffn_half(xp, mp[6], mp[7], mp[8], norm_g[l, 2], ffn_w_in[l, 1], ffn_w_out[l, 1])
        xs = ffn_half(xs, ms[6], ms[7], ms[8], norm_g[l, 2], ffn_w_in[l, 1], ffn_w_out[l, 1])
    return (xp, xs, jnp.stack(h_p), jnp.stack(h_s), jnp.stack(cv_p), jnp.stack(cv_s),
            jnp.stack(pl_p), jnp.stack(pl_s), jnp.stack(k_p), jnp.stack(v_p),
            jnp.stack(k_s), jnp.stack(v_s))
```

```python
import functools
import math

import jax
import jax.numpy as jnp
from jax import lax
from jax.experimental import pallas as pl
from jax.experimental.pallas import tpu as pltpu

F32 = jnp.float32
BF16 = jnp.bfloat16

D_MODEL = 1024
DEPTH = 4
D_FF = 2816
D_RNN = 1280
LRU_BLOCKS = 16
LRU_BS = D_RNN // LRU_BLOCKS
CONV_W = 4
LRU_C = 8.0
POOL_WINDOWS = (2, 4, 8, 16)
POOL_GW = D_MODEL // len(POOL_WINDOWS)
POOL_HIST = 15
N_HEADS = 8
HEAD_DIM = D_MODEL // (2 * N_HEADS)
HEAD_W = 2 * HEAD_DIM
NUM_BUCKETS = 32
MAX_DISTANCE = 128
CHUNK = 64
EPS = 1e-6
NEG_INF = -1e30

ADA_ROWS = 16
ADA_PROMPT_ROW0 = 8

VMEM_LIMIT_BYTES = 60 * 1024 * 1024
SUBLANES = 8

FFN_ROWS = 512
FFN_CHUNK = 1408
LRU_ROWS = 256
POOL_ROWS = 512
QKV_ROWS = 512
ATT_TQ = 256
ATT_TK = 256
OUT_ROWS = 512


def _cparams():
    return pltpu.CompilerParams(
        dimension_semantics=("arbitrary", "arbitrary"),
        vmem_limit_bytes=VMEM_LIMIT_BYTES,
    )


def _sigmoid(x):
    return 1.0 / (1.0 + jnp.exp(-x))


def _modulate(x, g, shift, scale):
    ms = jnp.mean(x * x, axis=-1, keepdims=True)
    y = x * lax.rsqrt(ms + EPS)
    return (y * g) * (1.0 + scale) + shift


def _dot(a, b):
    return jnp.dot(a, b, preferred_element_type=F32)


def _dot_nt(a, b):
    return lax.dot_general(a, b, (((1,), (1,)), ((), ())), preferred_element_type=F32)


def _adaln_kernel(c_ref, w_ref, b_ref, o_ref):
    c = c_ref[...]
    s = c * _sigmoid(c)
    y = _dot(s.astype(BF16), w_ref[0].astype(BF16))
    o_ref[0, 0] = y + b_ref[0, 0]


def _adaln(c_all, ada_w, ada_b):
    n_vec = 9
    out = pl.pallas_call(
        _adaln_kernel,
        grid=(DEPTH, n_vec),
        in_specs=[
            pl.BlockSpec((ADA_ROWS, D_MODEL), lambda l, k: (0, 0)),
            pl.BlockSpec((1, D_MODEL, D_MODEL), lambda l, k: (l, 0, k)),
            pl.BlockSpec((1, 1, 1, D_MODEL), lambda l, k: (l, k, 0, 0)),
        ],
        out_specs=pl.BlockSpec((1, 1, ADA_ROWS, D_MODEL), lambda l, k: (l, k, 0, 0)),
        out_shape=jax.ShapeDtypeStruct((DEPTH, n_vec, ADA_ROWS, D_MODEL), F32),
        compiler_params=_cparams(),
        name="adaln",
    )(c_all, ada_w, ada_b.reshape(DEPTH, n_vec, 1, D_MODEL))
    return out.reshape(DEPTH, n_vec, ADA_ROWS, 1, D_MODEL)


class _Rows:
    def __init__(self, batch, seq, rows, prompt):
        self.prompt = prompt
        if prompt:
            assert seq % rows == 0
            self.nb, self.r = 1, rows
            self.grid = (batch, seq // rows)
        else:
            self.nb, self.r = batch, seq
            self.grid = (1, 1)

    def act(self, width):
        return pl.BlockSpec((self.nb, self.r, width), lambda b, t: (b, t, 0))

    def per_seq(self, rows, width):
        return pl.BlockSpec((self.nb, rows, width), lambda b, t: (b, 0, 0))

    def mod(self, layer, k):
        if self.prompt:
            return pl.BlockSpec((1, 1, 1, 1, D_MODEL),
                                lambda b, t: (layer, k, ADA_PROMPT_ROW0 + b, 0, 0))
        return pl.BlockSpec((1, 1, self.nb, 1, D_MODEL), lambda b, t: (layer, k, 0, 0, 0))


def _const_spec(shape, index):
    return pl.BlockSpec(shape, lambda b, t: index)


def _ffn_kernel(x_ref, sh_ref, sc_ref, gt_ref, g_ref, win_ref, wout_ref, o_ref):
    x = x_ref[...]
    nb, r, _ = x.shape
    u = _modulate(x, g_ref[0, 0], sh_ref[0, 0], sc_ref[0, 0])
    ub = u.reshape(nb * r, D_MODEL).astype(BF16)
    acc = None
    for c in range(D_FF // FFN_CHUNK):
        lo = c * FFN_CHUNK
        a = _dot(ub, win_ref[0, 0, :, lo:lo + FFN_CHUNK])
        b = _dot(ub, win_ref[0, 0, :, D_FF + lo:D_FF + lo + FFN_CHUNK])
        h = ((a * _sigmoid(a)) * b).astype(BF16)
        y = _dot(h, wout_ref[0, 0, lo:lo + FFN_CHUNK, :])
        acc = y if acc is None else acc + y
    o_ref[...] = x + (0.5 * gt_ref[0, 0]) * acc.reshape(nb, r, D_MODEL)


def _ffn(x, rows, mods, norm_g4, w_in, w_out, layer, which):
    k0 = 0 if which == 0 else 6
    g_idx = 0 if which == 0 else 2
    return pl.pallas_call(
        _ffn_kernel,
        grid=rows.grid,
        in_specs=[
            rows.act(D_MODEL),
            rows.mod(layer, k0), rows.mod(layer, k0 + 1), rows.mod(layer, k0 + 2),
            _const_spec((1, 1, 1, D_MODEL), (layer, g_idx, 0, 0)),
            _const_spec((1, 1, D_MODEL, 2 * D_FF), (layer, which, 0, 0)),
            _const_spec((1, 1, D_FF, D_MODEL), (layer, which, 0, 0)),
        ],
        out_specs=rows.act(D_MODEL),
        out_shape=jax.ShapeDtypeStruct(x.shape, F32),
        compiler_params=_cparams(),
        name="ffn",
    )(x, mods, mods, mods, norm_g4, w_in, w_out)


def _shift_rows(v, d, fill):
    rolled = pltpu.roll(v, d, axis=1)
    row = lax.broadcasted_iota(jnp.int32, v.shape, 1)
    return jnp.where(row >= d, rolled, fill)


def _linear_scan(a, b):
    r = a.shape[1]
    d = 1
    while d < r:
        a_sh = _shift_rows(a, d, 1.0)
        b_sh = _shift_rows(b, d, 0.0)
        b = a * b_sh + b
        a = a * a_sh
        d *= 2
    return a, b


def _lru_kernel(x_ref, sh_ref, sc_ref, gt_ref, g_ref, win_ref, cw_ref, cb_ref, gw_ref, gab_ref,
                gxb_ref, lam_ref, wout_ref, conv0_ref, h0_ref,
                o_ref, hlast_ref, tail_ref, h_scr, tail_scr):
    @pl.when(pl.program_id(1) == 0)
    def _():
        h_scr[...] = h0_ref[...]
        tail_scr[...] = conv0_ref[...]

    x = x_ref[...]
    nb, r, _ = x.shape
    C = D_RNN
    u = _modulate(x, g_ref[0, 0], sh_ref[0, 0], sc_ref[0, 0])
    ub = u.reshape(nb * r, D_MODEL).astype(BF16)
    gate_br = _dot(ub, win_ref[0, :, :C])
    x_br = _dot(ub, win_ref[0, :, C:]).reshape(nb, r, C)

    xp = jnp.concatenate([tail_scr[...], x_br], axis=1)
    cw = cw_ref[0]
    off = SUBLANES - (CONV_W - 1)
    xc = cb_ref[0] + cw[0:1] * xp[:, off:off + r]
    for k in range(1, CONV_W):
        xc = xc + cw[k:k + 1] * xp[:, off + k:off + k + r]
    new_tail = x_br[:, r - SUBLANES:, :]
    tail_scr[...] = new_tail
    tail_ref[...] = new_tail

    xc2 = xc.reshape(nb * r, C)
    gates = _dot(xc2.astype(BF16), gw_ref[0])
    rg = _sigmoid(gates[:, :C] + gab_ref[0])
    ig = _sigmoid(gates[:, C:] + gxb_ref[0])
    nl = -lam_ref[0]
    softplus = jnp.maximum(nl, 0.0) + jnp.log1p(jnp.exp(-jnp.abs(nl)))
    log_a = (-LRU_C * rg) * softplus
    a = jnp.exp(log_a)
    th = jnp.tanh(log_a)
    one_minus_a2 = (-2.0 * th) / (1.0 - th)
    b_in = jnp.sqrt(one_minus_a2) * (ig * xc2)

    a_cum, b_cum = _linear_scan(a.reshape(nb, r, C), b_in.reshape(nb, r, C))
    h = a_cum * h_scr[...] + b_cum
    h_last = h[:, r - 1:r, :]
    h_scr[...] = h_last
    hlast_ref[...] = h_last

    gb = gate_br
    gelu = 0.5 * gb * (1.0 + jnp.tanh(math.sqrt(2.0 / math.pi) * (gb + 0.044715 * (gb * gb * gb))))
    y = _dot((gelu * h.reshape(nb * r, C)).astype(BF16), wout_ref[0])
    o_ref[...] = x + gt_ref[0, 0] * y.reshape(nb, r, D_MODEL)


def _lru(x, rows, mods, norm_g4, layer, j, w_in, conv_w, conv_b, gate_w, ga_b, gx_b, lam, w_out,
         conv0, h0):
    nbt = x.shape[0]
    C = D_RNN
    vec = lambda: _const_spec((1, 1, C), (j, 0, 0))
    out, h_last, tail = pl.pallas_call(
        _lru_kernel,
        grid=rows.grid,
        in_specs=[
            rows.act(D_MODEL),
            rows.mod(layer, 3), rows.mod(layer, 4), rows.mod(layer, 5),
            _const_spec((1, 1, 1, D_MODEL), (layer, 1, 0, 0)),
            _const_spec((1, D_MODEL, 2 * C), (j, 0, 0)),
            _const_spec((1, CONV_W, C), (j, 0, 0)),
            vec(),
            _const_spec((1, C, 2 * C), (j, 0, 0)),
            vec(), vec(), vec(),
            _const_spec((1, C, D_MODEL), (j, 0, 0)),
            rows.per_seq(SUBLANES, C),
            rows.per_seq(1, C),
        ],
        out_specs=[rows.act(D_MODEL), rows.per_seq(1, C), rows.per_seq(SUBLANES, C)],
        out_shape=[
            jax.ShapeDtypeStruct(x.shape, F32),
            jax.ShapeDtypeStruct((nbt, 1, C), F32),
            jax.ShapeDtypeStruct((nbt, SUBLANES, C), F32),
        ],
        scratch_shapes=[pltpu.VMEM((rows.nb, 1, C), F32), pltpu.VMEM((rows.nb, SUBLANES, C), F32)],
        compiler_params=_cparams(),
        name="lru",
    )(x, mods, mods, mods, norm_g4, w_in, conv_w, conv_b, gate_w, ga_b, gx_b, lam, w_out, conv0, h0)
    return out, h_last[:, 0], tail[:, SUBLANES - (CONV_W - 1):]


POOL_PAD = 16


def _pool_kernel(x_ref, sh_ref, sc_ref, gt_ref, g_ref, hist0_ref, w_ref, b_ref, ps_ref,
                 o_ref, st_ref, hist_scr, *, t_base):
    t = pl.program_id(1)

    @pl.when(t == 0)
    def _():
        hist_scr[...] = hist0_ref[...]

    x = x_ref[...]
    nb, r, _ = x.shape
    u = _modulate(x, g_ref[0, 0], sh_ref[0, 0], sc_ref[0, 0])
    ext = jnp.concatenate([hist_scr[...], u], axis=1)
    new_hist = u[:, r - POOL_PAD:, :]
    hist_scr[...] = new_hist
    st_ref[...] = new_hist

    row = lax.broadcasted_iota(jnp.int32, (1, r, 1), 1)
    t_glob = t_base + t * r + row
    gate = gt_ref[0, 0]
    s = ext
    for g, wnd in enumerate(POOL_WINDOWS):
        lo = g * POOL_GW
        s = s[:, :, (POOL_GW if g > 0 else 0):]
        s = s + pltpu.roll(s, wnd // 2, axis=1)
        cnt = jnp.minimum(t_glob + 1, wnd).astype(F32)
        mean = s[:, POOL_PAD:, :POOL_GW] / cnt
        d = mean - u[:, :, lo:lo + POOL_GW]
        y = _dot(d.reshape(nb * r, POOL_GW).astype(BF16), w_ref[0, g]).reshape(nb, r, POOL_GW)
        y = (y + b_ref[0, :, lo:lo + POOL_GW]) * ps_ref[0, :, lo:lo + POOL_GW]
        o_ref[:, :, lo:lo + POOL_GW] = x[:, :, lo:lo + POOL_GW] + gate[:, :, lo:lo + POOL_GW] * y


def _pool(x, rows, mods, norm_g4, layer, j, hist0, w, b, scale, t_base):
    nbt = x.shape[0]
    out, st = pl.pallas_call(
        functools.partial(_pool_kernel, t_base=t_base),
        grid=rows.grid,
        in_specs=[
            rows.act(D_MODEL),
            rows.mod(layer, 3), rows.mod(layer, 4), rows.mod(layer, 5),
            _const_spec((1, 1, 1, D_MODEL), (layer, 1, 0, 0)),
            rows.per_seq(POOL_PAD, D_MODEL),
            _const_spec((1, len(POOL_WINDOWS), POOL_GW, POOL_GW), (j, 0, 0, 0)),
            _const_spec((1, 1, D_MODEL), (j, 0, 0)),
            _const_spec((1, 1, D_MODEL), (j, 0, 0)),
        ],
        out_specs=[rows.act(D_MODEL), rows.per_seq(POOL_PAD, D_MODEL)],
        out_shape=[jax.ShapeDtypeStruct(x.shape, F32),
                   jax.ShapeDtypeStruct((nbt, POOL_PAD, D_MODEL), F32)],
        scratch_shapes=[pltpu.VMEM((rows.nb, POOL_PAD, D_MODEL), F32)],
        compiler_params=_cparams(),
        name="pool",
    )(x, mods, mods, mods, norm_g4, hist0, w, b, scale)
    return out, st[:, POOL_PAD - POOL_HIST:]


def _qkv_kernel(x_ref, sh_ref, sc_ref, g_ref, w_ref, qg_ref, kg_ref, seg_ref,
                q_ref, kf_ref, kb_ref, vf_ref, vb_ref):
    x = x_ref[...]
    nb, r, _ = x.shape
    u = _modulate(x, g_ref[0, 0], sh_ref[0, 0], sc_ref[0, 0])
    ub = u.reshape(nb * r, D_MODEL).astype(BF16)
    q = _dot(ub, w_ref[0, :, :D_MODEL])
    k = _dot(ub, w_ref[0, :, D_MODEL:2 * D_MODEL])
    v = _dot(ub, w_ref[0, :, 2 * D_MODEL:])

    def head_norm(z, gain):
        ms = _dot((z * z).astype(BF16), seg_ref[...])
        return (z * lax.rsqrt(ms + EPS)) * gain

    qn = head_norm(q, qg_ref[...]) * (HEAD_DIM ** -0.5)
    kn = head_norm(k, kg_ref[...])
    lane = lax.broadcasted_iota(jnp.int32, (1, D_MODEL), 1)
    first = (lane % HEAD_W) < HEAD_DIM
    q_ref[:, 0] = jnp.where(first, qn, 0.0).astype(BF16).reshape(nb, r, D_MODEL)
    q_ref[:, 1] = jnp.where(first, 0.0, qn).astype(BF16).reshape(nb, r, D_MODEL)
    kf_ref[...] = kn.reshape(nb, r, D_MODEL)
    kb_ref[...] = kn.astype(BF16).reshape(nb, r, D_MODEL)
    vf_ref[...] = v.reshape(nb, r, D_MODEL)
    vb_ref[...] = v.astype(BF16).reshape(nb, r, D_MODEL)


def _qkv(x, rows, mods, norm_g4, layer, j, w_in, qg, kg, seg):
    nbt, seq, _ = x.shape
    q_spec = pl.BlockSpec((rows.nb, 2, rows.r, D_MODEL), lambda b, t: (b, 0, t, 0))
    return pl.pallas_call(
        _qkv_kernel,
        grid=rows.grid,
        in_specs=[
            rows.act(D_MODEL),
            rows.mod(layer, 3), rows.mod(layer, 4),
            _const_spec((1, 1, 1, D_MODEL), (layer, 1, 0, 0)),
            _const_spec((1, D_MODEL, 3 * D_MODEL), (j, 0, 0)),
            _const_spec((1, D_MODEL), (0, 0)),
            _const_spec((1, D_MODEL), (0, 0)),
            _const_spec((D_MODEL, D_MODEL), (0, 0)),
        ],
        out_specs=[q_spec, rows.act(D_MODEL), rows.act(D_MODEL), rows.act(D_MODEL),
                   rows.act(D_MODEL)],
        out_shape=[
            jax.ShapeDtypeStruct((nbt, 2, seq, D_MODEL), BF16),
            jax.ShapeDtypeStruct(x.shape, F32),
            jax.ShapeDtypeStruct(x.shape, BF16),
            jax.ShapeDtypeStruct(x.shape, F32),
            jax.ShapeDtypeStruct(x.shape, BF16),
        ],
        compiler_params=_cparams(),
        name="qkv",
    )(x, mods, mods, norm_g4, w_in, qg, kg, seg)


_BUCKET_EDGES = (12, 16, 23, 32, 46, 64, 91)
_FAR_BUCKET = NUM_BUCKETS // 2 - 1


def _bias_kernel(rb_ref, o_ref, *, rel0, mask_from):
    h = pl.program_id(0)
    _, R, W = o_ref.shape
    i = lax.broadcasted_iota(jnp.int32, (R, W), 0)
    jj = lax.broadcasted_iota(jnp.int32, (R, W), 1)
    rel = jj - i + rel0
    n = jnp.abs(rel)
    large = jnp.full((R, W), NUM_BUCKETS // 4, jnp.int32)
    for edge in _BUCKET_EDGES:
        large = large + jnp.where(n >= edge, 1, 0)
    bucket = jnp.where(n < NUM_BUCKETS // 4, n, large) + jnp.where(rel > 0, NUM_BUCKETS // 2, 0)
    out = jnp.zeros((R, W), F32)
    for bidx in range(NUM_BUCKETS):
        out = jnp.where(bucket == bidx, rb_ref[bidx, h], out)
    out = out - rb_ref[_FAR_BUCKET, h]
    if mask_from is not None:
        visible = ((jj - mask_from) // CHUNK <= i // CHUNK) | (jj < mask_from)
        out = jnp.where(visible, out, NEG_INF)
    o_ref[0] = out


def _bias_tile(rel_bias, rows, width, rel0, mask_from):
    return pl.pallas_call(
        functools.partial(_bias_kernel, rel0=rel0, mask_from=mask_from),
        grid=(N_HEADS,),
        in_specs=[pl.BlockSpec(memory_space=pltpu.SMEM)],
        out_specs=pl.BlockSpec((1, rows, width), lambda h: (h, 0, 0)),
        out_shape=jax.ShapeDtypeStruct((N_HEADS, rows, width), F32),
        name="bias_tile",
    )(rel_bias)


def _diff_lambda(lp, lam_init):
    s1 = jnp.sum(lp[0:1] * lp[1:2], axis=-1, keepdims=True)
    s2 = jnp.sum(lp[2:3] * lp[3:4], axis=-1, keepdims=True)
    return jnp.exp(s1) - jnp.exp(s2) + lam_init


def _attn_finish(acc, l, nq, lam, subg, lam_init):
    o = acc / l
    o = o[:nq] - lam * o[nq:]
    ms = jnp.mean(o * o, axis=-1, keepdims=True)
    return ((o * lax.rsqrt(ms + EPS)) * subg) * (1.0 - lam_init)


def _attn_prompt_kernel(q_ref, k_ref, v_ref, bias_ref, lam_ref, subg_ref, o_ref,
                        m_scr, l_scr, acc_scr, *, lam_init):
    qi = pl.program_id(2)
    tq, tk = ATT_TQ, ATT_TK
    qq = q_ref[0].reshape(2 * tq, HEAD_W)
    m_scr[...] = jnp.full(m_scr.shape, NEG_INF, F32)
    l_scr[...] = jnp.zeros(l_scr.shape, F32)
    acc_scr[...] = jnp.zeros(acc_scr.shape, F32)

    def step(kt, bias):
        start = pl.multiple_of(kt * tk, tk)
        k = k_ref[0, pl.ds(start, tk), :]
        v = v_ref[0, pl.ds(start, tk), :]
        s = _dot_nt(qq, k)
        if bias is not None:
            s = (s.reshape(2, tq, tk) + bias[None]).reshape(2 * tq, tk)
        m_old = m_scr[...]
        m_new = jnp.maximum(m_old, jnp.max(s, axis=-1, keepdims=True))
        alpha = jnp.exp(m_old - m_new)
        p = jnp.exp(s - m_new)
        l_scr[...] = alpha * l_scr[...] + jnp.sum(p, axis=-1, keepdims=True)
        acc_scr[...] = alpha * acc_scr[...] + _dot(p.astype(BF16), v)
        m_scr[...] = m_new

    def far(kt, carry):
        step(kt, None)
        return carry

    lax.fori_loop(0, jnp.maximum(qi - 1, 0), far, 0)

    @pl.when(qi > 0)
    def _():
        step(qi - 1, bias_ref[0, :, :tk])

    step(qi, bias_ref[0, :, tk:])
    lam = _diff_lambda(lam_ref[0], lam_init)
    o_ref[0] = _attn_finish(acc_scr[...], l_scr[...], tq, lam, subg_ref[...], lam_init).astype(BF16)


def _attn_prompt(q, kb, vb, bias, lam_p, subg, j, lam_init):
    nbt, _, seq, _ = q.shape
    tq = ATT_TQ
    return pl.pallas_call(
        functools.partial(_attn_prompt_kernel, lam_init=lam_init),
        grid=(nbt, N_HEADS, seq // tq),
        in_specs=[
            pl.BlockSpec((1, 2, tq, HEAD_W), lambda b, h, i: (b, 0, i, h)),
            pl.BlockSpec((1, seq, HEAD_W), lambda b, h, i: (b, 0, h)),
            pl.BlockSpec((1, seq, HEAD_W), lambda b, h, i: (b, 0, h)),
            pl.BlockSpec((1, tq, 2 * ATT_TK), lambda b, h, i: (h, 0, 0)),
            pl.BlockSpec((1, 4, HEAD_DIM), lambda b, h, i: (j, 0, 0)),
            pl.BlockSpec((1, HEAD_W), lambda b, h, i: (0, 0)),
        ],
        out_specs=pl.BlockSpec((1, tq, HEAD_W), lambda b, h, i: (b, i, h)),
        out_shape=jax.ShapeDtypeStruct((nbt, seq, D_MODEL), BF16),
        scratch_shapes=[pltpu.VMEM((2 * tq, 1), F32), pltpu.VMEM((2 * tq, 1), F32),
                        pltpu.VMEM((2 * tq, HEAD_W), F32)],
        compiler_params=pltpu.CompilerParams(
            dimension_semantics=("arbitrary", "arbitrary", "arbitrary"),
            vmem_limit_bytes=VMEM_LIMIT_BYTES),
        name="attn_prompt",
    )(q, kb, vb, bias, lam_p, subg)


def _attn_sample_kernel(q_ref, ck_ref, cv_ref, kn_ref, vn_ref, bc_ref, bn_ref, lam_ref, subg_ref,
                        o_ref, *, lam_init):
    nq = q_ref.shape[2]
    lam = _diff_lambda(lam_ref[0], lam_init)
    for h in range(N_HEADS):
        cols = slice(h * HEAD_W, (h + 1) * HEAD_W)
        qq = q_ref[0, :, :, cols].reshape(2 * nq, HEAD_W)
        kc = ck_ref[0, :, cols].astype(BF16)
        vc = cv_ref[0, :, cols].astype(BF16)
        kn = kn_ref[0, :, cols]
        vn = vn_ref[0, :, cols]
        past = kc.shape[0]
        s_c = (_dot_nt(qq, kc).reshape(2, nq, past) + bc_ref[h][None]).reshape(2 * nq, past)
        s_n = (_dot_nt(qq, kn).reshape(2, nq, nq) + bn_ref[h][None]).reshape(2 * nq, nq)
        m = jnp.maximum(jnp.max(s_c, axis=-1, keepdims=True), jnp.max(s_n, axis=-1, keepdims=True))
        p_c = jnp.exp(s_c - m)
        p_n = jnp.exp(s_n - m)
        l = jnp.sum(p_c, axis=-1, keepdims=True) + jnp.sum(p_n, axis=-1, keepdims=True)
        acc = _dot(p_c.astype(BF16), vc) + _dot(p_n.astype(BF16), vn)
        o_ref[0, :, cols] = _attn_finish(acc, l, nq, lam, subg_ref[...], lam_init).astype(BF16)


def _attn_sample(q, ck, cv, kb, vb, bias_c, bias_n, lam_p, subg, j, lam_init):
    nbt, _, nq, _ = q.shape
    past = ck.shape[1]
    return pl.pallas_call(
        functools.partial(_attn_sample_kernel, lam_init=lam_init),
        grid=(nbt,),
        in_specs=[
            pl.BlockSpec((1, 2, nq, D_MODEL), lambda b: (b, 0, 0, 0)),
            pl.BlockSpec((1, past, D_MODEL), lambda b: (b, 0, 0)),
            pl.BlockSpec((1, past, D_MODEL), lambda b: (b, 0, 0)),
            pl.BlockSpec((1, nq, D_MODEL), lambda b: (b, 0, 0)),
            pl.BlockSpec((1, nq, D_MODEL), lambda b: (b, 0, 0)),
            pl.BlockSpec((N_HEADS, nq, past), lambda b: (0, 0, 0)),
            pl.BlockSpec((N_HEADS, nq, nq), lambda b: (0, 0, 0)),
            pl.BlockSpec((1, 4, HEAD_DIM), lambda b: (j, 0, 0)),
            pl.BlockSpec((1, HEAD_W), lambda b: (0, 0)),
        ],
        out_specs=pl.BlockSpec((1, nq, D_MODEL), lambda b: (b, 0, 0)),
        out_shape=jax.ShapeDtypeStruct((nbt, nq, D_MODEL), BF16),
        compiler_params=pltpu.CompilerParams(
            dimension_semantics=("arbitrary",), vmem_limit_bytes=VMEM_LIMIT_BYTES),
        name="attn_sample",
    )(q, ck, cv, kb, vb, bias_c, bias_n, lam_p, subg)


def _proj_res_kernel(x_ref, a_ref, gt_ref, w_ref, o_ref):
    x = x_ref[...]
    nb, r, _ = x.shape
    a = a_ref[...].reshape(nb * r, a_ref.shape[-1])
    y = _dot(a, w_ref[0])
    o_ref[...] = x + gt_ref[0, 0] * y.reshape(nb, r, D_MODEL)


def _proj_res(x, a, rows, mods, layer, j, w):
    return pl.pallas_call(
        _proj_res_kernel,
        grid=rows.grid,
        in_specs=[
            rows.act(D_MODEL), rows.act(a.shape[-1]), rows.mod(layer, 5),
            _const_spec((1,) + w.shape[1:], (j, 0, 0)),
        ],
        out_specs=rows.act(D_MODEL),
        out_shape=jax.ShapeDtypeStruct(x.shape, F32),
        compiler_params=_cparams(),
        name="proj_res",
    )(x, a, mods, w)


def kernel(x_prompt, x_sample, c_prompt, c_sample, state_lru_h, state_lru_conv, state_pool, cache_k, cache_v, ada_w, ada_b, norm_g, ffn_w_in, ffn_w_out, lru_w_in, lru_conv_w, lru_conv_b, lru_ga_w, lru_ga_b, lru_gx_w, lru_gx_b, lru_lambda, lru_w_out, pool_w, pool_b, pool_scale, attn_w_in, attn_q_g, attn_k_g, attn_lambda, attn_sub_g, attn_w_out, rel_bias):
    B, S, _ = x_prompt.shape
    SB, T, _ = x_sample.shape
    P = cache_k.shape[2]
    assert SB == ADA_PROMPT_ROW0 and ADA_PROMPT_ROW0 + B <= ADA_ROWS
    n_a, n_b, n_c = lru_w_in.shape[0], pool_w.shape[0], attn_w_in.shape[0]

    c_all = jnp.concatenate(
        [c_sample, c_prompt, jnp.zeros((ADA_ROWS - SB - B, D_MODEL), F32)], axis=0)
    mods = _adaln(c_all, ada_w, ada_b)
    norm_g4 = norm_g.reshape(DEPTH, 3, 1, D_MODEL)

    ffn_in_b = ffn_w_in.astype(BF16)
    ffn_out_b = ffn_w_out.astype(BF16)
    lru_in_b = lru_w_in.astype(BF16)
    lru_out_b = lru_w_out.astype(BF16)
    eye = jnp.eye(LRU_BLOCKS, dtype=F32)

    def block_diag(w):
        full = jnp.einsum('nhij,hg->nhigj', w, eye)
        return full.reshape(w.shape[0], D_RNN, D_RNN)

    lru_gate_b = jnp.concatenate([block_diag(lru_ga_w), block_diag(lru_gx_w)], axis=-1).astype(BF16)
    vec3 = lambda v: v.reshape(v.shape[0], 1, v.shape[-1])
    pool_w_b = pool_w.astype(BF16)
    attn_in_b = attn_w_in.astype(BF16)
    attn_out_b = attn_w_out.astype(BF16)
    seg = jnp.kron(jnp.eye(D_MODEL // HEAD_DIM, dtype=F32),
                   jnp.full((HEAD_DIM, HEAD_DIM), 1.0 / HEAD_DIM, F32)).astype(BF16)

    rows_p = {r: _Rows(B, S, r, True) for r in {FFN_ROWS, LRU_ROWS, POOL_ROWS, QKV_ROWS, OUT_ROWS}}
    rows_s = _Rows(SB, T, T, False)

    xp, xs = x_prompt, x_sample
    h_p, h_s, cv_p, cv_s, pl_p, pl_s, k_p, v_p, k_s, v_s = ([] for _ in range(10))
    for l in range(DEPTH):
        kind, j = l % 3, l // 3
        xp = _ffn(xp, rows_p[FFN_ROWS], mods, norm_g4, ffn_in_b, ffn_out_b, l, 0)
        xs = _ffn(xs, rows_s, mods, norm_g4, ffn_in_b, ffn_out_b, l, 0)
        if kind == 0:
            prm = (lru_in_b, lru_conv_w, vec3(lru_conv_b), lru_gate_b, vec3(lru_ga_b),
                   vec3(lru_gx_b), vec3(lru_lambda), lru_out_b)
            xp, hp, bp = _lru(xp, rows_p[LRU_ROWS], mods, norm_g4, l, j, *prm,
                              jnp.zeros((B, SUBLANES, D_RNN), F32), jnp.zeros((B, 1, D_RNN), F32))
            conv0 = jnp.pad(state_lru_conv[j], ((0, 0), (SUBLANES - (CONV_W - 1), 0), (0, 0)))
            xs, hs, bs = _lru(xs, rows_s, mods, norm_g4, l, j, *prm, conv0,
                              state_lru_h[j][:, None, :])
            h_p.append(hp), h_s.append(hs), cv_p.append(bp), cv_s.append(bs)
        elif kind == 1:
            prm = (pool_w_b, vec3(pool_b), vec3(pool_scale))
            xp, sp = _pool(xp, rows_p[POOL_ROWS], mods, norm_g4, l, j,
                           jnp.zeros((B, POOL_PAD, D_MODEL), F32), *prm, 0)
            hist0 = jnp.pad(state_pool[j], ((0, 0), (POOL_PAD - POOL_HIST, 0), (0, 0)))
            xs, ss = _pool(xs, rows_s, mods, norm_g4, l, j, hist0, *prm, POOL_HIST)
            pl_p.append(sp), pl_s.append(ss)
        else:
            lam_init = 0.8 - 0.6 * math.exp(-0.3 * l)
            qg = jnp.tile(attn_q_g[j], D_MODEL // HEAD_DIM)[None]
            kg = jnp.tile(attn_k_g[j], D_MODEL // HEAD_DIM)[None]
            subg = attn_sub_g[j][None]
            q, kf, kb, vf, vb = _qkv(xp, rows_p[QKV_ROWS], mods, norm_g4, l, j, attn_in_b, qg, kg, seg)
            bias_p = _bias_tile(rel_bias, ATT_TQ, 2 * ATT_TK, -ATT_TK, ATT_TK)
            o = _attn_prompt(q, kb, vb, bias_p, attn_lambda, subg, j, lam_init)
            xp = _proj_res(xp, o, rows_p[OUT_ROWS], mods, l, j, attn_out_b)
            k_p.append(kf.reshape(B, S, N_HEADS, HEAD_W)), v_p.append(vf.reshape(B, S, N_HEADS, HEAD_W))
            q, kf, kb, vf, vb = _qkv(xs, rows_s, mods, norm_g4, l, j, attn_in_b, qg, kg, seg)
            bias_c = _bias_tile(rel_bias, T, P, -P, None)
            bias_n = _bias_tile(rel_bias, T, T, 0, None)
            o = _attn_sample(q, cache_k[j].reshape(SB, P, D_MODEL), cache_v[j].reshape(SB, P, D_MODEL),
                             kb, vb, bias_c, bias_n, attn_lambda, subg, j, lam_init)
            xs = _proj_res(xs, o, rows_s, mods, l, j, attn_out_b)
            k_s.append(kf.reshape(SB, T, N_HEADS, HEAD_W)), v_s.append(vf.reshape(SB, T, N_HEADS, HEAD_W))
        xp = _ffn(xp, rows_p[FFN_ROWS], mods, norm_g4, ffn_in_b, ffn_out_b, l, 1)
        xs = _ffn(xs, rows_s, mods, norm_g4, ffn_in_b, ffn_out_b, l, 1)
    return (xp, xs, jnp.stack(h_p), jnp.stack(h_s), jnp.stack(cv_p), jnp.stack(cv_s),
            jnp.stack(pl_p), jnp.stack(pl_s), jnp.stack(k_p), jnp.stack(v_p),
            jnp.stack(k_s), jnp.stack(v_s))
```

```python
import functools
import math

import jax
import jax.numpy as jnp
from jax import lax
from jax.experimental import pallas as pl
from jax.experimental.pallas import tpu as pltpu

F32 = jnp.float32
BF16 = jnp.bfloat16

D_MODEL = 1024
DEPTH = 4
D_FF = 2816
D_RNN = 1280
LRU_BLOCKS = 16
LRU_BS = D_RNN // LRU_BLOCKS
CONV_W = 4
LRU_C = 8.0
POOL_WINDOWS = (2, 4, 8, 16)
POOL_GW = D_MODEL // len(POOL_WINDOWS)
POOL_HIST = 15
N_HEADS = 8
HEAD_DIM = D_MODEL // (2 * N_HEADS)
HEAD_W = 2 * HEAD_DIM
NUM_BUCKETS = 32
MAX_DISTANCE = 128
CHUNK = 64
EPS = 1e-6
NEG_INF = -1e30

ADA_ROWS = 16
ADA_PROMPT_ROW0 = 8

VMEM_LIMIT_BYTES = 60 * 1024 * 1024
SUBLANES = 8

FFN_ROWS = 512
FFN_CHUNK = 1408
LRU_ROWS = 256
POOL_ROWS = 512
QKV_ROWS = 512
ATT_TQ = 256
ATT_TK = 256
OUT_ROWS = 512


def _cparams():
    return pltpu.CompilerParams(
        dimension_semantics=("arbitrary", "arbitrary"),
        vmem_limit_bytes=VMEM_LIMIT_BYTES,
    )


def _sigmoid(x):
    return 1.0 / (1.0 + jnp.exp(-x))


def _modulate(x, g, shift, scale):
    ms = jnp.mean(x * x, axis=-1, keepdims=True)
    y = x * lax.rsqrt(ms + EPS)
    return (y * g) * (1.0 + scale) + shift


def _dot(a, b):
    return jnp.dot(a, b, preferred_element_type=F32)


def _dot_nt(a, b):
    return lax.dot_general(a, b, (((1,), (1,)), ((), ())), preferred_element_type=F32)


def _adaln_kernel(c_ref, w_ref, b_ref, o_ref):
    c = c_ref[...]
    s = c * _sigmoid(c)
    y = _dot(s.astype(BF16), w_ref[0].astype(BF16))
    o_ref[0, 0] = y + b_ref[0, 0]


def _adaln(c_all, ada_w, ada_b):
    n_vec = 9
    out = pl.pallas_call(
        _adaln_kernel,
        grid=(DEPTH, n_vec),
        in_specs=[
            pl.BlockSpec((ADA_ROWS, D_MODEL), lambda l, k: (0, 0)),
            pl.BlockSpec((1, D_MODEL, D_MODEL), lambda l, k: (l, 0, k)),
            pl.BlockSpec((1, 1, 1, D_MODEL), lambda l, k: (l, k, 0, 0)),
        ],
        out_specs=pl.BlockSpec((1, 1, ADA_ROWS, D_MODEL), lambda l, k: (l, k, 0, 0)),
        out_shape=jax.ShapeDtypeStruct((DEPTH, n_vec, ADA_ROWS, D_MODEL), F32),
        compiler_params=_cparams(),
        name="adaln",
    )(c_all, ada_w, ada_b.reshape(DEPTH, n_vec, 1, D_MODEL))
    return out.reshape(DEPTH, n_vec, ADA_ROWS, 1, D_MODEL)


class _Rows:
    def __init__(self, batch, seq, rows, prompt):
        self.prompt = prompt
        if prompt:
            assert seq % rows == 0
            self.nb, self.r = 1, rows
            self.grid = (batch, seq // rows)
        else:
            self.nb, self.r = batch, seq
            self.grid = (1, 1)

    def act(self, width):
        return pl.BlockSpec((self.nb, self.r, width), lambda b, t: (b, t, 0))

    def per_seq(self, rows, width):
        return pl.BlockSpec((self.nb, rows, width), lambda b, t: (b, 0, 0))

    def mod(self, layer, k):
        if self.prompt:
            return pl.BlockSpec((1, 1, 1, 1, D_MODEL),
                                lambda b, t: (layer, k, ADA_PROMPT_ROW0 + b, 0, 0))
        return pl.BlockSpec((1, 1, self.nb, 1, D_MODEL), lambda b, t: (layer, k, 0, 0, 0))


def _const_spec(shape, index):
    return pl.BlockSpec(shape, lambda b, t: index)


def _ffn_kernel(x_ref, sh_ref, sc_ref, gt_ref, g_ref, win_ref, wout_ref, o_ref):
    x = x_ref[...]
    nb, r, _ = x.shape
    u = _modulate(x, g_ref[0, 0], sh_ref[0, 0], sc_ref[0, 0])
    ub = u.reshape(nb * r, D_MODEL).astype(BF16)
    acc = None
    for c in range(D_FF // FFN_CHUNK):
        lo = c * FFN_CHUNK
        a = _dot(ub, win_ref[0, 0, :, lo:lo + FFN_CHUNK])
        b = _dot(ub, win_ref[0, 0, :, D_FF + lo:D_FF + lo + FFN_CHUNK])
        h = ((a * _sigmoid(a)) * b).astype(BF16)
        y = _dot(h, wout_ref[0, 0, lo:lo + FFN_CHUNK, :])
        acc = y if acc is None else acc + y
    o_ref[...] = x + (0.5 * gt_ref[0, 0]) * acc.reshape(nb, r, D_MODEL)


def _ffn(x, rows, mods, norm_g4, w_in, w_out, layer, which):
    k0 = 0 if which == 0 else 6
    g_idx = 0 if which == 0 else 2
    return pl.pallas_call(
        _ffn_kernel,
        grid=rows.grid,
        in_specs=[
            rows.act(D_MODEL),
            rows.mod(layer, k0), rows.mod(layer, k0 + 1), rows.mod(layer, k0 + 2),
            _const_spec((1, 1, 1, D_MODEL), (layer, g_idx, 0, 0)),
            _const_spec((1, 1, D_MODEL, 2 * D_FF), (layer, which, 0, 0)),
            _const_spec((1, 1, D_FF, D_MODEL), (layer, which, 0, 0)),
        ],
        out_specs=rows.act(D_MODEL),
        out_shape=jax.ShapeDtypeStruct(x.shape, F32),
        compiler_params=_cparams(),
        name="ffn",
    )(x, mods, mods, mods, norm_g4, w_in, w_out)


def _shift_rows(v, d, fill):
    rolled = pltpu.roll(v, d, axis=1)
    row = lax.broadcasted_iota(jnp.int32, v.shape, 1)
    return jnp.where(row >= d, rolled, fill)


def _linear_scan(a, b):
    r = a.shape[1]
    d = 1
    while d < r:
        a_sh = _shift_rows(a, d, 1.0)
        b_sh = _shift_rows(b, d, 0.0)
        b = a * b_sh + b
        a = a * a_sh
        d *= 2
    return a, b


def _lru_kernel(x_ref, sh_ref, sc_ref, gt_ref, g_ref, win_ref, cw_ref, cb_ref, gw_ref, gab_ref,
                gxb_ref, lam_ref, wout_ref, conv0_ref, h0_ref,
                o_ref, hlast_ref, tail_ref, h_scr, tail_scr):
    @pl.when(pl.program_id(1) == 0)
    def _():
        h_scr[...] = h0_ref[...]
        tail_scr[...] = conv0_ref[...]

    x = x_ref[...]
    nb, r, _ = x.shape
    C = D_RNN
    u = _modulate(x, g_ref[0, 0], sh_ref[0, 0], sc_ref[0, 0])
    ub = u.reshape(nb * r, D_MODEL).astype(BF16)
    gate_br = _dot(ub, win_ref[0, :, :C])
    x_br = _dot(ub, win_ref[0, :, C:]).reshape(nb, r, C)

    xp = jnp.concatenate([tail_scr[...], x_br], axis=1)
    cw = cw_ref[0]
    off = SUBLANES - (CONV_W - 1)
    xc = cb_ref[0] + cw[0:1] * xp[:, off:off + r]
    for k in range(1, CONV_W):
        xc = xc + cw[k:k + 1] * xp[:, off + k:off + k + r]
    new_tail = x_br[:, r - SUBLANES:, :]
    tail_scr[...] = new_tail
    tail_ref[...] = new_tail

    xc2 = xc.reshape(nb * r, C)
    gates = _dot(xc2.astype(BF16), gw_ref[0])
    rg = _sigmoid(gates[:, :C] + gab_ref[0])
    ig = _sigmoid(gates[:, C:] + gxb_ref[0])
    nl = -lam_ref[0]
    softplus = jnp.maximum(nl, 0.0) + jnp.log1p(jnp.exp(-jnp.abs(nl)))
    log_a = (-LRU_C * rg) * softplus
    a = jnp.exp(log_a)
    th = jnp.tanh(log_a)
    one_minus_a2 = (-2.0 * th) / (1.0 - th)
    b_in = jnp.sqrt(one_minus_a2) * (ig * xc2)

    a_cum, b_cum = _linear_scan(a.reshape(nb, r, C), b_in.reshape(nb, r, C))
    h = a_cum * h_scr[...] + b_cum
    h_last = h[:, r - 1:r, :]
    h_scr[...] = h_last
    hlast_ref[...] = h_last

    gb = gate_br
    gelu = 0.5 * gb * (1.0 + jnp.tanh(math.sqrt(2.0 / math.pi) * (gb + 0.044715 * (gb * gb * gb))))
    y = _dot((gelu * h.reshape(nb * r, C)).astype(BF16), wout_ref[0])
    o_ref[...] = x + gt_ref[0, 0] * y.reshape(nb, r, D_MODEL)


def _lru(x, rows, mods, norm_g4, layer, j, w_in, conv_w, conv_b, gate_w, ga_b, gx_b, lam, w_out,
         conv0, h0):
    nbt = x.shape[0]
    C = D_RNN
    vec = lambda: _const_spec((1, 1, C), (j, 0, 0))
    out, h_last, tail = pl.pallas_call(
        _lru_kernel,
        grid=rows.grid,
        in_specs=[
            rows.act(D_MODEL),
            rows.mod(layer, 3), rows.mod(layer, 4), rows.mod(layer, 5),
            _const_spec((1, 1, 1, D_MODEL), (layer, 1, 0, 0)),
            _const_spec((1, D_MODEL, 2 * C), (j, 0, 0)),
            _const_spec((1, CONV_W, C), (j, 0, 0)),
            vec(),
            _const_spec((1, C, 2 * C), (j, 0, 0)),
            vec(), vec(), vec(),
            _const_spec((1, C, D_MODEL), (j, 0, 0)),
            rows.per_seq(SUBLANES, C),
            rows.per_seq(1, C),
        ],
        out_specs=[rows.act(D_MODEL), rows.per_seq(1, C), rows.per_seq(SUBLANES, C)],
        out_shape=[
            jax.ShapeDtypeStruct(x.shape, F32),
            jax.ShapeDtypeStruct((nbt, 1, C), F32),
            jax.ShapeDtypeStruct((nbt, SUBLANES, C), F32),
        ],
        scratch_shapes=[pltpu.VMEM((rows.nb, 1, C), F32), pltpu.VMEM((rows.nb, SUBLANES, C), F32)],
        compiler_params=_cparams(),
        name="lru",
    )(x, mods, mods, mods, norm_g4, w_in, conv_w, conv_b, gate_w, ga_b, gx_b, lam, w_out, conv0, h0)
    return out, h_last[:, 0], tail[:, SUBLANES - (CONV_W - 1):]


POOL_PAD = 16


def _pool_kernel(x_ref, sh_ref, sc_ref, gt_ref, g_ref, hist0_ref, w_ref, b_ref, ps_ref,
                 o_ref, st_ref, hist_scr, *, t_base):
    t = pl.program_id(1)

    @pl.when(t == 0)
    def _():
        hist_scr[...] = hist0_ref[...]

    x = x_ref[...]
    nb, r, _ = x.shape
    u = _modulate(x, g_ref[0, 0], sh_ref[0, 0], sc_ref[0, 0])
    ext = jnp.concatenate([hist_scr[...], u], axis=1)
    new_hist = u[:, r - POOL_PAD:, :]
    hist_scr[...] = new_hist
    st_ref[...] = new_hist

    row = lax.broadcasted_iota(jnp.int32, (1, r, 1), 1)
    t_glob = t_base + t * r + row
    gate = gt_ref[0, 0]
    s = ext
    for g, wnd in enumerate(POOL_WINDOWS):
        lo = g * POOL_GW
        s = s[:, :, (POOL_GW if g > 0 else 0):]
        s = s + pltpu.roll(s, wnd // 2, axis=1)
        cnt = jnp.minimum(t_glob + 1, wnd).astype(F32)
        mean = s[:, POOL_PAD:, :POOL_GW] / cnt
        d = mean - u[:, :, lo:lo + POOL_GW]
        y = _dot(d.reshape(nb * r, POOL_GW).astype(BF16), w_ref[0, g]).reshape(nb, r, POOL_GW)
        y = (y + b_ref[0, :, lo:lo + POOL_GW]) * ps_ref[0, :, lo:lo + POOL_GW]
        o_ref[:, :, lo:lo + POOL_GW] = x[:, :, lo:lo + POOL_GW] + gate[:, :, lo:lo + POOL_GW] * y


def _pool(x, rows, mods, norm_g4, layer, j, hist0, w, b, scale, t_base):
    nbt = x.shape[0]
    out, st = pl.pallas_call(
        functools.partial(_pool_kernel, t_base=t_base),
        grid=rows.grid,
        in_specs=[
            rows.act(D_MODEL),
            rows.mod(layer, 3), rows.mod(layer, 4), rows.mod(layer, 5),
            _const_spec((1, 1, 1, D_MODEL), (layer, 1, 0, 0)),
            rows.per_seq(POOL_PAD, D_MODEL),
            _const_spec((1, len(POOL_WINDOWS), POOL_GW, POOL_GW), (j, 0, 0, 0)),
            _const_spec((1, 1, D_MODEL), (j, 0, 0)),
            _const_spec((1, 1, D_MODEL), (j, 0, 0)),
        ],
        out_specs=[rows.act(D_MODEL), rows.per_seq(POOL_PAD, D_MODEL)],
        out_shape=[jax.ShapeDtypeStruct(x.shape, F32),
                   jax.ShapeDtypeStruct((nbt, POOL_PAD, D_MODEL), F32)],
        scratch_shapes=[pltpu.VMEM((rows.nb, POOL_PAD, D_MODEL), F32)],
        compiler_params=_cparams(),
        name="pool",
    )(x, mods, mods, mods, norm_g4, hist0, w, b, scale)
    return out, st[:, POOL_PAD - POOL_HIST:]


def _qkv_kernel(x_ref, sh_ref, sc_ref, g_ref, w_ref, qg_ref, kg_ref, seg_ref,
                q_ref, kf_ref, kb_ref, vf_ref, vb_ref, *, v_transposed):
    x = x_ref[...]
    nb, r, _ = x.shape
    u = _modulate(x, g_ref[0, 0], sh_ref[0, 0], sc_ref[0, 0])
    ub = u.reshape(nb * r, D_MODEL).astype(BF16)
    q = _dot(ub, w_ref[0, :, :D_MODEL])
    k = _dot(ub, w_ref[0, :, D_MODEL:2 * D_MODEL])
    v = _dot(ub, w_ref[0, :, 2 * D_MODEL:])

    def head_norm(z, gain):
        ms = _dot((z * z).astype(BF16), seg_ref[...])
        return (z * lax.rsqrt(ms + EPS)) * gain

    qn = head_norm(q, qg_ref[...]) * (HEAD_DIM ** -0.5)
    kn = head_norm(k, kg_ref[...])
    lane = lax.broadcasted_iota(jnp.int32, (1, D_MODEL), 1)
    first = (lane % HEAD_W) < HEAD_DIM
    q_ref[:, 0] = jnp.where(first, qn, 0.0).astype(BF16).reshape(nb, r, D_MODEL)
    q_ref[:, 1] = jnp.where(first, 0.0, qn).astype(BF16).reshape(nb, r, D_MODEL)
    kf_ref[...] = kn.reshape(nb, r, D_MODEL)
    kb_ref[...] = kn.astype(BF16).reshape(nb, r, D_MODEL)
    vf_ref[...] = v.reshape(nb, r, D_MODEL)
    if v_transposed:
        for h in range(N_HEADS):
            vb_ref[0, h] = v[:, h * HEAD_W:(h + 1) * HEAD_W].T.astype(BF16)
    else:
        vb_ref[...] = v.astype(BF16).reshape(nb, r, D_MODEL)


def _qkv(x, rows, mods, norm_g4, layer, j, w_in, qg, kg, seg, v_transposed):
    nbt, seq, _ = x.shape
    q_spec = pl.BlockSpec((rows.nb, 2, rows.r, D_MODEL), lambda b, t: (b, 0, t, 0))
    if v_transposed:
        assert rows.nb == 1
        vb_spec = pl.BlockSpec((1, N_HEADS, HEAD_W, rows.r), lambda b, t: (b, 0, 0, t))
        vb_shape = jax.ShapeDtypeStruct((nbt, N_HEADS, HEAD_W, seq), BF16)
    else:
        vb_spec = rows.act(D_MODEL)
        vb_shape = jax.ShapeDtypeStruct(x.shape, BF16)
    return pl.pallas_call(
        functools.partial(_qkv_kernel, v_transposed=v_transposed),
        grid=rows.grid,
        in_specs=[
            rows.act(D_MODEL),
            rows.mod(layer, 3), rows.mod(layer, 4),
            _const_spec((1, 1, 1, D_MODEL), (layer, 1, 0, 0)),
            _const_spec((1, D_MODEL, 3 * D_MODEL), (j, 0, 0)),
            _const_spec((1, D_MODEL), (0, 0)),
            _const_spec((1, D_MODEL), (0, 0)),
            _const_spec((D_MODEL, D_MODEL), (0, 0)),
        ],
        out_specs=[q_spec, rows.act(D_MODEL), rows.act(D_MODEL), rows.act(D_MODEL), vb_spec],
        out_shape=[
            jax.ShapeDtypeStruct((nbt, 2, seq, D_MODEL), BF16),
            jax.ShapeDtypeStruct(x.shape, F32),
            jax.ShapeDtypeStruct(x.shape, BF16),
            jax.ShapeDtypeStruct(x.shape, F32),
            vb_shape,
        ],
        compiler_params=_cparams(),
        name="qkv",
    )(x, mods, mods, norm_g4, w_in, qg, kg, seg)


_BUCKET_EDGES = (12, 16, 23, 32, 46, 64, 91)
_FAR_BUCKET = NUM_BUCKETS // 2 - 1


def _bias_kernel(rb_ref, o_ref, *, rel0, mask_from, keys_on_rows, q_period):
    h = pl.program_id(0)
    _, R, W = o_ref.shape
    i = lax.broadcasted_iota(jnp.int32, (R, W), 0)
    jj = lax.broadcasted_iota(jnp.int32, (R, W), 1)
    if keys_on_rows:
        key, qry = i, jj & (q_period - 1)
    else:
        key, qry = jj, i
    rel = key - qry + rel0
    n = jnp.abs(rel)
    large = jnp.full((R, W), NUM_BUCKETS // 4, jnp.int32)
    for edge in _BUCKET_EDGES:
        large = large + jnp.where(n >= edge, 1, 0)
    bucket = jnp.where(n < NUM_BUCKETS // 4, n, large) + jnp.where(rel > 0, NUM_BUCKETS // 2, 0)
    out = jnp.zeros((R, W), F32)
    for bidx in range(NUM_BUCKETS):
        out = jnp.where(bucket == bidx, rb_ref[bidx, h], out)
    out = out - rb_ref[_FAR_BUCKET, h]
    if mask_from is not None:
        visible = ((key - mask_from) // CHUNK <= qry // CHUNK) | (key < mask_from)
        out = jnp.where(visible, out, NEG_INF)
    o_ref[0] = out


def _bias_tile(rel_bias, rows, width, rel0, mask_from, keys_on_rows=False, q_period=None):
    assert q_period is None or q_period & (q_period - 1) == 0
    return pl.pallas_call(
        functools.partial(_bias_kernel, rel0=rel0, mask_from=mask_from,
                          keys_on_rows=keys_on_rows, q_period=q_period),
        grid=(N_HEADS,),
        in_specs=[pl.BlockSpec(memory_space=pltpu.SMEM)],
        out_specs=pl.BlockSpec((1, rows, width), lambda h: (h, 0, 0)),
        out_shape=jax.ShapeDtypeStruct((N_HEADS, rows, width), F32),
        name="bias_tile",
    )(rel_bias)


def _diff_lambda(lp, lam_init):
    s1 = jnp.sum(lp[0:1] * lp[1:2], axis=-1, keepdims=True)
    s2 = jnp.sum(lp[2:3] * lp[3:4], axis=-1, keepdims=True)
    return jnp.exp(s1) - jnp.exp(s2) + lam_init


def _attn_finish(acc, l, nq, lam, subg, lam_init):
    o = acc / l
    o = o[:nq] - lam * o[nq:]
    ms = jnp.mean(o * o, axis=-1, keepdims=True)
    return ((o * lax.rsqrt(ms + EPS)) * subg) * (1.0 - lam_init)


def _attn_prompt_kernel(q_ref, k_ref, vt_ref, bias_ref, lam_ref, subg_ref, o_ref,
                        qt_scr, m_scr, l_scr, acc_scr, *, lam_init):
    qi = pl.program_id(2)
    tq, tk = ATT_TQ, ATT_TK
    qt_scr[:, :tq] = q_ref[0, 0].astype(F32).T.astype(BF16)
    qt_scr[:, tq:] = q_ref[0, 1].astype(F32).T.astype(BF16)
    m_scr[...] = jnp.full(m_scr.shape, NEG_INF, F32)
    l_scr[...] = jnp.zeros(l_scr.shape, F32)
    acc_scr[...] = jnp.zeros(acc_scr.shape, F32)

    def step(kt, bias):
        start = pl.multiple_of(kt * tk, tk)
        k = k_ref[0, pl.ds(start, tk), :]
        vt = vt_ref[0, 0, :, pl.ds(start, tk)]
        s = _dot(k, qt_scr[...])
        if bias is not None:
            s = s + bias
        m_old = m_scr[...]
        m_new = jnp.maximum(m_old, jnp.max(s, axis=0, keepdims=True))
        alpha = jnp.exp(m_old - m_new)
        p = jnp.exp(s - m_new)
        l_scr[...] = alpha * l_scr[...] + jnp.sum(p, axis=0, keepdims=True)
        acc_scr[...] = alpha * acc_scr[...] + _dot(vt, p.astype(BF16))
        m_scr[...] = m_new

    n_far = jnp.maximum(qi - 1, 0)

    def far_pair(i, carry):
        step(2 * i, None)
        step(2 * i + 1, None)
        return carry

    lax.fori_loop(0, n_far // 2, far_pair, 0)

    @pl.when(n_far % 2 == 1)
    def _():
        step(n_far - 1, None)

    @pl.when(qi > 0)
    def _():
        step(qi - 1, bias_ref[0, :tk, :])

    step(qi, bias_ref[0, tk:, :])

    lam = _diff_lambda(lam_ref[0], lam_init)
    o = acc_scr[...] / l_scr[...]
    o = o[:, :tq] - lam * o[:, tq:]
    ms = jnp.mean(o * o, axis=0, keepdims=True)
    on = ((o * lax.rsqrt(ms + EPS)) * subg_ref[...]) * (1.0 - lam_init)
    o_ref[0] = on.T.astype(BF16)


def _attn_prompt(q, kb, vt, bias, lam_p, subg_col, j, lam_init):
    nbt, _, seq, _ = q.shape
    tq, tk = ATT_TQ, ATT_TK
    return pl.pallas_call(
        functools.partial(_attn_prompt_kernel, lam_init=lam_init),
        grid=(nbt, N_HEADS, seq // tq),
        in_specs=[
            pl.BlockSpec((1, 2, tq, HEAD_W), lambda b, h, i: (b, 0, i, h)),
            pl.BlockSpec((1, seq, HEAD_W), lambda b, h, i: (b, 0, h)),
            pl.BlockSpec((1, 1, HEAD_W, seq), lambda b, h, i: (b, h, 0, 0)),
            pl.BlockSpec((1, 2 * tk, 2 * tq), lambda b, h, i: (h, 0, 0)),
            pl.BlockSpec((1, 4, HEAD_DIM), lambda b, h, i: (j, 0, 0)),
            pl.BlockSpec((HEAD_W, 1), lambda b, h, i: (0, 0)),
        ],
        out_specs=pl.BlockSpec((1, tq, HEAD_W), lambda b, h, i: (b, i, h)),
        out_shape=jax.ShapeDtypeStruct((nbt, seq, D_MODEL), BF16),
        scratch_shapes=[pltpu.VMEM((HEAD_W, 2 * tq), BF16),
                        pltpu.VMEM((1, 2 * tq), F32), pltpu.VMEM((1, 2 * tq), F32),
                        pltpu.VMEM((HEAD_W, 2 * tq), F32)],
        compiler_params=pltpu.CompilerParams(
            dimension_semantics=("arbitrary", "arbitrary", "arbitrary"),
            vmem_limit_bytes=VMEM_LIMIT_BYTES),
        name="attn_prompt",
    )(q, kb, vt, bias, lam_p, subg_col)


def _attn_sample_kernel(q_ref, ck_ref, cv_ref, kn_ref, vn_ref, bc_ref, bn_ref, lam_ref, subg_ref,
                        o_ref, *, lam_init):
    nq = q_ref.shape[2]
    lam = _diff_lambda(lam_ref[0], lam_init)
    for h in range(N_HEADS):
        cols = slice(h * HEAD_W, (h + 1) * HEAD_W)
        qq = q_ref[0, :, :, cols].reshape(2 * nq, HEAD_W)
        kc = ck_ref[0, :, cols].astype(BF16)
        vc = cv_ref[0, :, cols].astype(BF16)
        kn = kn_ref[0, :, cols]
        vn = vn_ref[0, :, cols]
        past = kc.shape[0]
        s_c = (_dot_nt(qq, kc).reshape(2, nq, past) + bc_ref[h][None]).reshape(2 * nq, past)
        s_n = (_dot_nt(qq, kn).reshape(2, nq, nq) + bn_ref[h][None]).reshape(2 * nq, nq)
        m = jnp.maximum(jnp.max(s_c, axis=-1, keepdims=True), jnp.max(s_n, axis=-1, keepdims=True))
        p_c = jnp.exp(s_c - m)
        p_n = jnp.exp(s_n - m)
        l = jnp.sum(p_c, axis=-1, keepdims=True) + jnp.sum(p_n, axis=-1, keepdims=True)
        acc = _dot(p_c.astype(BF16), vc) + _dot(p_n.astype(BF16), vn)
        o_ref[0, :, cols] = _attn_finish(acc, l, nq, lam, subg_ref[...], lam_init).astype(BF16)


def _attn_sample(q, ck, cv, kb, vb, bias_c, bias_n, lam_p, subg, j, lam_init):
    nbt, _, nq, _ = q.shape
    past = ck.shape[1]
    return pl.pallas_call(
        functools.partial(_attn_sample_kernel, lam_init=lam_init),
        grid=(nbt,),
        in_specs=[
            pl.BlockSpec((1, 2, nq, D_MODEL), lambda b: (b, 0, 0, 0)),
            pl.BlockSpec((1, past, D_MODEL), lambda b: (b, 0, 0)),
            pl.BlockSpec((1, past, D_MODEL), lambda b: (b, 0, 0)),
            pl.BlockSpec((1, nq, D_MODEL), lambda b: (b, 0, 0)),
            pl.BlockSpec((1, nq, D_MODEL), lambda b: (b, 0, 0)),
            pl.BlockSpec((N_HEADS, nq, past), lambda b: (0, 0, 0)),
            pl.BlockSpec((N_HEADS, nq, nq), lambda b: (0, 0, 0)),
            pl.BlockSpec((1, 4, HEAD_DIM), lambda b: (j, 0, 0)),
            pl.BlockSpec((1, HEAD_W), lambda b: (0, 0)),
        ],
        out_specs=pl.BlockSpec((1, nq, D_MODEL), lambda b: (b, 0, 0)),
        out_shape=jax.ShapeDtypeStruct((nbt, nq, D_MODEL), BF16),
        compiler_params=pltpu.CompilerParams(
            dimension_semantics=("arbitrary",), vmem_limit_bytes=VMEM_LIMIT_BYTES),
        name="attn_sample",
    )(q, ck, cv, kb, vb, bias_c, bias_n, lam_p, subg)


def _proj_res_kernel(x_ref, a_ref, gt_ref, w_ref, o_ref):
    x = x_ref[...]
    nb, r, _ = x.shape
    a = a_ref[...].reshape(nb * r, a_ref.shape[-1])
    y = _dot(a, w_ref[0])
    o_ref[...] = x + gt_ref[0, 0] * y.reshape(nb, r, D_MODEL)


def _proj_res(x, a, rows, mods, layer, j, w):
    return pl.pallas_call(
        _proj_res_kernel,
        grid=rows.grid,
        in_specs=[
            rows.act(D_MODEL), rows.act(a.shape[-1]), rows.mod(layer, 5),
            _const_spec((1,) + w.shape[1:], (j, 0, 0)),
        ],
        out_specs=rows.act(D_MODEL),
        out_shape=jax.ShapeDtypeStruct(x.shape, F32),
        compiler_params=_cparams(),
        name="proj_res",
    )(x, a, mods, w)


def kernel(x_prompt, x_sample, c_prompt, c_sample, state_lru_h, state_lru_conv, state_pool, cache_k, cache_v, ada_w, ada_b, norm_g, ffn_w_in, ffn_w_out, lru_w_in, lru_conv_w, lru_conv_b, lru_ga_w, lru_ga_b, lru_gx_w, lru_gx_b, lru_lambda, lru_w_out, pool_w, pool_b, pool_scale, attn_w_in, attn_q_g, attn_k_g, attn_lambda, attn_sub_g, attn_w_out, rel_bias):
    B, S, _ = x_prompt.shape
    SB, T, _ = x_sample.shape
    P = cache_k.shape[2]
    assert SB == ADA_PROMPT_ROW0 and ADA_PROMPT_ROW0 + B <= ADA_ROWS
    n_a, n_b, n_c = lru_w_in.shape[0], pool_w.shape[0], attn_w_in.shape[0]

    c_all = jnp.concatenate(
        [c_sample, c_prompt, jnp.zeros((ADA_ROWS - SB - B, D_MODEL), F32)], axis=0)
    mods = _adaln(c_all, ada_w, ada_b)
    norm_g4 = norm_g.reshape(DEPTH, 3, 1, D_MODEL)

    ffn_in_b = ffn_w_in.astype(BF16)
    ffn_out_b = ffn_w_out.astype(BF16)
    lru_in_b = lru_w_in.astype(BF16)
    lru_out_b = lru_w_out.astype(BF16)
    eye = jnp.eye(LRU_BLOCKS, dtype=F32)

    def block_diag(w):
        full = jnp.einsum('nhij,hg->nhigj', w, eye)
        return full.reshape(w.shape[0], D_RNN, D_RNN)

    lru_gate_b = jnp.concatenate([block_diag(lru_ga_w), block_diag(lru_gx_w)], axis=-1).astype(BF16)
    vec3 = lambda v: v.reshape(v.shape[0], 1, v.shape[-1])
    pool_w_b = pool_w.astype(BF16)
    attn_in_b = attn_w_in.astype(BF16)
    attn_out_b = attn_w_out.astype(BF16)
    seg = jnp.kron(jnp.eye(D_MODEL // HEAD_DIM, dtype=F32),
                   jnp.full((HEAD_DIM, HEAD_DIM), 1.0 / HEAD_DIM, F32)).astype(BF16)

    rows_p = {r: _Rows(B, S, r, True) for r in {FFN_ROWS, LRU_ROWS, POOL_ROWS, QKV_ROWS, OUT_ROWS}}
    rows_s = _Rows(SB, T, T, False)

    xp, xs = x_prompt, x_sample
    h_p, h_s, cv_p, cv_s, pl_p, pl_s, k_p, v_p, k_s, v_s = ([] for _ in range(10))
    for l in range(DEPTH):
        kind, j = l % 3, l // 3
        xp = _ffn(xp, rows_p[FFN_ROWS], mods, norm_g4, ffn_in_b, ffn_out_b, l, 0)
        xs = _ffn(xs, rows_s, mods, norm_g4, ffn_in_b, ffn_out_b, l, 0)
        if kind == 0:
            prm = (lru_in_b, lru_conv_w, vec3(lru_conv_b), lru_gate_b, vec3(lru_ga_b),
                   vec3(lru_gx_b), vec3(lru_lambda), lru_out_b)
            xp, hp, bp = _lru(xp, rows_p[LRU_ROWS], mods, norm_g4, l, j, *prm,
                              jnp.zeros((B, SUBLANES, D_RNN), F32), jnp.zeros((B, 1, D_RNN), F32))
            conv0 = jnp.pad(state_lru_conv[j], ((0, 0), (SUBLANES - (CONV_W - 1), 0), (0, 0)))
            xs, hs, bs = _lru(xs, rows_s, mods, norm_g4, l, j, *prm, conv0,
                              state_lru_h[j][:, None, :])
            h_p.append(hp), h_s.append(hs), cv_p.append(bp), cv_s.append(bs)
        elif kind == 1:
            prm = (pool_w_b, vec3(pool_b), vec3(pool_scale))
            xp, sp = _pool(xp, rows_p[POOL_ROWS], mods, norm_g4, l, j,
                           jnp.zeros((B, POOL_PAD, D_MODEL), F32), *prm, 0)
            hist0 = jnp.pad(state_pool[j], ((0, 0), (POOL_PAD - POOL_HIST, 0), (0, 0)))
            xs, ss = _pool(xs, rows_s, mods, norm_g4, l, j, hist0, *prm, POOL_HIST)
            pl_p.append(sp), pl_s.append(ss)
        else:
            lam_init = 0.8 - 0.6 * math.exp(-0.3 * l)
            qg = jnp.tile(attn_q_g[j], D_MODEL // HEAD_DIM)[None]
            kg = jnp.tile(attn_k_g[j], D_MODEL // HEAD_DIM)[None]
            subg = attn_sub_g[j][None]
            q, kf, kb, vf, vt = _qkv(xp, rows_p[QKV_ROWS], mods, norm_g4, l, j, attn_in_b, qg, kg, seg,
                                     True)
            bias_p = _bias_tile(rel_bias, 2 * ATT_TK, 2 * ATT_TQ, -ATT_TK, ATT_TK,
                                keys_on_rows=True, q_period=ATT_TQ)
            o = _attn_prompt(q, kb, vt, bias_p, attn_lambda, attn_sub_g[j][:, None], j, lam_init)
            xp = _proj_res(xp, o, rows_p[OUT_ROWS], mods, l, j, attn_out_b)
            k_p.append(kf.reshape(B, S, N_HEADS, HEAD_W)), v_p.append(vf.reshape(B, S, N_HEADS, HEAD_W))
            q, kf, kb, vf, vb = _qkv(xs, rows_s, mods, norm_g4, l, j, attn_in_b, qg, kg, seg, False)
            bias_c = _bias_tile(rel_bias, T, P, -P, None)
            bias_n = _bias_tile(rel_bias, T, T, 0, None)
            o = _attn_sample(q, cache_k[j].reshape(SB, P, D_MODEL), cache_v[j].reshape(SB, P, D_MODEL),
                             kb, vb, bias_c, bias_n, attn_lambda, subg, j, lam_init)
            xs = _proj_res(xs, o, rows_s, mods, l, j, attn_out_b)
            k_s.append(kf.reshape(SB, T, N_HEADS, HEAD_W)), v_s.append(vf.reshape(SB, T, N_HEADS, HEAD_W))
        xp = _ffn(xp, rows_p[FFN_ROWS], mods, norm_g4, ffn_in_b, ffn_out_b, l, 1)
        xs = _ffn(xs, rows_s, mods, norm_g4, ffn_in_b, ffn_out_b, l, 1)
    return (xp, xs, jnp.stack(h_p), jnp.stack(h_s), jnp.stack(cv_p), jnp.stack(cv_s),
            jnp.stack(pl_p), jnp.stack(pl_s), jnp.stack(k_p), jnp.stack(v_p),
            jnp.stack(k_s), jnp.stack(v_s))
```

```python
import functools
import math

import jax
import jax.numpy as jnp
from jax import lax
from jax.experimental import pallas as pl
from jax.experimental.pallas import tpu as pltpu

F32 = jnp.float32
BF16 = jnp.bfloat16

D_MODEL = 1024
DEPTH = 4
D_FF = 2816
D_RNN = 1280
LRU_BLOCKS = 16
LRU_BS = D_RNN // LRU_BLOCKS
CONV_W = 4
LRU_C = 8.0
POOL_WINDOWS = (2, 4, 8, 16)
POOL_GW = D_MODEL // len(POOL_WINDOWS)
POOL_HIST = 15
N_HEADS = 8
HEAD_DIM = D_MODEL // (2 * N_HEADS)
HEAD_W = 2 * HEAD_DIM
NUM_BUCKETS = 32
MAX_DISTANCE = 128
CHUNK = 64
EPS = 1e-6
NEG_INF = -1e30

ADA_ROWS = 16
ADA_PROMPT_ROW0 = 8

VMEM_LIMIT_BYTES = 60 * 1024 * 1024
SUBLANES = 8

FFN_ROWS = 512
FFN_CHUNK = 1408
LRU_ROWS = 256
POOL_ROWS = 512
QKV_ROWS = 512
ATT_TQ = 512
ATT_TK = 512
ATT_ROW_CHUNK = 64
VT_ROWS = HEAD_W + 16
LOG2E = math.log2(math.e)
OUT_ROWS = 512


def _cparams():
    return pltpu.CompilerParams(
        dimension_semantics=("arbitrary", "arbitrary"),
        vmem_limit_bytes=VMEM_LIMIT_BYTES,
    )


def _sigmoid(x):
    return 1.0 / (1.0 + jnp.exp(-x))


def _modulate(x, g, shift, scale):
    ms = jnp.mean(x * x, axis=-1, keepdims=True)
    y = x * lax.rsqrt(ms + EPS)
    return (y * g) * (1.0 + scale) + shift


def _dot(a, b):
    return jnp.dot(a, b, preferred_element_type=F32)


def _dot_nt(a, b):
    return lax.dot_general(a, b, (((1,), (1,)), ((), ())), preferred_element_type=F32)


def _adaln_kernel(c_ref, w_ref, b_ref, o_ref):
    c = c_ref[...]
    s = c * _sigmoid(c)
    y = _dot(s.astype(BF16), w_ref[0].astype(BF16))
    o_ref[0, 0] = y + b_ref[0, 0]


def _adaln(c_all, ada_w, ada_b):
    n_vec = 9
    out = pl.pallas_call(
        _adaln_kernel,
        grid=(DEPTH, n_vec),
        in_specs=[
            pl.BlockSpec((ADA_ROWS, D_MODEL), lambda l, k: (0, 0)),
            pl.BlockSpec((1, D_MODEL, D_MODEL), lambda l, k: (l, 0, k)),
            pl.BlockSpec((1, 1, 1, D_MODEL), lambda l, k: (l, k, 0, 0)),
        ],
        out_specs=pl.BlockSpec((1, 1, ADA_ROWS, D_MODEL), lambda l, k: (l, k, 0, 0)),
        out_shape=jax.ShapeDtypeStruct((DEPTH, n_vec, ADA_ROWS, D_MODEL), F32),
        compiler_params=_cparams(),
        name="adaln",
    )(c_all, ada_w, ada_b.reshape(DEPTH, n_vec, 1, D_MODEL))
    return out.reshape(DEPTH, n_vec, ADA_ROWS, 1, D_MODEL)


class _Rows:
    def __init__(self, batch, seq, rows, prompt):
        self.prompt = prompt
        if prompt:
            assert seq % rows == 0
            self.nb, self.r = 1, rows
            self.grid = (batch, seq // rows)
        else:
            self.nb, self.r = batch, seq
            self.grid = (1, 1)

    def act(self, width):
        return pl.BlockSpec((self.nb, self.r, width), lambda b, t: (b, t, 0))

    def per_seq(self, rows, width):
        return pl.BlockSpec((self.nb, rows, width), lambda b, t: (b, 0, 0))

    def mod(self, layer, k):
        if self.prompt:
            return pl.BlockSpec((1, 1, 1, 1, D_MODEL),
                                lambda b, t: (layer, k, ADA_PROMPT_ROW0 + b, 0, 0))
        return pl.BlockSpec((1, 1, self.nb, 1, D_MODEL), lambda b, t: (layer, k, 0, 0, 0))


def _const_spec(shape, index):
    return pl.BlockSpec(shape, lambda b, t: index)


def _ffn_kernel(x_ref, sh_ref, sc_ref, gt_ref, g_ref, win_ref, wout_ref, o_ref):
    x = x_ref[...]
    nb, r, _ = x.shape
    u = _modulate(x, g_ref[0, 0], sh_ref[0, 0], sc_ref[0, 0])
    ub = u.reshape(nb * r, D_MODEL).astype(BF16)
    acc = None
    for c in range(D_FF // FFN_CHUNK):
        lo = c * FFN_CHUNK
        a = _dot(ub, win_ref[0, 0, :, lo:lo + FFN_CHUNK])
        b = _dot(ub, win_ref[0, 0, :, D_FF + lo:D_FF + lo + FFN_CHUNK])
        h = ((a * _sigmoid(a)) * b).astype(BF16)
        y = _dot(h, wout_ref[0, 0, lo:lo + FFN_CHUNK, :])
        acc = y if acc is None else acc + y
    o_ref[...] = x + (0.5 * gt_ref[0, 0]) * acc.reshape(nb, r, D_MODEL)


def _ffn(x, rows, mods, norm_g4, w_in, w_out, layer, which):
    k0 = 0 if which == 0 else 6
    g_idx = 0 if which == 0 else 2
    return pl.pallas_call(
        _ffn_kernel,
        grid=rows.grid,
        in_specs=[
            rows.act(D_MODEL),
            rows.mod(layer, k0), rows.mod(layer, k0 + 1), rows.mod(layer, k0 + 2),
            _const_spec((1, 1, 1, D_MODEL), (layer, g_idx, 0, 0)),
            _const_spec((1, 1, D_MODEL, 2 * D_FF), (layer, which, 0, 0)),
            _const_spec((1, 1, D_FF, D_MODEL), (layer, which, 0, 0)),
        ],
        out_specs=rows.act(D_MODEL),
        out_shape=jax.ShapeDtypeStruct(x.shape, F32),
        compiler_params=_cparams(),
        name="ffn",
    )(x, mods, mods, mods, norm_g4, w_in, w_out)


def _shift_rows(v, d, fill):
    rolled = pltpu.roll(v, d, axis=1)
    row = lax.broadcasted_iota(jnp.int32, v.shape, 1)
    return jnp.where(row >= d, rolled, fill)


def _linear_scan(a, b):
    r = a.shape[1]
    d = 1
    while d < r:
        a_sh = _shift_rows(a, d, 1.0)
        b_sh = _shift_rows(b, d, 0.0)
        b = a * b_sh + b
        a = a * a_sh
        d *= 2
    return a, b


def _lru_kernel(x_ref, sh_ref, sc_ref, gt_ref, g_ref, win_ref, cw_ref, cb_ref, gw_ref, gab_ref,
                gxb_ref, lam_ref, wout_ref, conv0_ref, h0_ref,
                o_ref, hlast_ref, tail_ref, h_scr, tail_scr):
    @pl.when(pl.program_id(1) == 0)
    def _():
        h_scr[...] = h0_ref[...]
        tail_scr[...] = conv0_ref[...]

    x = x_ref[...]
    nb, r, _ = x.shape
    C = D_RNN
    u = _modulate(x, g_ref[0, 0], sh_ref[0, 0], sc_ref[0, 0])
    ub = u.reshape(nb * r, D_MODEL).astype(BF16)
    gate_br = _dot(ub, win_ref[0, :, :C])
    x_br = _dot(ub, win_ref[0, :, C:]).reshape(nb, r, C)

    xp = jnp.concatenate([tail_scr[...], x_br], axis=1)
    cw = cw_ref[0]
    off = SUBLANES - (CONV_W - 1)
    xc = cb_ref[0] + cw[0:1] * xp[:, off:off + r]
    for k in range(1, CONV_W):
        xc = xc + cw[k:k + 1] * xp[:, off + k:off + k + r]
    new_tail = x_br[:, r - SUBLANES:, :]
    tail_scr[...] = new_tail
    tail_ref[...] = new_tail

    xc2 = xc.reshape(nb * r, C)
    gates = _dot(xc2.astype(BF16), gw_ref[0])
    rg = _sigmoid(gates[:, :C] + gab_ref[0])
    ig = _sigmoid(gates[:, C:] + gxb_ref[0])
    nl = -lam_ref[0]
    softplus = jnp.maximum(nl, 0.0) + jnp.log1p(jnp.exp(-jnp.abs(nl)))
    log_a = (-LRU_C * rg) * softplus
    a = jnp.exp(log_a)
    th = jnp.tanh(log_a)
    one_minus_a2 = (-2.0 * th) / (1.0 - th)
    b_in = jnp.sqrt(one_minus_a2) * (ig * xc2)

    a_cum, b_cum = _linear_scan(a.reshape(nb, r, C), b_in.reshape(nb, r, C))
    h = a_cum * h_scr[...] + b_cum
    h_last = h[:, r - 1:r, :]
    h_scr[...] = h_last
    hlast_ref[...] = h_last

    gb = gate_br
    gelu = 0.5 * gb * (1.0 + jnp.tanh(math.sqrt(2.0 / math.pi) * (gb + 0.044715 * (gb * gb * gb))))
    y = _dot((gelu * h.reshape(nb * r, C)).astype(BF16), wout_ref[0])
    o_ref[...] = x + gt_ref[0, 0] * y.reshape(nb, r, D_MODEL)


def _lru(x, rows, mods, norm_g4, layer, j, w_in, conv_w, conv_b, gate_w, ga_b, gx_b, lam, w_out,
         conv0, h0):
    nbt = x.shape[0]
    C = D_RNN
    vec = lambda: _const_spec((1, 1, C), (j, 0, 0))
    out, h_last, tail = pl.pallas_call(
        _lru_kernel,
        grid=rows.grid,
        in_specs=[
            rows.act(D_MODEL),
            rows.mod(layer, 3), rows.mod(layer, 4), rows.mod(layer, 5),
            _const_spec((1, 1, 1, D_MODEL), (layer, 1, 0, 0)),
            _const_spec((1, D_MODEL, 2 * C), (j, 0, 0)),
            _const_spec((1, CONV_W, C), (j, 0, 0)),
            vec(),
            _const_spec((1, C, 2 * C), (j, 0, 0)),
            vec(), vec(), vec(),
            _const_spec((1, C, D_MODEL), (j, 0, 0)),
            rows.per_seq(SUBLANES, C),
            rows.per_seq(1, C),
        ],
        out_specs=[rows.act(D_MODEL), rows.per_seq(1, C), rows.per_seq(SUBLANES, C)],
        out_shape=[
            jax.ShapeDtypeStruct(x.shape, F32),
            jax.ShapeDtypeStruct((nbt, 1, C), F32),
            jax.ShapeDtypeStruct((nbt, SUBLANES, C), F32),
        ],
        scratch_shapes=[pltpu.VMEM((rows.nb, 1, C), F32), pltpu.VMEM((rows.nb, SUBLANES, C), F32)],
        compiler_params=_cparams(),
        name="lru",
    )(x, mods, mods, mods, norm_g4, w_in, conv_w, conv_b, gate_w, ga_b, gx_b, lam, w_out, conv0, h0)
    return out, h_last[:, 0], tail[:, SUBLANES - (CONV_W - 1):]


POOL_PAD = 16


def _pool_kernel(x_ref, sh_ref, sc_ref, gt_ref, g_ref, hist0_ref, w_ref, b_ref, ps_ref,
                 o_ref, st_ref, hist_scr, *, t_base):
    t = pl.program_id(1)

    @pl.when(t == 0)
    def _():
        hist_scr[...] = hist0_ref[...]

    x = x_ref[...]
    nb, r, _ = x.shape
    u = _modulate(x, g_ref[0, 0], sh_ref[0, 0], sc_ref[0, 0])
    ext = jnp.concatenate([hist_scr[...], u], axis=1)
    new_hist = u[:, r - POOL_PAD:, :]
    hist_scr[...] = new_hist
    st_ref[...] = new_hist

    row = lax.broadcasted_iota(jnp.int32, (1, r, 1), 1)
    t_glob = t_base + t * r + row
    gate = gt_ref[0, 0]
    s = ext
    for g, wnd in enumerate(POOL_WINDOWS):
        lo = g * POOL_GW
        s = s[:, :, (POOL_GW if g > 0 else 0):]
        s = s + pltpu.roll(s, wnd // 2, axis=1)
        cnt = jnp.minimum(t_glob + 1, wnd).astype(F32)
        mean = s[:, POOL_PAD:, :POOL_GW] / cnt
        d = mean - u[:, :, lo:lo + POOL_GW]
        y = _dot(d.reshape(nb * r, POOL_GW).astype(BF16), w_ref[0, g]).reshape(nb, r, POOL_GW)
        y = (y + b_ref[0, :, lo:lo + POOL_GW]) * ps_ref[0, :, lo:lo + POOL_GW]
        o_ref[:, :, lo:lo + POOL_GW] = x[:, :, lo:lo + POOL_GW] + gate[:, :, lo:lo + POOL_GW] * y


def _pool(x, rows, mods, norm_g4, layer, j, hist0, w, b, scale, t_base):
    nbt = x.shape[0]
    out, st = pl.pallas_call(
        functools.partial(_pool_kernel, t_base=t_base),
        grid=rows.grid,
        in_specs=[
            rows.act(D_MODEL),
            rows.mod(layer, 3), rows.mod(layer, 4), rows.mod(layer, 5),
            _const_spec((1, 1, 1, D_MODEL), (layer, 1, 0, 0)),
            rows.per_seq(POOL_PAD, D_MODEL),
            _const_spec((1, len(POOL_WINDOWS), POOL_GW, POOL_GW), (j, 0, 0, 0)),
            _const_spec((1, 1, D_MODEL), (j, 0, 0)),
            _const_spec((1, 1, D_MODEL), (j, 0, 0)),
        ],
        out_specs=[rows.act(D_MODEL), rows.per_seq(POOL_PAD, D_MODEL)],
        out_shape=[jax.ShapeDtypeStruct(x.shape, F32),
                   jax.ShapeDtypeStruct((nbt, POOL_PAD, D_MODEL), F32)],
        scratch_shapes=[pltpu.VMEM((rows.nb, POOL_PAD, D_MODEL), F32)],
        compiler_params=_cparams(),
        name="pool",
    )(x, mods, mods, mods, norm_g4, hist0, w, b, scale)
    return out, st[:, POOL_PAD - POOL_HIST:]


def _qkv_kernel(x_ref, sh_ref, sc_ref, g_ref, w_ref, qg_ref, kg_ref, seg_ref,
                q_ref, kf_ref, kb_ref, vf_ref, vb_ref, *, v_transposed):
    x = x_ref[...]
    nb, r, _ = x.shape
    u = _modulate(x, g_ref[0, 0], sh_ref[0, 0], sc_ref[0, 0])
    ub = u.reshape(nb * r, D_MODEL).astype(BF16)
    q = _dot(ub, w_ref[0, :, :D_MODEL])
    k = _dot(ub, w_ref[0, :, D_MODEL:2 * D_MODEL])
    v = _dot(ub, w_ref[0, :, 2 * D_MODEL:])

    def head_norm(z, gain):
        ms = _dot((z * z).astype(BF16), seg_ref[...])
        return (z * lax.rsqrt(ms + EPS)) * gain

    qn = head_norm(q, qg_ref[...]) * (HEAD_DIM ** -0.5 * LOG2E)
    kn = head_norm(k, kg_ref[...])
    lane = lax.broadcasted_iota(jnp.int32, (1, D_MODEL), 1)
    first = (lane % HEAD_W) < HEAD_DIM
    q_ref[:, 0] = jnp.where(first, qn, 0.0).astype(BF16).reshape(nb, r, D_MODEL)
    q_ref[:, 1] = jnp.where(first, 0.0, qn).astype(BF16).reshape(nb, r, D_MODEL)
    kf_ref[...] = kn.reshape(nb, r, D_MODEL)
    kb_ref[...] = kn.astype(BF16).reshape(nb, r, D_MODEL)
    vf_ref[...] = v.reshape(nb, r, D_MODEL)
    if v_transposed:
        for h in range(N_HEADS):
            vb_ref[0, h, :HEAD_W] = v[:, h * HEAD_W:(h + 1) * HEAD_W].T.astype(BF16)
            vb_ref[0, h, HEAD_W:] = jnp.ones((VT_ROWS - HEAD_W, r), BF16)
    else:
        vb_ref[...] = v.astype(BF16).reshape(nb, r, D_MODEL)


def _qkv(x, rows, mods, norm_g4, layer, j, w_in, qg, kg, seg, v_transposed):
    nbt, seq, _ = x.shape
    q_spec = pl.BlockSpec((rows.nb, 2, rows.r, D_MODEL), lambda b, t: (b, 0, t, 0))
    if v_transposed:
        assert rows.nb == 1
        vb_spec = pl.BlockSpec((1, N_HEADS, VT_ROWS, rows.r), lambda b, t: (b, 0, 0, t))
        vb_shape = jax.ShapeDtypeStruct((nbt, N_HEADS, VT_ROWS, seq), BF16)
    else:
        vb_spec = rows.act(D_MODEL)
        vb_shape = jax.ShapeDtypeStruct(x.shape, BF16)
    return pl.pallas_call(
        functools.partial(_qkv_kernel, v_transposed=v_transposed),
        grid=rows.grid,
        in_specs=[
            rows.act(D_MODEL),
            rows.mod(layer, 3), rows.mod(layer, 4),
            _const_spec((1, 1, 1, D_MODEL), (layer, 1, 0, 0)),
            _const_spec((1, D_MODEL, 3 * D_MODEL), (j, 0, 0)),
            _const_spec((1, D_MODEL), (0, 0)),
            _const_spec((1, D_MODEL), (0, 0)),
            _const_spec((D_MODEL, D_MODEL), (0, 0)),
        ],
        out_specs=[q_spec, rows.act(D_MODEL), rows.act(D_MODEL), rows.act(D_MODEL), vb_spec],
        out_shape=[
            jax.ShapeDtypeStruct((nbt, 2, seq, D_MODEL), BF16),
            jax.ShapeDtypeStruct(x.shape, F32),
            jax.ShapeDtypeStruct(x.shape, BF16),
            jax.ShapeDtypeStruct(x.shape, F32),
            vb_shape,
        ],
        compiler_params=_cparams(),
        name="qkv",
    )(x, mods, mods, norm_g4, w_in, qg, kg, seg)


_BUCKET_EDGES = (12, 16, 23, 32, 46, 64, 91)
_FAR_BUCKET = NUM_BUCKETS // 2 - 1


def _bucket_of(rel):
    n = abs(rel)
    small = NUM_BUCKETS // 4
    b = n if n < small else small + sum(n >= e for e in _BUCKET_EDGES)
    return b + (NUM_BUCKETS // 2 if rel > 0 else 0)


_BUCKET_RUNS = tuple((r, _bucket_of(r)) for r in range(-_BUCKET_EDGES[-1] + 1, _BUCKET_EDGES[-1] + 1)
                     if _bucket_of(r) != _bucket_of(r - 1))


def _bias_kernel(rb_ref, o_ref, *, rel0, mask_from, keys_on_rows, q_period):
    h = pl.program_id(0)
    _, R, W = o_ref.shape
    i = lax.broadcasted_iota(jnp.int32, (R, W), 0)
    jj = lax.broadcasted_iota(jnp.int32, (R, W), 1)
    if keys_on_rows:
        key, qry = i, jj & (q_period - 1)
    else:
        key, qry = jj, i
    rel = key - qry + rel0
    out = jnp.full((R, W), rb_ref[_FAR_BUCKET, h], F32)
    for first_rel, bucket in _BUCKET_RUNS:
        out = jnp.where(rel >= first_rel, rb_ref[bucket, h], out)
    out = (out - rb_ref[_FAR_BUCKET, h]) * LOG2E
    if mask_from is not None:
        visible = ((key - mask_from) // CHUNK <= qry // CHUNK) | (key < mask_from)
        out = jnp.where(visible, out, NEG_INF)
    o_ref[0] = out


def _bias_tile(rel_bias, rows, width, rel0, mask_from, keys_on_rows=False, q_period=None):
    assert q_period is None or q_period & (q_period - 1) == 0
    return pl.pallas_call(
        functools.partial(_bias_kernel, rel0=rel0, mask_from=mask_from,
                          keys_on_rows=keys_on_rows, q_period=q_period),
        grid=(N_HEADS,),
        in_specs=[pl.BlockSpec(memory_space=pltpu.SMEM)],
        out_specs=pl.BlockSpec((1, rows, width), lambda h: (h, 0, 0)),
        out_shape=jax.ShapeDtypeStruct((N_HEADS, rows, width), F32),
        name="bias_tile",
    )(rel_bias)


def _diff_lambda(lp, lam_init):
    s1 = jnp.sum(lp[0:1] * lp[1:2], axis=-1, keepdims=True)
    s2 = jnp.sum(lp[2:3] * lp[3:4], axis=-1, keepdims=True)
    return jnp.exp(s1) - jnp.exp(s2) + lam_init


def _attn_finish(acc, l, nq, lam, subg, lam_init):
    o = acc / l
    o = o[:nq] - lam * o[nq:]
    ms = jnp.mean(o * o, axis=-1, keepdims=True)
    return ((o * lax.rsqrt(ms + EPS)) * subg) * (1.0 - lam_init)


def _attn_prompt_kernel(q_ref, k_ref, vt_ref, bias_ref, lam_ref, subg_ref, o_ref,
                        qt_scr, s0_scr, s1_scr, p0_scr, p1_scr, m_scr, a_scr, acc_scr, *, lam_init):
    qi = pl.program_id(2)
    tq, tk = ATT_TQ, ATT_TK
    width = 2 * tq
    n_blocks = qi + 1
    qt_scr[:, :tq] = q_ref[0, 0].astype(F32).T.astype(BF16)
    qt_scr[:, tq:] = q_ref[0, 1].astype(F32).T.astype(BF16)
    m_scr[...] = jnp.full(m_scr.shape, NEG_INF, F32)
    a_scr[...] = jnp.ones(a_scr.shape, F32)
    acc_scr[...] = jnp.zeros(acc_scr.shape, F32)
    p1_scr[...] = jnp.zeros(p1_scr.shape, BF16)

    def block_start(c):
        return pl.multiple_of(jnp.clip(c, 0, n_blocks - 1) * tk, tk)

    def scores(c, s_ref):
        sel = jnp.clip(c - (n_blocks - 3), 0, 3)
        s_ref[...] = _dot(k_ref[0, pl.ds(block_start(c), tk), :], qt_scr[...]) + bias_ref[0, sel]

    def weigh(c, p_ref):
        pv = _dot(vt_ref[0, 0, :, pl.ds(block_start(c), tk)], p_ref[...])
        acc_scr[...] = a_scr[...] * acc_scr[...] + pv

    def softmax(s_ref, p_ref):
        m8 = None
        for r0 in range(0, tk, ATT_ROW_CHUNK):
            part = s_ref[r0:r0 + ATT_ROW_CHUNK, :].reshape(ATT_ROW_CHUNK // SUBLANES, SUBLANES, width)
            part = jnp.max(part, axis=0)
            m8 = part if m8 is None else jnp.maximum(m8, part)
        m_old = m_scr[...]
        m_new = jnp.maximum(m_old, jnp.max(m8, axis=0, keepdims=True))
        a_scr[...] = jnp.exp2(m_old - m_new)
        m_scr[...] = m_new
        for r0 in range(0, tk, ATT_ROW_CHUNK):
            rows = slice(r0, r0 + ATT_ROW_CHUNK)
            p_ref[rows, :] = jnp.exp2((s_ref[rows, :] - m_new).astype(BF16))

    scores(0, s0_scr)

    def trip(t, carry):
        c = 2 * t
        weigh(c - 1, p1_scr)
        softmax(s0_scr, p0_scr)
        scores(c + 1, s1_scr)
        weigh(c, p0_scr)
        softmax(s1_scr, p1_scr)
        scores(c + 2, s0_scr)
        return carry

    n_trips = (n_blocks + 1) // 2
    lax.fori_loop(0, n_trips, trip, 0)
    weigh(2 * n_trips - 1, p1_scr)

    lam = _diff_lambda(lam_ref[0], lam_init)
    acc = acc_scr[...]
    o = acc[:HEAD_W] / acc[HEAD_W:HEAD_W + 1]
    o = o[:, :tq] - lam * o[:, tq:]
    ms = jnp.mean(o * o, axis=0, keepdims=True)
    on = ((o * lax.rsqrt(ms + EPS)) * subg_ref[...]) * (1.0 - lam_init)
    o_ref[0] = on.T.astype(BF16)


def _attn_prompt(q, kb, vt, bias, lam_p, subg_col, j, lam_init):
    nbt, _, seq, _ = q.shape
    tq, tk = ATT_TQ, ATT_TK
    return pl.pallas_call(
        functools.partial(_attn_prompt_kernel, lam_init=lam_init),
        grid=(nbt, N_HEADS, seq // tq),
        in_specs=[
            pl.BlockSpec((1, 2, tq, HEAD_W), lambda b, h, i: (b, 0, i, h)),
            pl.BlockSpec((1, seq, HEAD_W), lambda b, h, i: (b, 0, h)),
            pl.BlockSpec((1, 1, VT_ROWS, seq), lambda b, h, i: (b, h, 0, 0)),
            pl.BlockSpec((1, 4, tk, 2 * tq), lambda b, h, i: (h, 0, 0, 0)),
            pl.BlockSpec((1, 4, HEAD_DIM), lambda b, h, i: (j, 0, 0)),
            pl.BlockSpec((HEAD_W, 1), lambda b, h, i: (0, 0)),
        ],
        out_specs=pl.BlockSpec((1, tq, HEAD_W), lambda b, h, i: (b, i, h)),
        out_shape=jax.ShapeDtypeStruct((nbt, seq, D_MODEL), BF16),
        scratch_shapes=[pltpu.VMEM((HEAD_W, 2 * tq), BF16),
                        pltpu.VMEM((tk, 2 * tq), F32), pltpu.VMEM((tk, 2 * tq), F32),
                        pltpu.VMEM((tk, 2 * tq), BF16), pltpu.VMEM((tk, 2 * tq), BF16),
                        pltpu.VMEM((1, 2 * tq), F32), pltpu.VMEM((1, 2 * tq), F32),
                        pltpu.VMEM((VT_ROWS, 2 * tq), F32)],
        compiler_params=pltpu.CompilerParams(
            dimension_semantics=("arbitrary", "arbitrary", "arbitrary"),
            vmem_limit_bytes=VMEM_LIMIT_BYTES),
        name="attn_prompt",
    )(q, kb, vt, bias, lam_p, subg_col)


def _attn_sample_kernel(q_ref, ck_ref, cv_ref, kn_ref, vn_ref, bc_ref, bn_ref, lam_ref, subg_ref,
                        o_ref, *, lam_init):
    nq = q_ref.shape[2]
    lam = _diff_lambda(lam_ref[0], lam_init)
    for h in range(N_HEADS):
        cols = slice(h * HEAD_W, (h + 1) * HEAD_W)
        qq = q_ref[0, :, :, cols].reshape(2 * nq, HEAD_W)
        kc = ck_ref[0, :, cols].astype(BF16)
        vc = cv_ref[0, :, cols].astype(BF16)
        kn = kn_ref[0, :, cols]
        vn = vn_ref[0, :, cols]
        past = kc.shape[0]
        s_c = (_dot_nt(qq, kc).reshape(2, nq, past) + bc_ref[h][None]).reshape(2 * nq, past)
        s_n = (_dot_nt(qq, kn).reshape(2, nq, nq) + bn_ref[h][None]).reshape(2 * nq, nq)
        m = jnp.maximum(jnp.max(s_c, axis=-1, keepdims=True), jnp.max(s_n, axis=-1, keepdims=True))
        p_c = jnp.exp2(s_c - m)
        p_n = jnp.exp2(s_n - m)
        l = jnp.sum(p_c, axis=-1, keepdims=True) + jnp.sum(p_n, axis=-1, keepdims=True)
        acc = _dot(p_c.astype(BF16), vc) + _dot(p_n.astype(BF16), vn)
        o_ref[0, :, cols] = _attn_finish(acc, l, nq, lam, subg_ref[...], lam_init).astype(BF16)


def _attn_sample(q, ck, cv, kb, vb, bias_c, bias_n, lam_p, subg, j, lam_init):
    nbt, _, nq, _ = q.shape
    past = ck.shape[1]
    return pl.pallas_call(
        functools.partial(_attn_sample_kernel, lam_init=lam_init),
        grid=(nbt,),
        in_specs=[
            pl.BlockSpec((1, 2, nq, D_MODEL), lambda b: (b, 0, 0, 0)),
            pl.BlockSpec((1, past, D_MODEL), lambda b: (b, 0, 0)),
            pl.BlockSpec((1, past, D_MODEL), lambda b: (b, 0, 0)),
            pl.BlockSpec((1, nq, D_MODEL), lambda b: (b, 0, 0)),
            pl.BlockSpec((1, nq, D_MODEL), lambda b: (b, 0, 0)),
            pl.BlockSpec((N_HEADS, nq, past), lambda b: (0, 0, 0)),
            pl.BlockSpec((N_HEADS, nq, nq), lambda b: (0, 0, 0)),
            pl.BlockSpec((1, 4, HEAD_DIM), lambda b: (j, 0, 0)),
            pl.BlockSpec((1, HEAD_W), lambda b: (0, 0)),
        ],
        out_specs=pl.BlockSpec((1, nq, D_MODEL), lambda b: (b, 0, 0)),
        out_shape=jax.ShapeDtypeStruct((nbt, nq, D_MODEL), BF16),
        compiler_params=pltpu.CompilerParams(
            dimension_semantics=("arbitrary",), vmem_limit_bytes=VMEM_LIMIT_BYTES),
        name="attn_sample",
    )(q, ck, cv, kb, vb, bias_c, bias_n, lam_p, subg)


def _proj_res_kernel(x_ref, a_ref, gt_ref, w_ref, o_ref):
    x = x_ref[...]
    nb, r, _ = x.shape
    a = a_ref[...].reshape(nb * r, a_ref.shape[-1])
    y = _dot(a, w_ref[0])
    o_ref[...] = x + gt_ref[0, 0] * y.reshape(nb, r, D_MODEL)


def _proj_res(x, a, rows, mods, layer, j, w):
    return pl.pallas_call(
        _proj_res_kernel,
        grid=rows.grid,
        in_specs=[
            rows.act(D_MODEL), rows.act(a.shape[-1]), rows.mod(layer, 5),
            _const_spec((1,) + w.shape[1:], (j, 0, 0)),
        ],
        out_specs=rows.act(D_MODEL),
        out_shape=jax.ShapeDtypeStruct(x.shape, F32),
        compiler_params=_cparams(),
        name="proj_res",
    )(x, a, mods, w)


def kernel(x_prompt, x_sample, c_prompt, c_sample, state_lru_h, state_lru_conv, state_pool, cache_k, cache_v, ada_w, ada_b, norm_g, ffn_w_in, ffn_w_out, lru_w_in, lru_conv_w, lru_conv_b, lru_ga_w, lru_ga_b, lru_gx_w, lru_gx_b, lru_lambda, lru_w_out, pool_w, pool_b, pool_scale, attn_w_in, attn_q_g, attn_k_g, attn_lambda, attn_sub_g, attn_w_out, rel_bias):
    B, S, _ = x_prompt.shape
    SB, T, _ = x_sample.shape
    P = cache_k.shape[2]
    assert SB == ADA_PROMPT_ROW0 and ADA_PROMPT_ROW0 + B <= ADA_ROWS
    n_a, n_b, n_c = lru_w_in.shape[0], pool_w.shape[0], attn_w_in.shape[0]

    c_all = jnp.concatenate(
        [c_sample, c_prompt, jnp.zeros((ADA_ROWS - SB - B, D_MODEL), F32)], axis=0)
    mods = _adaln(c_all, ada_w, ada_b)
    norm_g4 = norm_g.reshape(DEPTH, 3, 1, D_MODEL)

    ffn_in_b = ffn_w_in.astype(BF16)
    ffn_out_b = ffn_w_out.astype(BF16)
    lru_in_b = lru_w_in.astype(BF16)
    lru_out_b = lru_w_out.astype(BF16)
    eye = jnp.eye(LRU_BLOCKS, dtype=F32)

    def block_diag(w):
        full = jnp.einsum('nhij,hg->nhigj', w, eye)
        return full.reshape(w.shape[0], D_RNN, D_RNN)

    lru_gate_b = jnp.concatenate([block_diag(lru_ga_w), block_diag(lru_gx_w)], axis=-1).astype(BF16)
    vec3 = lambda v: v.reshape(v.shape[0], 1, v.shape[-1])
    pool_w_b = pool_w.astype(BF16)
    attn_in_b = attn_w_in.astype(BF16)
    attn_out_b = attn_w_out.astype(BF16)
    seg = jnp.kron(jnp.eye(D_MODEL // HEAD_DIM, dtype=F32),
                   jnp.full((HEAD_DIM, HEAD_DIM), 1.0 / HEAD_DIM, F32)).astype(BF16)

    rows_p = {r: _Rows(B, S, r, True) for r in {FFN_ROWS, LRU_ROWS, POOL_ROWS, QKV_ROWS, OUT_ROWS}}
    rows_s = _Rows(SB, T, T, False)

    xp, xs = x_prompt, x_sample
    h_p, h_s, cv_p, cv_s, pl_p, pl_s, k_p, v_p, k_s, v_s = ([] for _ in range(10))
    for l in range(DEPTH):
        kind, j = l % 3, l // 3
        xp = _ffn(xp, rows_p[FFN_ROWS], mods, norm_g4, ffn_in_b, ffn_out_b, l, 0)
        xs = _ffn(xs, rows_s, mods, norm_g4, ffn_in_b, ffn_out_b, l, 0)
        if kind == 0:
            prm = (lru_in_b, lru_conv_w, vec3(lru_conv_b), lru_gate_b, vec3(lru_ga_b),
                   vec3(lru_gx_b), vec3(lru_lambda), lru_out_b)
            xp, hp, bp = _lru(xp, rows_p[LRU_ROWS], mods, norm_g4, l, j, *prm,
                              jnp.zeros((B, SUBLANES, D_RNN), F32), jnp.zeros((B, 1, D_RNN), F32))
            conv0 = jnp.pad(state_lru_conv[j], ((0, 0), (SUBLANES - (CONV_W - 1), 0), (0, 0)))
            xs, hs, bs = _lru(xs, rows_s, mods, norm_g4, l, j, *prm, conv0,
                              state_lru_h[j][:, None, :])
            h_p.append(hp), h_s.append(hs), cv_p.append(bp), cv_s.append(bs)
        elif kind == 1:
            prm = (pool_w_b, vec3(pool_b), vec3(pool_scale))
            xp, sp = _pool(xp, rows_p[POOL_ROWS], mods, norm_g4, l, j,
                           jnp.zeros((B, POOL_PAD, D_MODEL), F32), *prm, 0)
            hist0 = jnp.pad(state_pool[j], ((0, 0), (POOL_PAD - POOL_HIST, 0), (0, 0)))
            xs, ss = _pool(xs, rows_s, mods, norm_g4, l, j, hist0, *prm, POOL_HIST)
            pl_p.append(sp), pl_s.append(ss)
        else:
            lam_init = 0.8 - 0.6 * math.exp(-0.3 * l)
            qg = jnp.tile(attn_q_g[j], D_MODEL // HEAD_DIM)[None]
            kg = jnp.tile(attn_k_g[j], D_MODEL // HEAD_DIM)[None]
            subg = attn_sub_g[j][None]
            q, kf, kb, vf, vt = _qkv(xp, rows_p[QKV_ROWS], mods, norm_g4, l, j, attn_in_b, qg, kg, seg,
                                     True)
            bias_p = _bias_tile(rel_bias, 4 * ATT_TK, 2 * ATT_TQ, -2 * ATT_TK, 2 * ATT_TK,
                                keys_on_rows=True, q_period=ATT_TQ)
            bias_p = bias_p.reshape(N_HEADS, 4, ATT_TK, 2 * ATT_TQ)
            o = _attn_prompt(q, kb, vt, bias_p, attn_lambda, attn_sub_g[j][:, None], j, lam_init)
            xp = _proj_res(xp, o, rows_p[OUT_ROWS], mods, l, j, attn_out_b)
            k_p.append(kf.reshape(B, S, N_HEADS, HEAD_W)), v_p.append(vf.reshape(B, S, N_HEADS, HEAD_W))
            q, kf, kb, vf, vb = _qkv(xs, rows_s, mods, norm_g4, l, j, attn_in_b, qg, kg, seg, False)
            bias_c = _bias_tile(rel_bias, T, P, -P, None)
            bias_n = _bias_tile(rel_bias, T, T, 0, None)
            o = _attn_sample(q, cache_k[j].reshape(SB, P, D_MODEL), cache_v[j].reshape(SB, P, D_MODEL),
                             kb, vb, bias_c, bias_n, attn_lambda, subg, j, lam_init)
            xs = _proj_res(xs, o, rows_s, mods, l, j, attn_out_b)
            k_s.append(kf.reshape(SB, T, N_HEADS, HEAD_W)), v_s.append(vf.reshape(SB, T, N_HEADS, HEAD_W))
        xp = _ffn(xp, rows_p[FFN_ROWS], mods, norm_g4, ffn_in_b, ffn_out_b, l, 1)
        xs = _ffn(xs, rows_s, mods, norm_g4, ffn_in_b, ffn_out_b, l, 1)
    return (xp, xs, jnp.stack(h_p), jnp.stack(h_s), jnp.stack(cv_p), jnp.stack(cv_s),
            jnp.stack(pl_p), jnp.stack(pl_s), jnp.stack(k_p), jnp.stack(v_p),
            jnp.stack(k_s), jnp.stack(v_s))
```

```python
import functools
import math

import jax
import jax.numpy as jnp
from jax import lax
from jax.experimental import pallas as pl
from jax.experimental.pallas import tpu as pltpu

F32 = jnp.float32
BF16 = jnp.bfloat16

D_MODEL = 1024
DEPTH = 4
D_FF = 2816
D_RNN = 1280
LRU_BLOCKS = 16
LRU_BS = D_RNN // LRU_BLOCKS
LRU_SUPER = 2
CONV_W = 4
LRU_C = 8.0
POOL_WINDOWS = (2, 4, 8, 16)
POOL_GW = D_MODEL // len(POOL_WINDOWS)
POOL_HIST = 15
N_HEADS = 8
HEAD_DIM = D_MODEL // (2 * N_HEADS)
HEAD_W = 2 * HEAD_DIM
NUM_BUCKETS = 32
MAX_DISTANCE = 128
CHUNK = 64
EPS = 1e-6
NEG_INF = -1e30

ADA_ROWS = 16
ADA_PROMPT_ROW0 = 8

VMEM_LIMIT_BYTES = 60 * 1024 * 1024
SUBLANES = 8

FFN_ROWS = 1024
FFN_CHUNK = 256
LRU_ROWS = 256
POOL_ROWS = 512
QKV_ROWS = 512
ATT_TQ = 512
ATT_TK = 512
ATT_ROW_CHUNK = 64
VT_ROWS = HEAD_W + 16
LOG2E = math.log2(math.e)
OUT_ROWS = 512


def _cparams():
    return pltpu.CompilerParams(
        dimension_semantics=("arbitrary", "arbitrary"),
        vmem_limit_bytes=VMEM_LIMIT_BYTES,
    )


def _sigmoid(x):
    return 0.5 * jnp.tanh(0.5 * x) + 0.5


def _modulate(x, g, shift, scale):
    ms = jnp.mean(x * x, axis=-1, keepdims=True)
    y = x * lax.rsqrt(ms + EPS)
    return (y * g) * (1.0 + scale) + shift


def _dot(a, b):
    return jnp.dot(a, b, preferred_element_type=F32)


def _dot_nt(a, b):
    return lax.dot_general(a, b, (((1,), (1,)), ((), ())), preferred_element_type=F32)


def _adaln_kernel(c_ref, w_ref, b_ref, o_ref):
    c = c_ref[...]
    s = c * _sigmoid(c)
    y = _dot(s.astype(BF16), w_ref[0].astype(BF16))
    o_ref[0, 0] = y + b_ref[0, 0]


def _adaln(c_all, ada_w, ada_b):
    n_vec = 9
    out = pl.pallas_call(
        _adaln_kernel,
        grid=(DEPTH, n_vec),
        in_specs=[
            pl.BlockSpec((ADA_ROWS, D_MODEL), lambda l, k: (0, 0)),
            pl.BlockSpec((1, D_MODEL, D_MODEL), lambda l, k: (l, 0, k)),
            pl.BlockSpec((1, 1, 1, D_MODEL), lambda l, k: (l, k, 0, 0)),
        ],
        out_specs=pl.BlockSpec((1, 1, ADA_ROWS, D_MODEL), lambda l, k: (l, k, 0, 0)),
        out_shape=jax.ShapeDtypeStruct((DEPTH, n_vec, ADA_ROWS, D_MODEL), F32),
        compiler_params=_cparams(),
        name="adaln",
    )(c_all, ada_w, ada_b.reshape(DEPTH, n_vec, 1, D_MODEL))
    return out.reshape(DEPTH, n_vec, ADA_ROWS, 1, D_MODEL)


class _Rows:
    def __init__(self, batch, seq, rows, prompt):
        self.prompt = prompt
        if prompt:
            assert seq % rows == 0
            self.nb, self.r = 1, rows
            self.grid = (batch, seq // rows)
        else:
            self.nb, self.r = batch, seq
            self.grid = (1, 1)

    def act(self, width):
        return pl.BlockSpec((self.nb, self.r, width), lambda b, t: (b, t, 0))

    def per_seq(self, rows, width):
        return pl.BlockSpec((self.nb, rows, width), lambda b, t: (b, 0, 0))

    def mod(self, layer, k):
        if self.prompt:
            return pl.BlockSpec((1, 1, 1, 1, D_MODEL),
                                lambda b, t: (layer, k, ADA_PROMPT_ROW0 + b, 0, 0))
        return pl.BlockSpec((1, 1, self.nb, 1, D_MODEL), lambda b, t: (layer, k, 0, 0, 0))


def _const_spec(shape, index, single_buffer=False):
    if single_buffer:
        return pl.BlockSpec(shape, lambda b, t: index, pipeline_mode=pl.Buffered(1))
    return pl.BlockSpec(shape, lambda b, t: index)


def _ffn_kernel(x_ref, sh_ref, sc_ref, gt_ref, g_ref, win_ref, wout_ref, o_ref):
    x = x_ref[...]
    nb, r, _ = x.shape
    u = _modulate(x, g_ref[0, 0], sh_ref[0, 0], sc_ref[0, 0])
    ub = u.reshape(nb * r, D_MODEL).astype(BF16)
    acc = None
    for c in range(D_FF // FFN_CHUNK):
        lo = c * FFN_CHUNK
        a = _dot(ub, win_ref[0, 0, :, lo:lo + FFN_CHUNK])
        b = _dot(ub, win_ref[0, 0, :, D_FF + lo:D_FF + lo + FFN_CHUNK])
        h = ((a * _sigmoid(a)) * b).astype(BF16)
        y = _dot(h, wout_ref[0, 0, lo:lo + FFN_CHUNK, :])
        acc = y if acc is None else acc + y
    o_ref[...] = x + (0.5 * gt_ref[0, 0]) * acc.reshape(nb, r, D_MODEL)


def _ffn(x, rows, mods, norm_g4, w_in, w_out, layer, which):
    k0 = 0 if which == 0 else 6
    g_idx = 0 if which == 0 else 2
    return pl.pallas_call(
        _ffn_kernel,
        grid=rows.grid,
        in_specs=[
            rows.act(D_MODEL),
            rows.mod(layer, k0), rows.mod(layer, k0 + 1), rows.mod(layer, k0 + 2),
            _const_spec((1, 1, 1, D_MODEL), (layer, g_idx, 0, 0)),
            _const_spec((1, 1, D_MODEL, 2 * D_FF), (layer, which, 0, 0), True),
            _const_spec((1, 1, D_FF, D_MODEL), (layer, which, 0, 0), True),
        ],
        out_specs=rows.act(D_MODEL),
        out_shape=jax.ShapeDtypeStruct(x.shape, F32),
        compiler_params=_cparams(),
        name="ffn",
    )(x, mods, mods, mods, norm_g4, w_in, w_out)


def _group_scan(a, b):
    row = lax.broadcasted_iota(jnp.int32, a.shape, 1)
    d = 1
    while d < SUBLANES:
        keep = row >= d
        a_sh = jnp.where(keep, pltpu.roll(a, d, axis=1), 1.0)
        b_sh = jnp.where(keep, pltpu.roll(b, d, axis=1), 0.0)
        b = a * b_sh + b
        a = a * a_sh
        d *= 2
    return a, b


def _linear_scan(a, b, h_prev, h_ref):
    nb, r, C = a.shape
    a_grp, b_grp = _group_scan(a.reshape(nb * r // SUBLANES, SUBLANES, C),
                               b.reshape(nb * r // SUBLANES, SUBLANES, C))
    a_grp = a_grp.reshape(nb, r, C)
    b_grp = b_grp.reshape(nb, r, C)
    state = jnp.broadcast_to(h_prev, (nb, SUBLANES, C))
    for g0 in range(0, r, SUBLANES):
        h_g = a_grp[:, g0:g0 + SUBLANES] * state + b_grp[:, g0:g0 + SUBLANES]
        h_ref[:, g0:g0 + SUBLANES, :] = h_g
        state = jnp.broadcast_to(h_g[:, SUBLANES - 1:SUBLANES], (nb, SUBLANES, C))
    return state[:, :1]


def _lru_kernel(x_ref, sh_ref, sc_ref, gt_ref, g_ref, win_ref, cw_ref, cb_ref, gw_ref, gab_ref,
                gxb_ref, lam_ref, wout_ref, conv0_ref, h0_ref,
                o_ref, hlast_ref, tail_ref, h_scr, tail_scr, hall_scr):
    @pl.when(pl.program_id(1) == 0)
    def _():
        h_scr[...] = h0_ref[...]
        tail_scr[...] = conv0_ref[...]

    x = x_ref[...]
    nb, r, _ = x.shape
    C = D_RNN
    u = _modulate(x, g_ref[0, 0], sh_ref[0, 0], sc_ref[0, 0])
    ub = u.reshape(nb * r, D_MODEL).astype(BF16)
    gate_br = _dot(ub, win_ref[0, :, :C])
    x_br = _dot(ub, win_ref[0, :, C:]).reshape(nb, r, C)

    xp = jnp.concatenate([tail_scr[...], x_br], axis=1)
    cw = cw_ref[0]
    off = SUBLANES - (CONV_W - 1)
    xc = cb_ref[0] + cw[0:1] * xp[:, off:off + r]
    for k in range(1, CONV_W):
        xc = xc + cw[k:k + 1] * xp[:, off + k:off + k + r]
    new_tail = x_br[:, r - SUBLANES:, :]
    tail_scr[...] = new_tail
    tail_ref[...] = new_tail

    xc2 = xc.reshape(nb * r, C)
    xcb = xc2.astype(BF16)
    sw = C // LRU_SUPER
    gates = [_dot(xcb[:, s * sw:(s + 1) * sw], gw_ref[0, s]) for s in range(LRU_SUPER)]
    rg = _sigmoid(jnp.concatenate([g[:, :sw] for g in gates], axis=-1) + gab_ref[0])
    ig = _sigmoid(jnp.concatenate([g[:, sw:] for g in gates], axis=-1) + gxb_ref[0])
    nl = -lam_ref[0]
    softplus = jnp.maximum(nl, 0.0) + jnp.log1p(jnp.exp(-jnp.abs(nl)))
    log_a = (-LRU_C * rg) * softplus
    a = jnp.exp(log_a)
    th = jnp.tanh(log_a)
    one_minus_a2 = (-2.0 * th) / (1.0 - th)
    root = jnp.where(one_minus_a2 > 0.0, one_minus_a2 * lax.rsqrt(one_minus_a2), 0.0)
    b_in = root * (ig * xc2)

    h_last = _linear_scan(a.reshape(nb, r, C), b_in.reshape(nb, r, C), h_scr[...], hall_scr)
    h_scr[...] = h_last
    hlast_ref[...] = h_last

    gb = gate_br
    gelu = 0.5 * gb * (1.0 + jnp.tanh(math.sqrt(2.0 / math.pi) * (gb + 0.044715 * (gb * gb * gb))))
    y = _dot((gelu * hall_scr[...].reshape(nb * r, C)).astype(BF16), wout_ref[0])
    o_ref[...] = x + gt_ref[0, 0] * y.reshape(nb, r, D_MODEL)


def _lru(x, rows, mods, norm_g4, layer, j, w_in, conv_w, conv_b, gate_w, ga_b, gx_b, lam, w_out,
         conv0, h0):
    nbt = x.shape[0]
    C = D_RNN
    vec = lambda: _const_spec((1, 1, C), (j, 0, 0))
    out, h_last, tail = pl.pallas_call(
        _lru_kernel,
        grid=rows.grid,
        in_specs=[
            rows.act(D_MODEL),
            rows.mod(layer, 3), rows.mod(layer, 4), rows.mod(layer, 5),
            _const_spec((1, 1, 1, D_MODEL), (layer, 1, 0, 0)),
            _const_spec((1, D_MODEL, 2 * C), (j, 0, 0)),
            _const_spec((1, CONV_W, C), (j, 0, 0)),
            vec(),
            _const_spec((1, LRU_SUPER, C // LRU_SUPER, 2 * C // LRU_SUPER), (j, 0, 0, 0)),
            vec(), vec(), vec(),
            _const_spec((1, C, D_MODEL), (j, 0, 0)),
            rows.per_seq(SUBLANES, C),
            rows.per_seq(1, C),
        ],
        out_specs=[rows.act(D_MODEL), rows.per_seq(1, C), rows.per_seq(SUBLANES, C)],
        out_shape=[
            jax.ShapeDtypeStruct(x.shape, F32),
            jax.ShapeDtypeStruct((nbt, 1, C), F32),
            jax.ShapeDtypeStruct((nbt, SUBLANES, C), F32),
        ],
        scratch_shapes=[pltpu.VMEM((rows.nb, 1, C), F32), pltpu.VMEM((rows.nb, SUBLANES, C), F32),
                        pltpu.VMEM((rows.nb, rows.r, C), F32)],
        compiler_params=_cparams(),
        name="lru",
    )(x, mods, mods, mods, norm_g4, w_in, conv_w, conv_b, gate_w, ga_b, gx_b, lam, w_out, conv0, h0)
    return out, h_last[:, 0], tail[:, SUBLANES - (CONV_W - 1):]


POOL_PAD = 16


def _pool_kernel(x_ref, sh_ref, sc_ref, gt_ref, g_ref, hist0_ref, w_ref, b_ref, ps_ref,
                 o_ref, st_ref, hist_scr, *, t_base):
    t = pl.program_id(1)

    @pl.when(t == 0)
    def _():
        hist_scr[...] = hist0_ref[...]

    x = x_ref[...]
    nb, r, _ = x.shape
    u = _modulate(x, g_ref[0, 0], sh_ref[0, 0], sc_ref[0, 0])
    ext = jnp.concatenate([hist_scr[...], u], axis=1)
    new_hist = u[:, r - POOL_PAD:, :]
    hist_scr[...] = new_hist
    st_ref[...] = new_hist

    row = lax.broadcasted_iota(jnp.int32, (1, r, 1), 1)
    t_glob = t_base + t * r + row
    gate = gt_ref[0, 0]
    s = ext
    for g, wnd in enumerate(POOL_WINDOWS):
        lo = g * POOL_GW
        s = s[:, :, (POOL_GW if g > 0 else 0):]
        s = s + pltpu.roll(s, wnd // 2, axis=1)
        cnt = jnp.minimum(t_glob + 1, wnd).astype(F32)
        mean = s[:, POOL_PAD:, :POOL_GW] / cnt
        d = mean - u[:, :, lo:lo + POOL_GW]
        y = _dot(d.reshape(nb * r, POOL_GW).astype(BF16), w_ref[0, g]).reshape(nb, r, POOL_GW)
        y = (y + b_ref[0, :, lo:lo + POOL_GW]) * ps_ref[0, :, lo:lo + POOL_GW]
        o_ref[:, :, lo:lo + POOL_GW] = x[:, :, lo:lo + POOL_GW] + gate[:, :, lo:lo + POOL_GW] * y


def _pool(x, rows, mods, norm_g4, layer, j, hist0, w, b, scale, t_base):
    nbt = x.shape[0]
    out, st = pl.pallas_call(
        functools.partial(_pool_kernel, t_base=t_base),
        grid=rows.grid,
        in_specs=[
            rows.act(D_MODEL),
            rows.mod(layer, 3), rows.mod(layer, 4), rows.mod(layer, 5),
            _const_spec((1, 1, 1, D_MODEL), (layer, 1, 0, 0)),
            rows.per_seq(POOL_PAD, D_MODEL),
            _const_spec((1, len(POOL_WINDOWS), POOL_GW, POOL_GW), (j, 0, 0, 0)),
            _const_spec((1, 1, D_MODEL), (j, 0, 0)),
            _const_spec((1, 1, D_MODEL), (j, 0, 0)),
        ],
        out_specs=[rows.act(D_MODEL), rows.per_seq(POOL_PAD, D_MODEL)],
        out_shape=[jax.ShapeDtypeStruct(x.shape, F32),
                   jax.ShapeDtypeStruct((nbt, POOL_PAD, D_MODEL), F32)],
        scratch_shapes=[pltpu.VMEM((rows.nb, POOL_PAD, D_MODEL), F32)],
        compiler_params=_cparams(),
        name="pool",
    )(x, mods, mods, mods, norm_g4, hist0, w, b, scale)
    return out, st[:, POOL_PAD - POOL_HIST:]


def _qkv_kernel(x_ref, sh_ref, sc_ref, g_ref, w_ref, qg_ref, kg_ref, seg_ref,
                q_ref, kf_ref, kb_ref, vf_ref, vb_ref, *, v_transposed):
    x = x_ref[...]
    nb, r, _ = x.shape
    u = _modulate(x, g_ref[0, 0], sh_ref[0, 0], sc_ref[0, 0])
    ub = u.reshape(nb * r, D_MODEL).astype(BF16)
    q = _dot(ub, w_ref[0, :, :D_MODEL])
    k = _dot(ub, w_ref[0, :, D_MODEL:2 * D_MODEL])
    v = _dot(ub, w_ref[0, :, 2 * D_MODEL:])

    def head_norm(z, gain):
        ms = _dot((z * z).astype(BF16), seg_ref[...])
        return (z * lax.rsqrt(ms + EPS)) * gain

    qn = head_norm(q, qg_ref[...]) * (HEAD_DIM ** -0.5 * LOG2E)
    kn = head_norm(k, kg_ref[...])
    lane = lax.broadcasted_iota(jnp.int32, (1, D_MODEL), 1)
    first = (lane % HEAD_W) < HEAD_DIM
    q_ref[:, 0] = jnp.where(first, qn, 0.0).astype(BF16).reshape(nb, r, D_MODEL)
    q_ref[:, 1] = jnp.where(first, 0.0, qn).astype(BF16).reshape(nb, r, D_MODEL)
    kf_ref[...] = kn.reshape(nb, r, D_MODEL)
    kb_ref[...] = kn.astype(BF16).reshape(nb, r, D_MODEL)
    vf_ref[...] = v.reshape(nb, r, D_MODEL)
    if v_transposed:
        for h in range(N_HEADS):
            vb_ref[0, h, :HEAD_W] = v[:, h * HEAD_W:(h + 1) * HEAD_W].T.astype(BF16)
            vb_ref[0, h, HEAD_W:] = jnp.ones((VT_ROWS - HEAD_W, r), BF16)
    else:
        vb_ref[...] = v.astype(BF16).reshape(nb, r, D_MODEL)


def _qkv(x, rows, mods, norm_g4, layer, j, w_in, qg, kg, seg, v_transposed):
    nbt, seq, _ = x.shape
    q_spec = pl.BlockSpec((rows.nb, 2, rows.r, D_MODEL), lambda b, t: (b, 0, t, 0))
    if v_transposed:
        assert rows.nb == 1
        vb_spec = pl.BlockSpec((1, N_HEADS, VT_ROWS, rows.r), lambda b, t: (b, 0, 0, t))
        vb_shape = jax.ShapeDtypeStruct((nbt, N_HEADS, VT_ROWS, seq), BF16)
    else:
        vb_spec = rows.act(D_MODEL)
        vb_shape = jax.ShapeDtypeStruct(x.shape, BF16)
    return pl.pallas_call(
        functools.partial(_qkv_kernel, v_transposed=v_transposed),
        grid=rows.grid,
        in_specs=[
            rows.act(D_MODEL),
            rows.mod(layer, 3), rows.mod(layer, 4),
            _const_spec((1, 1, 1, D_MODEL), (layer, 1, 0, 0)),
            _const_spec((1, D_MODEL, 3 * D_MODEL), (j, 0, 0)),
            _const_spec((1, D_MODEL), (0, 0)),
            _const_spec((1, D_MODEL), (0, 0)),
            _const_spec((D_MODEL, D_MODEL), (0, 0)),
        ],
        out_specs=[q_spec, rows.act(D_MODEL), rows.act(D_MODEL), rows.act(D_MODEL), vb_spec],
        out_shape=[
            jax.ShapeDtypeStruct((nbt, 2, seq, D_MODEL), BF16),
            jax.ShapeDtypeStruct(x.shape, F32),
            jax.ShapeDtypeStruct(x.shape, BF16),
            jax.ShapeDtypeStruct(x.shape, F32),
            vb_shape,
        ],
        compiler_params=_cparams(),
        name="qkv",
    )(x, mods, mods, norm_g4, w_in, qg, kg, seg)


_BUCKET_EDGES = (12, 16, 23, 32, 46, 64, 91)
_FAR_BUCKET = NUM_BUCKETS // 2 - 1


def _bucket_of(rel):
    n = abs(rel)
    small = NUM_BUCKETS // 4
    b = n if n < small else small + sum(n >= e for e in _BUCKET_EDGES)
    return b + (NUM_BUCKETS // 2 if rel > 0 else 0)


_BUCKET_RUNS = tuple((r, _bucket_of(r)) for r in range(-_BUCKET_EDGES[-1] + 1, _BUCKET_EDGES[-1] + 1)
                     if _bucket_of(r) != _bucket_of(r - 1))


def _bias_kernel(rb_ref, o_ref, *, rel0, mask_from, keys_on_rows, q_period, block_fill):
    h = pl.program_id(0)
    blk = pl.program_id(1)
    _, R, W = o_ref.shape

    for idx, fill in enumerate(block_fill):
        if fill is not None:
            @pl.when(blk == idx)
            def _(fill=fill):
                o_ref[0] = jnp.full((R, W), fill, F32)

    is_computed = functools.reduce(jnp.logical_or,
                                   [blk == idx for idx, fill in enumerate(block_fill) if fill is None])

    @pl.when(is_computed)
    def _():
        i = lax.broadcasted_iota(jnp.int32, (R, W), 0) + blk * R
        jj = lax.broadcasted_iota(jnp.int32, (R, W), 1)
        if keys_on_rows:
            key, qry = i, jj & (q_period - 1)
        else:
            key, qry = jj, i
        rel = key - qry + rel0
        out = jnp.full((R, W), rb_ref[_FAR_BUCKET, h], F32)
        for first_rel, bucket in _BUCKET_RUNS:
            out = jnp.where(rel >= first_rel, rb_ref[bucket, h], out)
        out = (out - rb_ref[_FAR_BUCKET, h]) * LOG2E
        if mask_from is not None:
            visible = ((key - mask_from) // CHUNK <= qry // CHUNK) | (key < mask_from)
            out = jnp.where(visible, out, NEG_INF)
        o_ref[0] = out


def _bias_tile(rel_bias, rows, width, rel0, mask_from, keys_on_rows=False, q_period=None,
               row_blocks=1):
    assert q_period is None or q_period & (q_period - 1) == 0
    assert rows % row_blocks == 0
    R = rows // row_blocks
    block_fill = []
    for blk in range(row_blocks):
        fill = None
        if keys_on_rows:
            lo, hi = blk * R, (blk + 1) * R
            unmasked = mask_from is None or hi <= mask_from
            if unmasked and hi - 1 + rel0 <= -_BUCKET_EDGES[-1]:
                fill = 0.0
            if mask_from is not None and lo >= mask_from and \
                    (lo - mask_from) // CHUNK > (q_period - 1) // CHUNK:
                fill = NEG_INF
        block_fill.append(fill)
    assert any(fill is None for fill in block_fill)
    return pl.pallas_call(
        functools.partial(_bias_kernel, rel0=rel0, mask_from=mask_from, keys_on_rows=keys_on_rows,
                          q_period=q_period, block_fill=tuple(block_fill)),
        grid=(N_HEADS, row_blocks),
        in_specs=[pl.BlockSpec(memory_space=pltpu.SMEM)],
        out_specs=pl.BlockSpec((1, R, width), lambda h, blk: (h, blk, 0)),
        out_shape=jax.ShapeDtypeStruct((N_HEADS, rows, width), F32),
        name="bias_tile",
    )(rel_bias)


def _diff_lambda(lp, lam_init):
    s1 = jnp.sum(lp[0:1] * lp[1:2], axis=-1, keepdims=True)
    s2 = jnp.sum(lp[2:3] * lp[3:4], axis=-1, keepdims=True)
    return jnp.exp(s1) - jnp.exp(s2) + lam_init


def _attn_finish(acc, l, nq, lam, subg, lam_init):
    o = acc / l
    o = o[:nq] - lam * o[nq:]
    ms = jnp.mean(o * o, axis=-1, keepdims=True)
    return ((o * lax.rsqrt(ms + EPS)) * subg) * (1.0 - lam_init)


def _attn_prompt_kernel(q_ref, k_ref, vt_ref, bias_ref, lam_ref, subg_ref, o_ref,
                        qt_scr, s0_scr, s1_scr, p0_scr, p1_scr, m_scr, a_scr, acc_scr, *, lam_init):
    qi = pl.program_id(2)
    tq, tk = ATT_TQ, ATT_TK
    width = 2 * tq
    n_blocks = qi + 1
    qt_scr[:, :tq] = q_ref[0, 0].astype(F32).T.astype(BF16)
    qt_scr[:, tq:] = q_ref[0, 1].astype(F32).T.astype(BF16)
    m_scr[...] = jnp.full(m_scr.shape, NEG_INF, F32)
    a_scr[...] = jnp.ones(a_scr.shape, F32)
    acc_scr[...] = jnp.zeros(acc_scr.shape, F32)
    p1_scr[...] = jnp.zeros(p1_scr.shape, BF16)

    def block_start(c):
        return pl.multiple_of(jnp.clip(c, 0, n_blocks - 1) * tk, tk)

    def scores(c, s_ref):
        sel = jnp.clip(c - (n_blocks - 3), 0, 3)
        s_ref[...] = _dot(k_ref[0, pl.ds(block_start(c), tk), :], qt_scr[...]) + bias_ref[0, sel]

    def weigh(c, p_ref):
        pv = _dot(vt_ref[0, 0, :, pl.ds(block_start(c), tk)], p_ref[...])
        acc_scr[...] = a_scr[...] * acc_scr[...] + pv

    def softmax(s_ref, p_ref):
        m8 = None
        for r0 in range(0, tk, ATT_ROW_CHUNK):
            part = s_ref[r0:r0 + ATT_ROW_CHUNK, :].reshape(ATT_ROW_CHUNK // SUBLANES, SUBLANES, width)
            part = jnp.max(part, axis=0)
            m8 = part if m8 is None else jnp.maximum(m8, part)
        m_old = m_scr[...]
        m_new = jnp.maximum(m_old, jnp.max(m8, axis=0, keepdims=True))
        a_scr[...] = jnp.exp2(m_old - m_new)
        m_scr[...] = m_new
        for r0 in range(0, tk, ATT_ROW_CHUNK):
            rows = slice(r0, r0 + ATT_ROW_CHUNK)
            p_ref[rows, :] = jnp.exp2((s_ref[rows, :] - m_new).astype(BF16))

    scores(0, s0_scr)

    def trip(t, carry):
        c = 2 * t
        weigh(c - 1, p1_scr)
        softmax(s0_scr, p0_scr)
        scores(c + 1, s1_scr)
        weigh(c, p0_scr)
        softmax(s1_scr, p1_scr)
        scores(c + 2, s0_scr)
        return carry

    n_trips = (n_blocks + 1) // 2
    lax.fori_loop(0, n_trips, trip, 0)
    weigh(2 * n_trips - 1, p1_scr)

    lam = _diff_lambda(lam_ref[0], lam_init)
    acc = acc_scr[...]
    o = acc[:HEAD_W] / acc[HEAD_W:HEAD_W + 1]
    o = o[:, :tq] - lam * o[:, tq:]
    ms = jnp.mean(o * o, axis=0, keepdims=True)
    on = ((o * lax.rsqrt(ms + EPS)) * subg_ref[...]) * (1.0 - lam_init)
    o_ref[0] = on.T.astype(BF16)


def _attn_prompt(q, kb, vt, bias, lam_p, subg_col, j, lam_init):
    nbt, _, seq, _ = q.shape
    tq, tk = ATT_TQ, ATT_TK
    return pl.pallas_call(
        functools.partial(_attn_prompt_kernel, lam_init=lam_init),
        grid=(nbt, N_HEADS, seq // tq),
        in_specs=[
            pl.BlockSpec((1, 2, tq, HEAD_W), lambda b, h, i: (b, 0, i, h)),
            pl.BlockSpec((1, seq, HEAD_W), lambda b, h, i: (b, 0, h)),
            pl.BlockSpec((1, 1, VT_ROWS, seq), lambda b, h, i: (b, h, 0, 0)),
            pl.BlockSpec((1, 4, tk, 2 * tq), lambda b, h, i: (h, 0, 0, 0)),
            pl.BlockSpec((1, 4, HEAD_DIM), lambda b, h, i: (j, 0, 0)),
            pl.BlockSpec((HEAD_W, 1), lambda b, h, i: (0, 0)),
        ],
        out_specs=pl.BlockSpec((1, tq, HEAD_W), lambda b, h, i: (b, i, h)),
        out_shape=jax.ShapeDtypeStruct((nbt, seq, D_MODEL), BF16),
        scratch_shapes=[pltpu.VMEM((HEAD_W, 2 * tq), BF16),
                        pltpu.VMEM((tk, 2 * tq), F32), pltpu.VMEM((tk, 2 * tq), F32),
                        pltpu.VMEM((tk, 2 * tq), BF16), pltpu.VMEM((tk, 2 * tq), BF16),
                        pltpu.VMEM((1, 2 * tq), F32), pltpu.VMEM((1, 2 * tq), F32),
                        pltpu.VMEM((VT_ROWS, 2 * tq), F32)],
        compiler_params=pltpu.CompilerParams(
            dimension_semantics=("arbitrary", "arbitrary", "arbitrary"),
            vmem_limit_bytes=VMEM_LIMIT_BYTES),
        name="attn_prompt",
    )(q, kb, vt, bias, lam_p, subg_col)


def _attn_sample_kernel(q_ref, ck_ref, cv_ref, kn_ref, vn_ref, bc_ref, bn_ref, lam_ref, subg_ref,
                        o_ref, *, lam_init):
    nq = q_ref.shape[2]
    lam = _diff_lambda(lam_ref[0], lam_init)
    for h in range(N_HEADS):
        cols = slice(h * HEAD_W, (h + 1) * HEAD_W)
        qq = q_ref[0, :, :, cols].reshape(2 * nq, HEAD_W)
        kc = ck_ref[0, :, cols].astype(BF16)
        vc = cv_ref[0, :, cols].astype(BF16)
        kn = kn_ref[0, :, cols]
        vn = vn_ref[0, :, cols]
        past = kc.shape[0]
        s_c = (_dot_nt(qq, kc).reshape(2, nq, past) + bc_ref[h][None]).reshape(2 * nq, past)
        s_n = (_dot_nt(qq, kn).reshape(2, nq, nq) + bn_ref[h][None]).reshape(2 * nq, nq)
        m = jnp.maximum(jnp.max(s_c, axis=-1, keepdims=True), jnp.max(s_n, axis=-1, keepdims=True))
        p_c = jnp.exp2(s_c - m)
        p_n = jnp.exp2(s_n - m)
        l = jnp.sum(p_c, axis=-1, keepdims=True) + jnp.sum(p_n, axis=-1, keepdims=True)
        acc = _dot(p_c.astype(BF16), vc) + _dot(p_n.astype(BF16), vn)
        o_ref[0, :, cols] = _attn_finish(acc, l, nq, lam, subg_ref[...], lam_init).astype(BF16)


def _attn_sample(q, ck, cv, kb, vb, bias_c, bias_n, lam_p, subg, j, lam_init):
    nbt, _, nq, _ = q.shape
    past = ck.shape[1]
    return pl.pallas_call(
        functools.partial(_attn_sample_kernel, lam_init=lam_init),
        grid=(nbt,),
        in_specs=[
            pl.BlockSpec((1, 2, nq, D_MODEL), lambda b: (b, 0, 0, 0)),
            pl.BlockSpec((1, past, D_MODEL), lambda b: (b, 0, 0)),
            pl.BlockSpec((1, past, D_MODEL), lambda b: (b, 0, 0)),
            pl.BlockSpec((1, nq, D_MODEL), lambda b: (b, 0, 0)),
            pl.BlockSpec((1, nq, D_MODEL), lambda b: (b, 0, 0)),
            pl.BlockSpec((N_HEADS, nq, past), lambda b: (0, 0, 0)),
            pl.BlockSpec((N_HEADS, nq, nq), lambda b: (0, 0, 0)),
            pl.BlockSpec((1, 4, HEAD_DIM), lambda b: (j, 0, 0)),
            pl.BlockSpec((1, HEAD_W), lambda b: (0, 0)),
        ],
        out_specs=pl.BlockSpec((1, nq, D_MODEL), lambda b: (b, 0, 0)),
        out_shape=jax.ShapeDtypeStruct((nbt, nq, D_MODEL), BF16),
        compiler_params=pltpu.CompilerParams(
            dimension_semantics=("arbitrary",), vmem_limit_bytes=VMEM_LIMIT_BYTES),
        name="attn_sample",
    )(q, ck, cv, kb, vb, bias_c, bias_n, lam_p, subg)


def _proj_res_kernel(x_ref, a_ref, gt_ref, w_ref, o_ref):
    x = x_ref[...]
    nb, r, _ = x.shape
    a = a_ref[...].reshape(nb * r, a_ref.shape[-1])
    y = _dot(a, w_ref[0])
    o_ref[...] = x + gt_ref[0, 0] * y.reshape(nb, r, D_MODEL)


def _proj_res(x, a, rows, mods, layer, j, w):
    return pl.pallas_call(
        _proj_res_kernel,
        grid=rows.grid,
        in_specs=[
            rows.act(D_MODEL), rows.act(a.shape[-1]), rows.mod(layer, 5),
            _const_spec((1,) + w.shape[1:], (j, 0, 0)),
        ],
        out_specs=rows.act(D_MODEL),
        out_shape=jax.ShapeDtypeStruct(x.shape, F32),
        compiler_params=_cparams(),
        name="proj_res",
    )(x, a, mods, w)


def kernel(x_prompt, x_sample, c_prompt, c_sample, state_lru_h, state_lru_conv, state_pool, cache_k, cache_v, ada_w, ada_b, norm_g, ffn_w_in, ffn_w_out, lru_w_in, lru_conv_w, lru_conv_b, lru_ga_w, lru_ga_b, lru_gx_w, lru_gx_b, lru_lambda, lru_w_out, pool_w, pool_b, pool_scale, attn_w_in, attn_q_g, attn_k_g, attn_lambda, attn_sub_g, attn_w_out, rel_bias):
    B, S, _ = x_prompt.shape
    SB, T, _ = x_sample.shape
    P = cache_k.shape[2]
    assert SB == ADA_PROMPT_ROW0 and ADA_PROMPT_ROW0 + B <= ADA_ROWS
    n_a, n_b, n_c = lru_w_in.shape[0], pool_w.shape[0], attn_w_in.shape[0]

    c_all = jnp.concatenate(
        [c_sample, c_prompt, jnp.zeros((ADA_ROWS - SB - B, D_MODEL), F32)], axis=0)
    mods = _adaln(c_all, ada_w, ada_b)
    norm_g4 = norm_g.reshape(DEPTH, 3, 1, D_MODEL)

    ffn_in_b = ffn_w_in.astype(BF16)
    ffn_out_b = ffn_w_out.astype(BF16)
    lru_in_b = lru_w_in.astype(BF16)
    lru_out_b = lru_w_out.astype(BF16)
    per_super = LRU_BLOCKS // LRU_SUPER
    eye = jnp.eye(per_super, dtype=F32)

    def super_blocks(w):
        w5 = w.reshape(w.shape[0], LRU_SUPER, per_super, LRU_BS, LRU_BS)
        full = jnp.einsum('nshij,hg->nshigj', w5, eye)
        return full.reshape(w.shape[0], LRU_SUPER, per_super * LRU_BS, per_super * LRU_BS)

    lru_gate_b = jnp.concatenate([super_blocks(lru_ga_w), super_blocks(lru_gx_w)], axis=-1).astype(BF16)
    vec3 = lambda v: v.reshape(v.shape[0], 1, v.shape[-1])
    pool_w_b = pool_w.astype(BF16)
    attn_in_b = attn_w_in.astype(BF16)
    attn_out_b = attn_w_out.astype(BF16)
    seg = jnp.kron(jnp.eye(D_MODEL // HEAD_DIM, dtype=F32),
                   jnp.full((HEAD_DIM, HEAD_DIM), 1.0 / HEAD_DIM, F32)).astype(BF16)

    rows_p = {r: _Rows(B, S, r, True) for r in {FFN_ROWS, LRU_ROWS, POOL_ROWS, QKV_ROWS, OUT_ROWS}}
    rows_s = _Rows(SB, T, T, False)

    xp, xs = x_prompt, x_sample
    h_p, h_s, cv_p, cv_s, pl_p, pl_s, k_p, v_p, k_s, v_s = ([] for _ in range(10))
    for l in range(DEPTH):
        kind, j = l % 3, l // 3
        xp = _ffn(xp, rows_p[FFN_ROWS], mods, norm_g4, ffn_in_b, ffn_out_b, l, 0)
        xs = _ffn(xs, rows_s, mods, norm_g4, ffn_in_b, ffn_out_b, l, 0)
        if kind == 0:
            prm = (lru_in_b, lru_conv_w, vec3(lru_conv_b), lru_gate_b, vec3(lru_ga_b),
                   vec3(lru_gx_b), vec3(lru_lambda), lru_out_b)
            xp, hp, bp = _lru(xp, rows_p[LRU_ROWS], mods, norm_g4, l, j, *prm,
                              jnp.zeros((B, SUBLANES, D_RNN), F32), jnp.zeros((B, 1, D_RNN), F32))
            conv0 = jnp.pad(state_lru_conv[j], ((0, 0), (SUBLANES - (CONV_W - 1), 0), (0, 0)))
            xs, hs, bs = _lru(xs, rows_s, mods, norm_g4, l, j, *prm, conv0,
                              state_lru_h[j][:, None, :])
            h_p.append(hp), h_s.append(hs), cv_p.append(bp), cv_s.append(bs)
        elif kind == 1:
            prm = (pool_w_b, vec3(pool_b), vec3(pool_scale))
            xp, sp = _pool(xp, rows_p[POOL_ROWS], mods, norm_g4, l, j,
                           jnp.zeros((B, POOL_PAD, D_MODEL), F32), *prm, 0)
            hist0 = jnp.pad(state_pool[j], ((0, 0), (POOL_PAD - POOL_HIST, 0), (0, 0)))
            xs, ss = _pool(xs, rows_s, mods, norm_g4, l, j, hist0, *prm, POOL_HIST)
            pl_p.append(sp), pl_s.append(ss)
        else:
            lam_init = 0.8 - 0.6 * math.exp(-0.3 * l)
            qg = jnp.tile(attn_q_g[j], D_MODEL // HEAD_DIM)[None]
            kg = jnp.tile(attn_k_g[j], D_MODEL // HEAD_DIM)[None]
            subg = attn_sub_g[j][None]
            q, kf, kb, vf, vt = _qkv(xp, rows_p[QKV_ROWS], mods, norm_g4, l, j, attn_in_b, qg, kg, seg,
                                     True)
            bias_p = _bias_tile(rel_bias, 4 * ATT_TK, 2 * ATT_TQ, -2 * ATT_TK, 2 * ATT_TK,
                                keys_on_rows=True, q_period=ATT_TQ, row_blocks=4)
            bias_p = bias_p.reshape(N_HEADS, 4, ATT_TK, 2 * ATT_TQ)
            o = _attn_prompt(q, kb, vt, bias_p, attn_lambda, attn_sub_g[j][:, None], j, lam_init)
            xp = _proj_res(xp, o, rows_p[OUT_ROWS], mods, l, j, attn_out_b)
            k_p.append(kf.reshape(B, S, N_HEADS, HEAD_W)), v_p.append(vf.reshape(B, S, N_HEADS, HEAD_W))
            q, kf, kb, vf, vb = _qkv(xs, rows_s, mods, norm_g4, l, j, attn_in_b, qg, kg, seg, False)
            bias_c = _bias_tile(rel_bias, T, P, -P, None)
            bias_n = _bias_tile(rel_bias, T, T, 0, None)
            o = _attn_sample(q, cache_k[j].reshape(SB, P, D_MODEL), cache_v[j].reshape(SB, P, D_MODEL),
                             kb, vb, bias_c, bias_n, attn_lambda, subg, j, lam_init)
            xs = _proj_res(xs, o, rows_s, mods, l, j, attn_out_b)
            k_s.append(kf.reshape(SB, T, N_HEADS, HEAD_W)), v_s.append(vf.reshape(SB, T, N_HEADS, HEAD_W))
        xp = _ffn(xp, rows_p[FFN_ROWS], mods, norm_g4, ffn_in_b, ffn_out_b, l, 1)
        xs = _ffn(xs, rows_s, mods, norm_g4, ffn_in_b, ffn_out_b, l, 1)
    return (xp, xs, jnp.stack(h_p), jnp.stack(h_s), jnp.stack(cv_p), jnp.stack(cv_s),
            jnp.stack(pl_p), jnp.stack(pl_s), jnp.stack(k_p), jnp.stack(v_p),
            jnp.stack(k_s), jnp.stack(v_s))
```

```python
import functools
import math

import jax
import jax.numpy as jnp
from jax import lax
from jax.experimental import pallas as pl
from jax.experimental.pallas import tpu as pltpu

F32 = jnp.float32
BF16 = jnp.bfloat16

D_MODEL = 1024
DEPTH = 4
D_FF = 2816
D_RNN = 1280
LRU_BLOCKS = 16
LRU_BS = D_RNN // LRU_BLOCKS
LRU_SUPER = 2
CONV_W = 4
LRU_C = 8.0
POOL_WINDOWS = (2, 4, 8, 16)
POOL_GW = D_MODEL // len(POOL_WINDOWS)
POOL_HIST = 15
N_HEADS = 8
HEAD_DIM = D_MODEL // (2 * N_HEADS)
HEAD_W = 2 * HEAD_DIM
NUM_BUCKETS = 32
MAX_DISTANCE = 128
CHUNK = 64
EPS = 1e-6
NEG_INF = -1e30

ADA_ROWS = 16
ADA_PROMPT_ROW0 = 8

VMEM_LIMIT_BYTES = 60 * 1024 * 1024
SUBLANES = 8

FFN_ROWS = 1024
FFN_CHUNK = 256
LRU_ROWS = 256
POOL_ROWS = 512
QKV_ROWS = 512
ATT_TQ = 512
ATT_TK = 512
ATT_ROW_CHUNK = 256
VT_ROWS = HEAD_W + 16
LOG2E = math.log2(math.e)
OUT_ROWS = 512


def _cparams():
    return pltpu.CompilerParams(
        dimension_semantics=("arbitrary", "arbitrary"),
        vmem_limit_bytes=VMEM_LIMIT_BYTES,
    )


def _sigmoid(x):
    return 0.5 * jnp.tanh(0.5 * x) + 0.5


def _modulate(x, g, shift, scale):
    ms = jnp.mean(x * x, axis=-1, keepdims=True)
    y = x * lax.rsqrt(ms + EPS)
    return (y * g) * (1.0 + scale) + shift


def _dot(a, b):
    return jnp.dot(a, b, preferred_element_type=F32)


def _dot_nt(a, b):
    return lax.dot_general(a, b, (((1,), (1,)), ((), ())), preferred_element_type=F32)


def _adaln_kernel(c_ref, w_ref, b_ref, o_ref):
    c = c_ref[...]
    s = c * _sigmoid(c)
    y = _dot(s.astype(BF16), w_ref[0].astype(BF16))
    o_ref[0, 0] = y + b_ref[0, 0]


def _adaln(c_all, ada_w, ada_b):
    n_vec = 9
    out = pl.pallas_call(
        _adaln_kernel,
        grid=(DEPTH, n_vec),
        in_specs=[
            pl.BlockSpec((ADA_ROWS, D_MODEL), lambda l, k: (0, 0)),
            pl.BlockSpec((1, D_MODEL, D_MODEL), lambda l, k: (l, 0, k)),
            pl.BlockSpec((1, 1, 1, D_MODEL), lambda l, k: (l, k, 0, 0)),
        ],
        out_specs=pl.BlockSpec((1, 1, ADA_ROWS, D_MODEL), lambda l, k: (l, k, 0, 0)),
        out_shape=jax.ShapeDtypeStruct((DEPTH, n_vec, ADA_ROWS, D_MODEL), F32),
        compiler_params=_cparams(),
        name="adaln",
    )(c_all, ada_w, ada_b.reshape(DEPTH, n_vec, 1, D_MODEL))
    return out.reshape(DEPTH, n_vec, ADA_ROWS, 1, D_MODEL)


class _Rows:
    def __init__(self, batch, seq, rows, prompt):
        self.prompt = prompt
        if prompt:
            assert seq % rows == 0
            self.nb, self.r = 1, rows
            self.grid = (batch, seq // rows)
        else:
            self.nb, self.r = batch, seq
            self.grid = (1, 1)

    def act(self, width):
        return pl.BlockSpec((self.nb, self.r, width), lambda b, t: (b, t, 0))

    def per_seq(self, rows, width):
        return pl.BlockSpec((self.nb, rows, width), lambda b, t: (b, 0, 0))

    def mod(self, layer, k):
        if self.prompt:
            return pl.BlockSpec((1, 1, 1, 1, D_MODEL),
                                lambda b, t: (layer, k, ADA_PROMPT_ROW0 + b, 0, 0))
        return pl.BlockSpec((1, 1, self.nb, 1, D_MODEL), lambda b, t: (layer, k, 0, 0, 0))


def _const_spec(shape, index, single_buffer=False):
    if single_buffer:
        return pl.BlockSpec(shape, lambda b, t: index, pipeline_mode=pl.Buffered(1))
    return pl.BlockSpec(shape, lambda b, t: index)


def _ffn_kernel(x_ref, sh_ref, sc_ref, gt_ref, g_ref, win_ref, wout_ref, o_ref):
    x = x_ref[...]
    nb, r, _ = x.shape
    u = _modulate(x, g_ref[0, 0], sh_ref[0, 0], sc_ref[0, 0])
    ub = u.reshape(nb * r, D_MODEL).astype(BF16)
    acc = None
    for c in range(D_FF // FFN_CHUNK):
        lo = c * FFN_CHUNK
        a = _dot(ub, win_ref[0, 0, :, lo:lo + FFN_CHUNK])
        b = _dot(ub, win_ref[0, 0, :, D_FF + lo:D_FF + lo + FFN_CHUNK])
        h = ((a * _sigmoid(a)) * b).astype(BF16)
        y = _dot(h, wout_ref[0, 0, lo:lo + FFN_CHUNK, :])
        acc = y if acc is None else acc + y
    o_ref[...] = x + (0.5 * gt_ref[0, 0]) * acc.reshape(nb, r, D_MODEL)


def _ffn(x, rows, mods, norm_g4, w_in, w_out, layer, which):
    k0 = 0 if which == 0 else 6
    g_idx = 0 if which == 0 else 2
    return pl.pallas_call(
        _ffn_kernel,
        grid=rows.grid,
        in_specs=[
            rows.act(D_MODEL),
            rows.mod(layer, k0), rows.mod(layer, k0 + 1), rows.mod(layer, k0 + 2),
            _const_spec((1, 1, 1, D_MODEL), (layer, g_idx, 0, 0)),
            _const_spec((1, 1, D_MODEL, 2 * D_FF), (layer, which, 0, 0), True),
            _const_spec((1, 1, D_FF, D_MODEL), (layer, which, 0, 0), True),
        ],
        out_specs=rows.act(D_MODEL),
        out_shape=jax.ShapeDtypeStruct(x.shape, F32),
        compiler_params=_cparams(),
        name="ffn",
    )(x, mods, mods, mods, norm_g4, w_in, w_out)


def _group_scan(a, b):
    row = lax.broadcasted_iota(jnp.int32, a.shape, 1)
    d = 1
    while d < SUBLANES:
        keep = row >= d
        a_sh = jnp.where(keep, pltpu.roll(a, d, axis=1), 1.0)
        b_sh = jnp.where(keep, pltpu.roll(b, d, axis=1), 0.0)
        b = a * b_sh + b
        a = a * a_sh
        d *= 2
    return a, b


def _linear_scan(a, b, h_prev, h_ref):
    nb, r, C = a.shape
    a_grp, b_grp = _group_scan(a.reshape(nb * r // SUBLANES, SUBLANES, C),
                               b.reshape(nb * r // SUBLANES, SUBLANES, C))
    a_grp = a_grp.reshape(nb, r, C)
    b_grp = b_grp.reshape(nb, r, C)
    state = jnp.broadcast_to(h_prev, (nb, SUBLANES, C))
    for g0 in range(0, r, SUBLANES):
        h_g = a_grp[:, g0:g0 + SUBLANES] * state + b_grp[:, g0:g0 + SUBLANES]
        h_ref[:, g0:g0 + SUBLANES, :] = h_g
        state = jnp.broadcast_to(h_g[:, SUBLANES - 1:SUBLANES], (nb, SUBLANES, C))
    return state[:, :1]


def _lru_kernel(x_ref, sh_ref, sc_ref, gt_ref, g_ref, win_ref, cw_ref, cb_ref, gw_ref, gab_ref,
                gxb_ref, lam_ref, wout_ref, conv0_ref, h0_ref,
                o_ref, hlast_ref, tail_ref, h_scr, tail_scr, hall_scr):
    @pl.when(pl.program_id(1) == 0)
    def _():
        h_scr[...] = h0_ref[...]
        tail_scr[...] = conv0_ref[...]

    x = x_ref[...]
    nb, r, _ = x.shape
    C = D_RNN
    u = _modulate(x, g_ref[0, 0], sh_ref[0, 0], sc_ref[0, 0])
    ub = u.reshape(nb * r, D_MODEL).astype(BF16)
    gate_br = _dot(ub, win_ref[0, :, :C])
    x_br = _dot(ub, win_ref[0, :, C:]).reshape(nb, r, C)

    xp = jnp.concatenate([tail_scr[...], x_br], axis=1)
    cw = cw_ref[0]
    off = SUBLANES - (CONV_W - 1)
    xc = cb_ref[0] + cw[0:1] * xp[:, off:off + r]
    for k in range(1, CONV_W):
        xc = xc + cw[k:k + 1] * xp[:, off + k:off + k + r]
    new_tail = x_br[:, r - SUBLANES:, :]
    tail_scr[...] = new_tail
    tail_ref[...] = new_tail

    xc2 = xc.reshape(nb * r, C)
    xcb = xc2.astype(BF16)
    sw = C // LRU_SUPER
    gates = [_dot(xcb[:, s * sw:(s + 1) * sw], gw_ref[0, s]) for s in range(LRU_SUPER)]
    rg = _sigmoid(jnp.concatenate([g[:, :sw] for g in gates], axis=-1) + gab_ref[0])
    ig = _sigmoid(jnp.concatenate([g[:, sw:] for g in gates], axis=-1) + gxb_ref[0])
    nl = -lam_ref[0]
    softplus = jnp.maximum(nl, 0.0) + jnp.log1p(jnp.exp(-jnp.abs(nl)))
    log_a = (-LRU_C * rg) * softplus
    a = jnp.exp(log_a)
    th = jnp.tanh(log_a)
    one_minus_a2 = (-2.0 * th) / (1.0 - th)
    root = jnp.where(one_minus_a2 > 0.0, one_minus_a2 * lax.rsqrt(one_minus_a2), 0.0)
    b_in = root * (ig * xc2)

    h_last = _linear_scan(a.reshape(nb, r, C), b_in.reshape(nb, r, C), h_scr[...], hall_scr)
    h_scr[...] = h_last
    hlast_ref[...] = h_last

    gb = gate_br
    gelu = 0.5 * gb * (1.0 + jnp.tanh(math.sqrt(2.0 / math.pi) * (gb + 0.044715 * (gb * gb * gb))))
    y = _dot((gelu * hall_scr[...].reshape(nb * r, C)).astype(BF16), wout_ref[0])
    o_ref[...] = x + gt_ref[0, 0] * y.reshape(nb, r, D_MODEL)


def _lru(x, rows, mods, norm_g4, layer, j, w_in, conv_w, conv_b, gate_w, ga_b, gx_b, lam, w_out,
         conv0, h0):
    nbt = x.shape[0]
    C = D_RNN
    vec = lambda: _const_spec((1, 1, C), (j, 0, 0))
    out, h_last, tail = pl.pallas_call(
        _lru_kernel,
        grid=rows.grid,
        in_specs=[
            rows.act(D_MODEL),
            rows.mod(layer, 3), rows.mod(layer, 4), rows.mod(layer, 5),
            _const_spec((1, 1, 1, D_MODEL), (layer, 1, 0, 0)),
            _const_spec((1, D_MODEL, 2 * C), (j, 0, 0)),
            _const_spec((1, CONV_W, C), (j, 0, 0)),
            vec(),
            _const_spec((1, LRU_SUPER, C // LRU_SUPER, 2 * C // LRU_SUPER), (j, 0, 0, 0)),
            vec(), vec(), vec(),
            _const_spec((1, C, D_MODEL), (j, 0, 0)),
            rows.per_seq(SUBLANES, C),
            rows.per_seq(1, C),
        ],
        out_specs=[rows.act(D_MODEL), rows.per_seq(1, C), rows.per_seq(SUBLANES, C)],
        out_shape=[
            jax.ShapeDtypeStruct(x.shape, F32),
            jax.ShapeDtypeStruct((nbt, 1, C), F32),
            jax.ShapeDtypeStruct((nbt, SUBLANES, C), F32),
        ],
        scratch_shapes=[pltpu.VMEM((rows.nb, 1, C), F32), pltpu.VMEM((rows.nb, SUBLANES, C), F32),
                        pltpu.VMEM((rows.nb, rows.r, C), F32)],
        compiler_params=_cparams(),
        name="lru",
    )(x, mods, mods, mods, norm_g4, w_in, conv_w, conv_b, gate_w, ga_b, gx_b, lam, w_out, conv0, h0)
    return out, h_last[:, 0], tail[:, SUBLANES - (CONV_W - 1):]


POOL_PAD = 16


def _pool_kernel(x_ref, sh_ref, sc_ref, gt_ref, g_ref, hist0_ref, w_ref, b_ref, ps_ref,
                 o_ref, st_ref, hist_scr, *, t_base):
    t = pl.program_id(1)

    @pl.when(t == 0)
    def _():
        hist_scr[...] = hist0_ref[...]

    x = x_ref[...]
    nb, r, _ = x.shape
    u = _modulate(x, g_ref[0, 0], sh_ref[0, 0], sc_ref[0, 0])
    ext = jnp.concatenate([hist_scr[...], u], axis=1)
    new_hist = u[:, r - POOL_PAD:, :]
    hist_scr[...] = new_hist
    st_ref[...] = new_hist

    row = lax.broadcasted_iota(jnp.int32, (1, r, 1), 1)
    t_glob = t_base + t * r + row
    gate = gt_ref[0, 0]
    s = ext
    for g, wnd in enumerate(POOL_WINDOWS):
        lo = g * POOL_GW
        s = s[:, :, (POOL_GW if g > 0 else 0):]
        s = s + pltpu.roll(s, wnd // 2, axis=1)
        cnt = jnp.minimum(t_glob + 1, wnd).astype(F32)
        mean = s[:, POOL_PAD:, :POOL_GW] / cnt
        d = mean - u[:, :, lo:lo + POOL_GW]
        y = _dot(d.reshape(nb * r, POOL_GW).astype(BF16), w_ref[0, g]).reshape(nb, r, POOL_GW)
        y = (y + b_ref[0, :, lo:lo + POOL_GW]) * ps_ref[0, :, lo:lo + POOL_GW]
        o_ref[:, :, lo:lo + POOL_GW] = x[:, :, lo:lo + POOL_GW] + gate[:, :, lo:lo + POOL_GW] * y


def _pool(x, rows, mods, norm_g4, layer, j, hist0, w, b, scale, t_base):
    nbt = x.shape[0]
    out, st = pl.pallas_call(
        functools.partial(_pool_kernel, t_base=t_base),
        grid=rows.grid,
        in_specs=[
            rows.act(D_MODEL),
            rows.mod(layer, 3), rows.mod(layer, 4), rows.mod(layer, 5),
            _const_spec((1, 1, 1, D_MODEL), (layer, 1, 0, 0)),
            rows.per_seq(POOL_PAD, D_MODEL),
            _const_spec((1, len(POOL_WINDOWS), POOL_GW, POOL_GW), (j, 0, 0, 0)),
            _const_spec((1, 1, D_MODEL), (j, 0, 0)),
            _const_spec((1, 1, D_MODEL), (j, 0, 0)),
        ],
        out_specs=[rows.act(D_MODEL), rows.per_seq(POOL_PAD, D_MODEL)],
        out_shape=[jax.ShapeDtypeStruct(x.shape, F32),
                   jax.ShapeDtypeStruct((nbt, POOL_PAD, D_MODEL), F32)],
        scratch_shapes=[pltpu.VMEM((rows.nb, POOL_PAD, D_MODEL), F32)],
        compiler_params=_cparams(),
        name="pool",
    )(x, mods, mods, mods, norm_g4, hist0, w, b, scale)
    return out, st[:, POOL_PAD - POOL_HIST:]


def _qkv_kernel(x_ref, sh_ref, sc_ref, g_ref, w_ref, qg_ref, kg_ref, seg_ref,
                q_ref, kf_ref, kb_ref, vf_ref, vb_ref, *, v_transposed):
    x = x_ref[...]
    nb, r, _ = x.shape
    u = _modulate(x, g_ref[0, 0], sh_ref[0, 0], sc_ref[0, 0])
    ub = u.reshape(nb * r, D_MODEL).astype(BF16)
    q = _dot(ub, w_ref[0, :, :D_MODEL])
    k = _dot(ub, w_ref[0, :, D_MODEL:2 * D_MODEL])
    v = _dot(ub, w_ref[0, :, 2 * D_MODEL:])

    def head_norm(z, gain):
        ms = _dot((z * z).astype(BF16), seg_ref[...])
        return (z * lax.rsqrt(ms + EPS)) * gain

    qn = head_norm(q, qg_ref[...]) * (HEAD_DIM ** -0.5 * LOG2E)
    kn = head_norm(k, kg_ref[...])
    lane = lax.broadcasted_iota(jnp.int32, (1, D_MODEL), 1)
    first = (lane % HEAD_W) < HEAD_DIM
    q_ref[:, 0] = jnp.where(first, qn, 0.0).astype(BF16).reshape(nb, r, D_MODEL)
    q_ref[:, 1] = jnp.where(first, 0.0, qn).astype(BF16).reshape(nb, r, D_MODEL)
    kf_ref[...] = kn.reshape(nb, r, D_MODEL)
    kb_ref[...] = kn.astype(BF16).reshape(nb, r, D_MODEL)
    vf_ref[...] = v.reshape(nb, r, D_MODEL)
    if v_transposed:
        for h in range(N_HEADS):
            vb_ref[0, h, :HEAD_W] = v[:, h * HEAD_W:(h + 1) * HEAD_W].T.astype(BF16)
            vb_ref[0, h, HEAD_W:] = jnp.ones((VT_ROWS - HEAD_W, r), BF16)
    else:
        vb_ref[...] = v.astype(BF16).reshape(nb, r, D_MODEL)


def _qkv(x, rows, mods, norm_g4, layer, j, w_in, qg, kg, seg, v_transposed):
    nbt, seq, _ = x.shape
    q_spec = pl.BlockSpec((rows.nb, 2, rows.r, D_MODEL), lambda b, t: (b, 0, t, 0))
    if v_transposed:
        assert rows.nb == 1
        vb_spec = pl.BlockSpec((1, N_HEADS, VT_ROWS, rows.r), lambda b, t: (b, 0, 0, t))
        vb_shape = jax.ShapeDtypeStruct((nbt, N_HEADS, VT_ROWS, seq), BF16)
    else:
        vb_spec = rows.act(D_MODEL)
        vb_shape = jax.ShapeDtypeStruct(x.shape, BF16)
    return pl.pallas_call(
        functools.partial(_qkv_kernel, v_transposed=v_transposed),
        grid=rows.grid,
        in_specs=[
            rows.act(D_MODEL),
            rows.mod(layer, 3), rows.mod(layer, 4),
            _const_spec((1, 1, 1, D_MODEL), (layer, 1, 0, 0)),
            _const_spec((1, D_MODEL, 3 * D_MODEL), (j, 0, 0)),
            _const_spec((1, D_MODEL), (0, 0)),
            _const_spec((1, D_MODEL), (0, 0)),
            _const_spec((D_MODEL, D_MODEL), (0, 0)),
        ],
        out_specs=[q_spec, rows.act(D_MODEL), rows.act(D_MODEL), rows.act(D_MODEL), vb_spec],
        out_shape=[
            jax.ShapeDtypeStruct((nbt, 2, seq, D_MODEL), BF16),
            jax.ShapeDtypeStruct(x.shape, F32),
            jax.ShapeDtypeStruct(x.shape, BF16),
            jax.ShapeDtypeStruct(x.shape, F32),
            vb_shape,
        ],
        compiler_params=_cparams(),
        name="qkv",
    )(x, mods, mods, norm_g4, w_in, qg, kg, seg)


_BUCKET_EDGES = (12, 16, 23, 32, 46, 64, 91)
_FAR_BUCKET = NUM_BUCKETS // 2 - 1


def _bucket_of(rel):
    n = abs(rel)
    small = NUM_BUCKETS // 4
    b = n if n < small else small + sum(n >= e for e in _BUCKET_EDGES)
    return b + (NUM_BUCKETS // 2 if rel > 0 else 0)


_BUCKET_RUNS = tuple((r, _bucket_of(r)) for r in range(-_BUCKET_EDGES[-1] + 1, _BUCKET_EDGES[-1] + 1)
                     if _bucket_of(r) != _bucket_of(r - 1))


def _bias_kernel(rb_ref, o_ref, *, rel0, mask_from, keys_on_rows, q_period, block_fill):
    h = pl.program_id(0)
    blk = pl.program_id(1)
    _, R, W = o_ref.shape

    for idx, fill in enumerate(block_fill):
        if fill is not None:
            @pl.when(blk == idx)
            def _(fill=fill):
                o_ref[0] = jnp.full((R, W), fill, F32)

    is_computed = functools.reduce(jnp.logical_or,
                                   [blk == idx for idx, fill in enumerate(block_fill) if fill is None])

    @pl.when(is_computed)
    def _():
        i = lax.broadcasted_iota(jnp.int32, (R, W), 0) + blk * R
        jj = lax.broadcasted_iota(jnp.int32, (R, W), 1)
        if keys_on_rows:
            key, qry = i, jj & (q_period - 1)
        else:
            key, qry = jj, i
        rel = key - qry + rel0
        out = jnp.full((R, W), rb_ref[_FAR_BUCKET, h], F32)
        for first_rel, bucket in _BUCKET_RUNS:
            out = jnp.where(rel >= first_rel, rb_ref[bucket, h], out)
        out = (out - rb_ref[_FAR_BUCKET, h]) * LOG2E
        if mask_from is not None:
            visible = ((key - mask_from) // CHUNK <= qry // CHUNK) | (key < mask_from)
            out = jnp.where(visible, out, NEG_INF)
        o_ref[0] = out


def _bias_tile(rel_bias, rows, width, rel0, mask_from, keys_on_rows=False, q_period=None,
               row_blocks=1):
    assert q_period is None or q_period & (q_period - 1) == 0
    assert rows % row_blocks == 0
    R = rows // row_blocks
    block_fill = []
    for blk in range(row_blocks):
        fill = None
        if keys_on_rows:
            lo, hi = blk * R, (blk + 1) * R
            unmasked = mask_from is None or hi <= mask_from
            if unmasked and hi - 1 + rel0 <= -_BUCKET_EDGES[-1]:
                fill = 0.0
            if mask_from is not None and lo >= mask_from and \
                    (lo - mask_from) // CHUNK > (q_period - 1) // CHUNK:
                fill = NEG_INF
        block_fill.append(fill)
    assert any(fill is None for fill in block_fill)
    return pl.pallas_call(
        functools.partial(_bias_kernel, rel0=rel0, mask_from=mask_from, keys_on_rows=keys_on_rows,
                          q_period=q_period, block_fill=tuple(block_fill)),
        grid=(N_HEADS, row_blocks),
        in_specs=[pl.BlockSpec(memory_space=pltpu.SMEM)],
        out_specs=pl.BlockSpec((1, R, width), lambda h, blk: (h, blk, 0)),
        out_shape=jax.ShapeDtypeStruct((N_HEADS, rows, width), F32),
        name="bias_tile",
    )(rel_bias)


def _diff_lambda(lp, lam_init):
    s1 = jnp.sum(lp[0:1] * lp[1:2], axis=-1, keepdims=True)
    s2 = jnp.sum(lp[2:3] * lp[3:4], axis=-1, keepdims=True)
    return jnp.exp(s1) - jnp.exp(s2) + lam_init


def _attn_finish(acc, l, nq, lam, subg, lam_init):
    o = acc / l
    o = o[:nq] - lam * o[nq:]
    ms = jnp.mean(o * o, axis=-1, keepdims=True)
    return ((o * lax.rsqrt(ms + EPS)) * subg) * (1.0 - lam_init)


def _attn_prompt_kernel(q_ref, k_ref, vt_ref, bias_ref, lam_ref, subg_ref, o_ref,
                        qt_scr, s0_scr, s1_scr, mx0_scr, mx1_scr, m_scr, acc_scr, *, lam_init):
    qi = pl.program_id(2)
    tq, tk = ATT_TQ, ATT_TK
    width = 2 * tq
    n_blocks = qi + 1
    qt_scr[:, :tq] = q_ref[0, 0].astype(F32).T.astype(BF16)
    qt_scr[:, tq:] = q_ref[0, 1].astype(F32).T.astype(BF16)
    m_scr[...] = jnp.full(m_scr.shape, NEG_INF, F32)
    acc_scr[...] = jnp.zeros(acc_scr.shape, F32)

    def block_start(c):
        return jnp.clip(c, 0, n_blocks - 1) * tk

    def scores(c, s_ref, mx_ref):
        sel = jnp.clip(c - (n_blocks - 3), 0, 3)
        bias = bias_ref[0, sel]
        start = pl.multiple_of(block_start(c), tk)
        s = _dot(k_ref[0, pl.ds(start, tk), :], qt_scr[...]) + jnp.concatenate([bias, bias], axis=1)
        s_ref[...] = s
        mx_ref[...] = jnp.max(s.reshape(tk // SUBLANES, SUBLANES, width), axis=0)

    def absorb(c, s_ref, mx_ref):
        m_old = m_scr[...]
        m_new = jnp.maximum(m_old, jnp.max(mx_ref[...], axis=0, keepdims=True))
        alpha = jnp.exp2(m_old - m_new)
        m_scr[...] = m_new
        pv = None
        for r0 in range(0, tk, ATT_ROW_CHUNK):
            p = jnp.exp2((s_ref[r0:r0 + ATT_ROW_CHUNK, :] - m_new).astype(BF16))
            start = pl.multiple_of(block_start(c) + r0, ATT_ROW_CHUNK)
            part = _dot(vt_ref[0, 0, :, pl.ds(start, ATT_ROW_CHUNK)], p)
            pv = part if pv is None else pv + part
        acc_scr[...] = alpha * acc_scr[...] + pv

    scores(0, s0_scr, mx0_scr)

    def trip(t, carry):
        c = 2 * t
        scores(c + 1, s1_scr, mx1_scr)
        absorb(c, s0_scr, mx0_scr)
        scores(c + 2, s0_scr, mx0_scr)
        absorb(c + 1, s1_scr, mx1_scr)
        return carry

    lax.fori_loop(0, (n_blocks + 1) // 2, trip, 0)

    lam = _diff_lambda(lam_ref[0], lam_init)
    acc = acc_scr[...]
    o = acc[:HEAD_W] / acc[HEAD_W:HEAD_W + 1]
    o = o[:, :tq] - lam * o[:, tq:]
    ms = jnp.mean(o * o, axis=0, keepdims=True)
    on = ((o * lax.rsqrt(ms + EPS)) * subg_ref[...]) * (1.0 - lam_init)
    o_ref[0] = on.T.astype(BF16)


def _attn_prompt(q, kb, vt, bias, lam_p, subg_col, j, lam_init):
    nbt, _, seq, _ = q.shape
    tq, tk = ATT_TQ, ATT_TK
    return pl.pallas_call(
        functools.partial(_attn_prompt_kernel, lam_init=lam_init),
        grid=(nbt, N_HEADS, seq // tq),
        in_specs=[
            pl.BlockSpec((1, 2, tq, HEAD_W), lambda b, h, i: (b, 0, i, h)),
            pl.BlockSpec((1, seq, HEAD_W), lambda b, h, i: (b, 0, h)),
            pl.BlockSpec((1, 1, VT_ROWS, seq), lambda b, h, i: (b, h, 0, 0)),
            pl.BlockSpec((1, 4, tk, tq), lambda b, h, i: (h, 0, 0, 0)),
            pl.BlockSpec((1, 4, HEAD_DIM), lambda b, h, i: (j, 0, 0)),
            pl.BlockSpec((HEAD_W, 1), lambda b, h, i: (0, 0)),
        ],
        out_specs=pl.BlockSpec((1, tq, HEAD_W), lambda b, h, i: (b, i, h)),
        out_shape=jax.ShapeDtypeStruct((nbt, seq, D_MODEL), BF16),
        scratch_shapes=[pltpu.VMEM((HEAD_W, 2 * tq), BF16),
                        pltpu.VMEM((tk, 2 * tq), F32), pltpu.VMEM((tk, 2 * tq), F32),
                        pltpu.VMEM((SUBLANES, 2 * tq), F32), pltpu.VMEM((SUBLANES, 2 * tq), F32),
                        pltpu.VMEM((1, 2 * tq), F32),
                        pltpu.VMEM((VT_ROWS, 2 * tq), F32)],
        compiler_params=pltpu.CompilerParams(
            dimension_semantics=("arbitrary", "arbitrary", "arbitrary"),
            vmem_limit_bytes=VMEM_LIMIT_BYTES),
        name="attn_prompt",
    )(q, kb, vt, bias, lam_p, subg_col)


def _attn_sample_kernel(q_ref, ck_ref, cv_ref, kn_ref, vn_ref, bc_ref, bn_ref, lam_ref, subg_ref,
                        o_ref, *, lam_init):
    nq = q_ref.shape[2]
    lam = _diff_lambda(lam_ref[0], lam_init)
    for h in range(N_HEADS):
        cols = slice(h * HEAD_W, (h + 1) * HEAD_W)
        qq = q_ref[0, :, :, cols].reshape(2 * nq, HEAD_W)
        kc = ck_ref[0, :, cols].astype(BF16)
        vc = cv_ref[0, :, cols].astype(BF16)
        kn = kn_ref[0, :, cols]
        vn = vn_ref[0, :, cols]
        past = kc.shape[0]
        s_c = (_dot_nt(qq, kc).reshape(2, nq, past) + bc_ref[h][None]).reshape(2 * nq, past)
        s_n = (_dot_nt(qq, kn).reshape(2, nq, nq) + bn_ref[h][None]).reshape(2 * nq, nq)
        m = jnp.maximum(jnp.max(s_c, axis=-1, keepdims=True), jnp.max(s_n, axis=-1, keepdims=True))
        p_c = jnp.exp2(s_c - m)
        p_n = jnp.exp2(s_n - m)
        l = jnp.sum(p_c, axis=-1, keepdims=True) + jnp.sum(p_n, axis=-1, keepdims=True)
        acc = _dot(p_c.astype(BF16), vc) + _dot(p_n.astype(BF16), vn)
        o_ref[0, :, cols] = _attn_finish(acc, l, nq, lam, subg_ref[...], lam_init).astype(BF16)


def _attn_sample(q, ck, cv, kb, vb, bias_c, bias_n, lam_p, subg, j, lam_init):
    nbt, _, nq, _ = q.shape
    past = ck.shape[1]
    return pl.pallas_call(
        functools.partial(_attn_sample_kernel, lam_init=lam_init),
        grid=(nbt,),
        in_specs=[
            pl.BlockSpec((1, 2, nq, D_MODEL), lambda b: (b, 0, 0, 0)),
            pl.BlockSpec((1, past, D_MODEL), lambda b: (b, 0, 0)),
            pl.BlockSpec((1, past, D_MODEL), lambda b: (b, 0, 0)),
            pl.BlockSpec((1, nq, D_MODEL), lambda b: (b, 0, 0)),
            pl.BlockSpec((1, nq, D_MODEL), lambda b: (b, 0, 0)),
            pl.BlockSpec((N_HEADS, nq, past), lambda b: (0, 0, 0)),
            pl.BlockSpec((N_HEADS, nq, nq), lambda b: (0, 0, 0)),
            pl.BlockSpec((1, 4, HEAD_DIM), lambda b: (j, 0, 0)),
            pl.BlockSpec((1, HEAD_W), lambda b: (0, 0)),
        ],
        out_specs=pl.BlockSpec((1, nq, D_MODEL), lambda b: (b, 0, 0)),
        out_shape=jax.ShapeDtypeStruct((nbt, nq, D_MODEL), BF16),
        compiler_params=pltpu.CompilerParams(
            dimension_semantics=("arbitrary",), vmem_limit_bytes=VMEM_LIMIT_BYTES),
        name="attn_sample",
    )(q, ck, cv, kb, vb, bias_c, bias_n, lam_p, subg)


def _proj_res_kernel(x_ref, a_ref, gt_ref, w_ref, o_ref):
    x = x_ref[...]
    nb, r, _ = x.shape
    a = a_ref[...].reshape(nb * r, a_ref.shape[-1])
    y = _dot(a, w_ref[0])
    o_ref[...] = x + gt_ref[0, 0] * y.reshape(nb, r, D_MODEL)


def _proj_res(x, a, rows, mods, layer, j, w):
    return pl.pallas_call(
        _proj_res_kernel,
        grid=rows.grid,
        in_specs=[
            rows.act(D_MODEL), rows.act(a.shape[-1]), rows.mod(layer, 5),
            _const_spec((1,) + w.shape[1:], (j, 0, 0)),
        ],
        out_specs=rows.act(D_MODEL),
        out_shape=jax.ShapeDtypeStruct(x.shape, F32),
        compiler_params=_cparams(),
        name="proj_res",
    )(x, a, mods, w)


def kernel(x_prompt, x_sample, c_prompt, c_sample, state_lru_h, state_lru_conv, state_pool, cache_k, cache_v, ada_w, ada_b, norm_g, ffn_w_in, ffn_w_out, lru_w_in, lru_conv_w, lru_conv_b, lru_ga_w, lru_ga_b, lru_gx_w, lru_gx_b, lru_lambda, lru_w_out, pool_w, pool_b, pool_scale, attn_w_in, attn_q_g, attn_k_g, attn_lambda, attn_sub_g, attn_w_out, rel_bias):
    B, S, _ = x_prompt.shape
    SB, T, _ = x_sample.shape
    P = cache_k.shape[2]
    assert SB == ADA_PROMPT_ROW0 and ADA_PROMPT_ROW0 + B <= ADA_ROWS
    n_a, n_b, n_c = lru_w_in.shape[0], pool_w.shape[0], attn_w_in.shape[0]

    c_all = jnp.concatenate(
        [c_sample, c_prompt, jnp.zeros((ADA_ROWS - SB - B, D_MODEL), F32)], axis=0)
    mods = _adaln(c_all, ada_w, ada_b)
    norm_g4 = norm_g.reshape(DEPTH, 3, 1, D_MODEL)

    ffn_in_b = ffn_w_in.astype(BF16)
    ffn_out_b = ffn_w_out.astype(BF16)
    lru_in_b = lru_w_in.astype(BF16)
    lru_out_b = lru_w_out.astype(BF16)
    per_super = LRU_BLOCKS // LRU_SUPER
    eye = jnp.eye(per_super, dtype=F32)

    def super_blocks(w):
        w5 = w.reshape(w.shape[0], LRU_SUPER, per_super, LRU_BS, LRU_BS)
        full = jnp.einsum('nshij,hg->nshigj', w5, eye)
        return full.reshape(w.shape[0], LRU_SUPER, per_super * LRU_BS, per_super * LRU_BS)

    lru_gate_b = jnp.concatenate([super_blocks(lru_ga_w), super_blocks(lru_gx_w)], axis=-1).astype(BF16)
    vec3 = lambda v: v.reshape(v.shape[0], 1, v.shape[-1])
    pool_w_b = pool_w.astype(BF16)
    attn_in_b = attn_w_in.astype(BF16)
    attn_out_b = attn_w_out.astype(BF16)
    seg = jnp.kron(jnp.eye(D_MODEL // HEAD_DIM, dtype=F32),
                   jnp.full((HEAD_DIM, HEAD_DIM), 1.0 / HEAD_DIM, F32)).astype(BF16)

    rows_p = {r: _Rows(B, S, r, True) for r in {FFN_ROWS, LRU_ROWS, POOL_ROWS, QKV_ROWS, OUT_ROWS}}
    rows_s = _Rows(SB, T, T, False)

    xp, xs = x_prompt, x_sample
    h_p, h_s, cv_p, cv_s, pl_p, pl_s, k_p, v_p, k_s, v_s = ([] for _ in range(10))
    for l in range(DEPTH):
        kind, j = l % 3, l // 3
        xp = _ffn(xp, rows_p[FFN_ROWS], mods, norm_g4, ffn_in_b, ffn_out_b, l, 0)
        xs = _ffn(xs, rows_s, mods, norm_g4, ffn_in_b, ffn_out_b, l, 0)
        if kind == 0:
            prm = (lru_in_b, lru_conv_w, vec3(lru_conv_b), lru_gate_b, vec3(lru_ga_b),
                   vec3(lru_gx_b), vec3(lru_lambda), lru_out_b)
            xp, hp, bp = _lru(xp, rows_p[LRU_ROWS], mods, norm_g4, l, j, *prm,
                              jnp.zeros((B, SUBLANES, D_RNN), F32), jnp.zeros((B, 1, D_RNN), F32))
            conv0 = jnp.pad(state_lru_conv[j], ((0, 0), (SUBLANES - (CONV_W - 1), 0), (0, 0)))
            xs, hs, bs = _lru(xs, rows_s, mods, norm_g4, l, j, *prm, conv0,
                              state_lru_h[j][:, None, :])
            h_p.append(hp), h_s.append(hs), cv_p.append(bp), cv_s.append(bs)
        elif kind == 1:
            prm = (pool_w_b, vec3(pool_b), vec3(pool_scale))
            xp, sp = _pool(xp, rows_p[POOL_ROWS], mods, norm_g4, l, j,
                           jnp.zeros((B, POOL_PAD, D_MODEL), F32), *prm, 0)
            hist0 = jnp.pad(state_pool[j], ((0, 0), (POOL_PAD - POOL_HIST, 0), (0, 0)))
            xs, ss = _pool(xs, rows_s, mods, norm_g4, l, j, hist0, *prm, POOL_HIST)
            pl_p.append(sp), pl_s.append(ss)
        else:
            lam_init = 0.8 - 0.6 * math.exp(-0.3 * l)
            qg = jnp.tile(attn_q_g[j], D_MODEL // HEAD_DIM)[None]
            kg = jnp.tile(attn_k_g[j], D_MODEL // HEAD_DIM)[None]
            subg = attn_sub_g[j][None]
            q, kf, kb, vf, vt = _qkv(xp, rows_p[QKV_ROWS], mods, norm_g4, l, j, attn_in_b, qg, kg, seg,
                                     True)
            bias_p = _bias_tile(rel_bias, 4 * ATT_TK, ATT_TQ, -2 * ATT_TK, 2 * ATT_TK,
                                keys_on_rows=True, q_period=ATT_TQ, row_blocks=16)
            bias_p = bias_p.reshape(N_HEADS, 4, ATT_TK, ATT_TQ)
            o = _attn_prompt(q, kb, vt, bias_p, attn_lambda, attn_sub_g[j][:, None], j, lam_init)
            xp = _proj_res(xp, o, rows_p[OUT_ROWS], mods, l, j, attn_out_b)
            k_p.append(kf.reshape(B, S, N_HEADS, HEAD_W)), v_p.append(vf.reshape(B, S, N_HEADS, HEAD_W))
            q, kf, kb, vf, vb = _qkv(xs, rows_s, mods, norm_g4, l, j, attn_in_b, qg, kg, seg, False)
            bias_c = _bias_tile(rel_bias, T, P, -P, None)
            bias_n = _bias_tile(rel_bias, T, T, 0, None)
            o = _attn_sample(q, cache_k[j].reshape(SB, P, D_MODEL), cache_v[j].reshape(SB, P, D_MODEL),
                             kb, vb, bias_c, bias_n, attn_lambda, subg, j, lam_init)
            xs = _proj_res(xs, o, rows_s, mods, l, j, attn_out_b)
            k_s.append(kf.reshape(SB, T, N_HEADS, HEAD_W)), v_s.append(vf.reshape(SB, T, N_HEADS, HEAD_W))
        xp = _ffn(xp, rows_p[FFN_ROWS], mods, norm_g4, ffn_in_b, ffn_out_b, l, 1)
        xs = _ffn(xs, rows_s, mods, norm_g4, ffn_in_b, ffn_out_b, l, 1)
    return (xp, xs, jnp.stack(h_p), jnp.stack(h_s), jnp.stack(cv_p), jnp.stack(cv_s),
            jnp.stack(pl_p), jnp.stack(pl_s), jnp.stack(k_p), jnp.stack(v_p),
            jnp.stack(k_s), jnp.stack(v_s))
```

```python
import functools
import math

import jax
import jax.numpy as jnp
import numpy as np
from jax import lax
from jax.experimental import pallas as pl
from jax.experimental.pallas import tpu as pltpu

F32 = jnp.float32
BF16 = jnp.bfloat16

D_MODEL = 1024
DEPTH = 4
D_FF = 2816
D_RNN = 1280
LRU_BLOCKS = 16
LRU_BS = D_RNN // LRU_BLOCKS
LRU_SUPER = 2
CONV_W = 4
LRU_C = 8.0
POOL_WINDOWS = (2, 4, 8, 16)
POOL_GW = D_MODEL // len(POOL_WINDOWS)
POOL_HIST = 15
N_HEADS = 8
HEAD_DIM = D_MODEL // (2 * N_HEADS)
HEAD_W = 2 * HEAD_DIM
NUM_BUCKETS = 32
MAX_DISTANCE = 128
CHUNK = 64
EPS = 1e-6
NEG_INF = -1e30

ADA_ROWS = 16
ADA_PROMPT_ROW0 = 8

VMEM_LIMIT_BYTES = 60 * 1024 * 1024
SUBLANES = 8
LANES = 128

FFN_ROWS = 1024
FFN_CHUNK = 256
LRU_ROWS = 256
LRU_PERM_ROWS = 256
POOL_ROWS = 512
QKV_ROWS = 512
ATT_TQ = 1024
ATT_TK = 512
BIAS_TILE_ROWS = 128
ATT_ROW_CHUNK = 256
VT_ROWS = HEAD_W + 16
LOG2E = math.log2(math.e)
OUT_ROWS = 512


def _cparams():
    return pltpu.CompilerParams(
        dimension_semantics=("arbitrary", "arbitrary"),
        vmem_limit_bytes=VMEM_LIMIT_BYTES,
    )


def _sigmoid(x):
    return 0.5 * jnp.tanh(0.5 * x) + 0.5


def _modulate(x, g, shift, scale):
    ms = jnp.mean(x * x, axis=-1, keepdims=True)
    y = x * lax.rsqrt(ms + EPS)
    return (y * g) * (1.0 + scale) + shift


def _dot(a, b):
    return jnp.dot(a, b, preferred_element_type=F32)


def _dot_nt(a, b):
    return lax.dot_general(a, b, (((1,), (1,)), ((), ())), preferred_element_type=F32)


def _adaln_kernel(c_ref, w_ref, b_ref, o_ref):
    c = c_ref[...]
    s = c * _sigmoid(c)
    y = _dot(s.astype(BF16), w_ref[0].astype(BF16))
    o_ref[0, 0] = y + b_ref[0, 0]


def _adaln(c_all, ada_w, ada_b):
    n_vec = 9
    out = pl.pallas_call(
        _adaln_kernel,
        grid=(DEPTH, n_vec),
        in_specs=[
            pl.BlockSpec((ADA_ROWS, D_MODEL), lambda l, k: (0, 0)),
            pl.BlockSpec((1, D_MODEL, D_MODEL), lambda l, k: (l, 0, k)),
            pl.BlockSpec((1, 1, 1, D_MODEL), lambda l, k: (l, k, 0, 0)),
        ],
        out_specs=pl.BlockSpec((1, 1, ADA_ROWS, D_MODEL), lambda l, k: (l, k, 0, 0)),
        out_shape=jax.ShapeDtypeStruct((DEPTH, n_vec, ADA_ROWS, D_MODEL), F32),
        compiler_params=_cparams(),
        name="adaln",
    )(c_all, ada_w, ada_b.reshape(DEPTH, n_vec, 1, D_MODEL))
    return out.reshape(DEPTH, n_vec, ADA_ROWS, 1, D_MODEL)


class _Rows:
    def __init__(self, batch, seq, rows, prompt, seqs_per_block=1):
        self.prompt = prompt
        if prompt:
            assert seq % rows == 0 and batch % seqs_per_block == 0
            assert ADA_PROMPT_ROW0 % seqs_per_block == 0
            self.nb, self.r = seqs_per_block, rows
            self.grid = (batch // seqs_per_block, seq // rows)
        else:
            self.nb, self.r = batch, seq
            self.grid = (1, 1)

    def act(self, width):
        return pl.BlockSpec((self.nb, self.r, width), lambda b, t: (b, t, 0))

    def per_seq(self, rows, width):
        return pl.BlockSpec((self.nb, rows, width), lambda b, t: (b, 0, 0))

    def mod(self, layer, k):
        if self.prompt:
            first = ADA_PROMPT_ROW0 // self.nb
            return pl.BlockSpec((1, 1, self.nb, 1, D_MODEL), lambda b, t: (layer, k, first + b, 0, 0))
        return pl.BlockSpec((1, 1, self.nb, 1, D_MODEL), lambda b, t: (layer, k, 0, 0, 0))


def _const_spec(shape, index, single_buffer=False):
    if single_buffer:
        return pl.BlockSpec(shape, lambda b, t: index, pipeline_mode=pl.Buffered(1))
    return pl.BlockSpec(shape, lambda b, t: index)


def _ffn_kernel(x_ref, sh_ref, sc_ref, gt_ref, g_ref, win_ref, wout_ref, o_ref):
    x = x_ref[...]
    nb, r, _ = x.shape
    u = _modulate(x, g_ref[0, 0], sh_ref[0, 0], sc_ref[0, 0])
    ub = u.reshape(nb * r, D_MODEL).astype(BF16)
    acc = None
    for c in range(D_FF // FFN_CHUNK):
        lo = c * FFN_CHUNK
        a = _dot(ub, win_ref[0, 0, :, lo:lo + FFN_CHUNK])
        b = _dot(ub, win_ref[0, 0, :, D_FF + lo:D_FF + lo + FFN_CHUNK])
        h = ((a * _sigmoid(a)) * b).astype(BF16)
        y = _dot(h, wout_ref[0, 0, lo:lo + FFN_CHUNK, :])
        acc = y if acc is None else acc + y
    o_ref[...] = x + (0.5 * gt_ref[0, 0]) * acc.reshape(nb, r, D_MODEL)


def _ffn(x, rows, mods, norm_g4, w_in, w_out, layer, which):
    k0 = 0 if which == 0 else 6
    g_idx = 0 if which == 0 else 2
    return pl.pallas_call(
        _ffn_kernel,
        grid=rows.grid,
        in_specs=[
            rows.act(D_MODEL),
            rows.mod(layer, k0), rows.mod(layer, k0 + 1), rows.mod(layer, k0 + 2),
            _const_spec((1, 1, 1, D_MODEL), (layer, g_idx, 0, 0)),
            _const_spec((1, 1, D_MODEL, 2 * D_FF), (layer, which, 0, 0), True),
            _const_spec((1, 1, D_FF, D_MODEL), (layer, which, 0, 0), True),
        ],
        out_specs=rows.act(D_MODEL),
        out_shape=jax.ShapeDtypeStruct(x.shape, F32),
        compiler_params=_cparams(),
        name="ffn",
    )(x, mods, mods, mods, norm_g4, w_in, w_out)


def _group_scan(a, b):
    row = lax.broadcasted_iota(jnp.int32, a.shape, 1)
    d = 1
    while d < SUBLANES:
        keep = row >= d
        a_sh = jnp.where(keep, pltpu.roll(a, d, axis=1), 1.0)
        b_sh = jnp.where(keep, pltpu.roll(b, d, axis=1), 0.0)
        b = a * b_sh + b
        a = a * a_sh
        d *= 2
    return a, b


def _linear_scan(a, b, h_prev, a_ref, h_ref):
    nb, r, C = a.shape
    first = slice(0, SUBLANES)
    a_run, h_run = a[:, first], b[:, first]
    a_ref[:, first, :] = a_run
    h_ref[:, first, :] = h_run
    for g0 in range(SUBLANES, r, SUBLANES):
        rows = slice(g0, g0 + SUBLANES)
        h_run = a[:, rows] * h_run + b[:, rows]
        a_run = a[:, rows] * a_run
        a_ref[:, rows, :] = a_run
        h_ref[:, rows, :] = h_run
    a_inc, h_inc = _group_scan(a_run, h_run)
    seg_end = a_inc * h_prev + h_inc
    row = lax.broadcasted_iota(jnp.int32, seg_end.shape, 1)
    seg_in = jnp.where(row == 0, h_prev, pltpu.roll(seg_end, 1, axis=1))
    for g0 in range(0, r, SUBLANES):
        rows = slice(g0, g0 + SUBLANES)
        h_ref[:, rows, :] = h_ref[:, rows, :] + a_ref[:, rows, :] * seg_in
    return seg_end[:, SUBLANES - 1:, :]


LRU_TAIL = (CONV_W - 1) * SUBLANES


def _lru_kernel(x_ref, sh_ref, sc_ref, gt_ref, g_ref, win_ref, cw_ref, cb_ref, gw_ref, gab_ref,
                gxb_ref, lam_ref, wout_ref, conv0_ref, h0_ref, perm_ref, unperm_ref,
                o_ref, hlast_ref, tail_ref, h_scr, tail_scr, arun_scr, hall_scr):
    @pl.when(pl.program_id(1) == 0)
    def _():
        h_scr[...] = h0_ref[...]
        tail_scr[...] = conv0_ref[...]

    x = x_ref[...]
    nb, r, _ = x.shape
    assert r // SUBLANES >= CONV_W - 1
    C = D_RNN
    u = _modulate(x, g_ref[0, 0], sh_ref[0, 0], sc_ref[0, 0])
    n_perm = perm_ref.shape[0]

    def permute(p_ref, v):
        parts = [_dot(p_ref[...], v[lo:lo + n_perm]) for lo in range(0, nb * r, n_perm)]
        return jnp.concatenate(parts, axis=0).astype(BF16)

    ub = permute(perm_ref, u.reshape(nb * r, D_MODEL).astype(BF16))
    gate_br = _dot(ub, win_ref[0, :, :C])
    x_br = _dot(ub, win_ref[0, :, C:]).reshape(nb, r, C)

    last = x_br[:, r - LRU_TAIL:, :]
    groups = (nb * (CONV_W - 1), SUBLANES, C)
    row = lax.broadcasted_iota(jnp.int32, groups, 1)
    wrap = jnp.where(row == 0, pltpu.roll(tail_scr[...].reshape(groups), 1, axis=1),
                     pltpu.roll(last.reshape(groups), 1, axis=1)).reshape(nb, LRU_TAIL, C)
    tail_scr[...] = last
    tail_ref[...] = last
    cw = cw_ref[0]
    xc = cb_ref[0] + cw[CONV_W - 1:CONV_W] * x_br
    for d in range(1, CONV_W):
        shifted = jnp.concatenate([wrap[:, LRU_TAIL - d * SUBLANES:], x_br[:, :r - d * SUBLANES]], axis=1)
        xc = xc + cw[CONV_W - 1 - d:CONV_W - d] * shifted

    xc2 = xc.reshape(nb * r, C)
    xcb = xc2.astype(BF16)
    sw = C // LRU_SUPER
    gates = [_dot(xcb[:, s * sw:(s + 1) * sw], gw_ref[0, s]) for s in range(LRU_SUPER)]
    rg = _sigmoid(jnp.concatenate([g[:, :sw] for g in gates], axis=-1) + gab_ref[0])
    ig = _sigmoid(jnp.concatenate([g[:, sw:] for g in gates], axis=-1) + gxb_ref[0])
    nl = -lam_ref[0]
    softplus = jnp.maximum(nl, 0.0) + jnp.log1p(jnp.exp(-jnp.abs(nl)))
    log_a = (-LRU_C * rg) * softplus
    a = jnp.exp(log_a)
    th = jnp.tanh(log_a)
    one_minus_a2 = (-2.0 * th) / (1.0 - th)
    root = jnp.where(one_minus_a2 > 0.0, one_minus_a2 * lax.rsqrt(one_minus_a2), 0.0)
    b_in = root * (ig * xc2)

    h_last = _linear_scan(a.reshape(nb, r, C), b_in.reshape(nb, r, C), h_scr[...], arun_scr, hall_scr)
    h_scr[...] = h_last
    hlast_ref[...] = h_last

    gb = gate_br
    gelu = 0.5 * gb * (1.0 + jnp.tanh(math.sqrt(2.0 / math.pi) * (gb + 0.044715 * (gb * gb * gb))))
    z = (gelu * hall_scr[...].reshape(nb * r, C)).astype(BF16)
    y = _dot(permute(unperm_ref, z), wout_ref[0])
    o_ref[...] = x + gt_ref[0, 0] * y.reshape(nb, r, D_MODEL)


def _lru(x, rows, mods, norm_g4, layer, j, w_in, conv_w, conv_b, gate_w, ga_b, gx_b, lam, w_out,
         conv0, h0):
    nbt = x.shape[0]
    C = D_RNN
    seqs = max(1, LRU_PERM_ROWS // rows.r)
    assert rows.nb % seqs == 0
    n_rows, G = seqs * rows.r, rows.r // SUBLANES
    b_idx, g_idx, i_idx = np.meshgrid(np.arange(seqs), np.arange(G), np.arange(SUBLANES), indexing='ij')
    source = (b_idx * rows.r + i_idx * G + g_idx).reshape(n_rows)
    perm = np.zeros((n_rows, n_rows), np.float32)
    perm[np.arange(n_rows), source] = 1.0
    vec = lambda: _const_spec((1, 1, C), (j, 0, 0))
    out, h_last, tail = pl.pallas_call(
        _lru_kernel,
        grid=rows.grid,
        in_specs=[
            rows.act(D_MODEL),
            rows.mod(layer, 3), rows.mod(layer, 4), rows.mod(layer, 5),
            _const_spec((1, 1, 1, D_MODEL), (layer, 1, 0, 0)),
            _const_spec((1, D_MODEL, 2 * C), (j, 0, 0)),
            _const_spec((1, CONV_W, C), (j, 0, 0)),
            vec(),
            _const_spec((1, LRU_SUPER, C // LRU_SUPER, 2 * C // LRU_SUPER), (j, 0, 0, 0)),
            vec(), vec(), vec(),
            _const_spec((1, C, D_MODEL), (j, 0, 0)),
            rows.per_seq(LRU_TAIL, C),
            rows.per_seq(1, C),
            _const_spec((n_rows, n_rows), (0, 0)),
            _const_spec((n_rows, n_rows), (0, 0)),
        ],
        out_specs=[rows.act(D_MODEL), rows.per_seq(1, C), rows.per_seq(LRU_TAIL, C)],
        out_shape=[
            jax.ShapeDtypeStruct(x.shape, F32),
            jax.ShapeDtypeStruct((nbt, 1, C), F32),
            jax.ShapeDtypeStruct((nbt, LRU_TAIL, C), F32),
        ],
        scratch_shapes=[pltpu.VMEM((rows.nb, 1, C), F32), pltpu.VMEM((rows.nb, LRU_TAIL, C), F32),
                        pltpu.VMEM((rows.nb, rows.r, C), F32), pltpu.VMEM((rows.nb, rows.r, C), F32)],
        compiler_params=_cparams(),
        name="lru",
    )(x, mods, mods, mods, norm_g4, w_in, conv_w, conv_b, gate_w, ga_b, gx_b, lam, w_out, conv0, h0,
      jnp.asarray(perm, BF16), jnp.asarray(perm.T, BF16))
    return out, h_last[:, 0], tail[:, SUBLANES - 1::SUBLANES]


POOL_PAD = 16


def _pool_kernel(x_ref, sh_ref, sc_ref, gt_ref, g_ref, hist0_ref, w_ref, b_ref, ps_ref,
                 o_ref, st_ref, hist_scr, *, t_base):
    t = pl.program_id(1)

    @pl.when(t == 0)
    def _():
        hist_scr[...] = hist0_ref[...]

    x = x_ref[...]
    nb, r, _ = x.shape
    u = _modulate(x, g_ref[0, 0], sh_ref[0, 0], sc_ref[0, 0])
    ext = jnp.concatenate([hist_scr[...], u], axis=1)
    new_hist = u[:, r - POOL_PAD:, :]
    hist_scr[...] = new_hist
    st_ref[...] = new_hist

    row = lax.broadcasted_iota(jnp.int32, (1, r, 1), 1)
    t_glob = t_base + t * r + row
    gate = gt_ref[0, 0]
    s = ext
    for g, wnd in enumerate(POOL_WINDOWS):
        lo = g * POOL_GW
        s = s[:, :, (POOL_GW if g > 0 else 0):]
        s = s + pltpu.roll(s, wnd // 2, axis=1)
        cnt = jnp.minimum(t_glob + 1, wnd).astype(F32)
        mean = s[:, POOL_PAD:, :POOL_GW] / cnt
        d = mean - u[:, :, lo:lo + POOL_GW]
        y = _dot(d.reshape(nb * r, POOL_GW).astype(BF16), w_ref[0, g]).reshape(nb, r, POOL_GW)
        y = (y + b_ref[0, :, lo:lo + POOL_GW]) * ps_ref[0, :, lo:lo + POOL_GW]
        o_ref[:, :, lo:lo + POOL_GW] = x[:, :, lo:lo + POOL_GW] + gate[:, :, lo:lo + POOL_GW] * y


def _pool(x, rows, mods, norm_g4, layer, j, hist0, w, b, scale, t_base):
    nbt = x.shape[0]
    out, st = pl.pallas_call(
        functools.partial(_pool_kernel, t_base=t_base),
        grid=rows.grid,
        in_specs=[
            rows.act(D_MODEL),
            rows.mod(layer, 3), rows.mod(layer, 4), rows.mod(layer, 5),
            _const_spec((1, 1, 1, D_MODEL), (layer, 1, 0, 0)),
            rows.per_seq(POOL_PAD, D_MODEL),
            _const_spec((1, len(POOL_WINDOWS), POOL_GW, POOL_GW), (j, 0, 0, 0)),
            _const_spec((1, 1, D_MODEL), (j, 0, 0)),
            _const_spec((1, 1, D_MODEL), (j, 0, 0)),
        ],
        out_specs=[rows.act(D_MODEL), rows.per_seq(POOL_PAD, D_MODEL)],
        out_shape=[jax.ShapeDtypeStruct(x.shape, F32),
                   jax.ShapeDtypeStruct((nbt, POOL_PAD, D_MODEL), F32)],
        scratch_shapes=[pltpu.VMEM((rows.nb, POOL_PAD, D_MODEL), F32)],
        compiler_params=_cparams(),
        name="pool",
    )(x, mods, mods, mods, norm_g4, hist0, w, b, scale)
    return out, st[:, POOL_PAD - POOL_HIST:]


def _qkv_kernel(x_ref, sh_ref, sc_ref, g_ref, w_ref, qg_ref, kg_ref, seg_ref,
                q_ref, kf_ref, kb_ref, vf_ref, vb_ref, *, v_transposed):
    x = x_ref[...]
    nb, r, _ = x.shape
    u = _modulate(x, g_ref[0, 0], sh_ref[0, 0], sc_ref[0, 0])
    ub = u.reshape(nb * r, D_MODEL).astype(BF16)
    q = _dot(ub, w_ref[0, :, :D_MODEL])
    k = _dot(ub, w_ref[0, :, D_MODEL:2 * D_MODEL])
    v = _dot(ub, w_ref[0, :, 2 * D_MODEL:])

    def head_norm(z, gain):
        ms = _dot((z * z).astype(BF16), seg_ref[...])
        return (z * lax.rsqrt(ms + EPS)) * gain

    qn = head_norm(q, qg_ref[...]) * (HEAD_DIM ** -0.5 * LOG2E)
    kn = head_norm(k, kg_ref[...])
    lane = lax.broadcasted_iota(jnp.int32, (1, D_MODEL), 1)
    first = (lane % HEAD_W) < HEAD_DIM
    q_ref[:, 0] = jnp.where(first, qn, 0.0).astype(BF16).reshape(nb, r, D_MODEL)
    q_ref[:, 1] = jnp.where(first, 0.0, qn).astype(BF16).reshape(nb, r, D_MODEL)
    kf_ref[...] = kn.reshape(nb, r, D_MODEL)
    kb_ref[...] = kn.astype(BF16).reshape(nb, r, D_MODEL)
    vf_ref[...] = v.reshape(nb, r, D_MODEL)
    if v_transposed:
        for h in range(N_HEADS):
            vb_ref[0, h, :HEAD_W] = v[:, h * HEAD_W:(h + 1) * HEAD_W].T.astype(BF16)
            vb_ref[0, h, HEAD_W:] = jnp.ones((VT_ROWS - HEAD_W, r), BF16)
    else:
        vb_ref[...] = v.astype(BF16).reshape(nb, r, D_MODEL)


def _qkv(x, rows, mods, norm_g4, layer, j, w_in, qg, kg, seg, v_transposed):
    nbt, seq, _ = x.shape
    q_spec = pl.BlockSpec((rows.nb, 2, rows.r, D_MODEL), lambda b, t: (b, 0, t, 0))
    if v_transposed:
        assert rows.nb == 1
        vb_spec = pl.BlockSpec((1, N_HEADS, VT_ROWS, rows.r), lambda b, t: (b, 0, 0, t))
        vb_shape = jax.ShapeDtypeStruct((nbt, N_HEADS, VT_ROWS, seq), BF16)
    else:
        vb_spec = rows.act(D_MODEL)
        vb_shape = jax.ShapeDtypeStruct(x.shape, BF16)
    return pl.pallas_call(
        functools.partial(_qkv_kernel, v_transposed=v_transposed),
        grid=rows.grid,
        in_specs=[
            rows.act(D_MODEL),
            rows.mod(layer, 3), rows.mod(layer, 4),
            _const_spec((1, 1, 1, D_MODEL), (layer, 1, 0, 0)),
            _const_spec((1, D_MODEL, 3 * D_MODEL), (j, 0, 0)),
            _const_spec((1, D_MODEL), (0, 0)),
            _const_spec((1, D_MODEL), (0, 0)),
            _const_spec((D_MODEL, D_MODEL), (0, 0)),
        ],
        out_specs=[q_spec, rows.act(D_MODEL), rows.act(D_MODEL), rows.act(D_MODEL), vb_spec],
        out_shape=[
            jax.ShapeDtypeStruct((nbt, 2, seq, D_MODEL), BF16),
            jax.ShapeDtypeStruct(x.shape, F32),
            jax.ShapeDtypeStruct(x.shape, BF16),
            jax.ShapeDtypeStruct(x.shape, F32),
            vb_shape,
        ],
        compiler_params=_cparams(),
        name="qkv",
    )(x, mods, mods, norm_g4, w_in, qg, kg, seg)


_BUCKET_EDGES = (12, 16, 23, 32, 46, 64, 91)
_FAR_BUCKET = NUM_BUCKETS // 2 - 1


def _bucket_of(rel):
    n = abs(rel)
    small = NUM_BUCKETS // 4
    b = n if n < small else small + sum(n >= e for e in _BUCKET_EDGES)
    return b + (NUM_BUCKETS // 2 if rel > 0 else 0)


_BUCKET_RUNS = tuple((r, _bucket_of(r)) for r in range(-_BUCKET_EDGES[-1] + 1, _BUCKET_EDGES[-1] + 1)
                     if _bucket_of(r) != _bucket_of(r - 1))


def _bias_kernel(rb_ref, o_ref, *, rel0, mask_from, keys_on_rows, tile, tiles):
    h = pl.program_id(0)
    R, W = tile
    for row0, col0, fill in tiles:
        window = (0, slice(row0, row0 + R), slice(col0, col0 + W))
        if fill is not None:
            o_ref[window] = jnp.full((R, W), fill, F32)
            continue
        i = lax.broadcasted_iota(jnp.int32, (R, W), 0) + row0
        jj = lax.broadcasted_iota(jnp.int32, (R, W), 1) + col0
        key, qry = (i, jj) if keys_on_rows else (jj, i)
        rel = key - qry + rel0
        out = jnp.full((R, W), rb_ref[_FAR_BUCKET, h], F32)
        for first_rel, bucket in _BUCKET_RUNS:
            out = jnp.where(rel >= first_rel, rb_ref[bucket, h], out)
        out = (out - rb_ref[_FAR_BUCKET, h]) * LOG2E
        if mask_from is not None:
            visible = ((key - mask_from) // CHUNK <= qry // CHUNK) | (key < mask_from)
            out = jnp.where(visible, out, NEG_INF)
        o_ref[window] = out


def _bias_tile(rel_bias, rows, width, rel0, mask_from, keys_on_rows=False):
    R = BIAS_TILE_ROWS if rows % BIAS_TILE_ROWS == 0 else rows
    W = LANES if width % LANES == 0 else width
    tiles = []
    for row0 in range(0, rows, R):
        for col0 in range(0, width, W):
            key0, qry0 = (row0, col0) if keys_on_rows else (col0, row0)
            key1, qry1 = (row0 + R, col0 + W) if keys_on_rows else (col0 + W, row0 + R)
            fill = None
            unmasked = mask_from is None or key1 <= mask_from
            if unmasked and (key1 - 1) - qry0 + rel0 <= -_BUCKET_EDGES[-1]:
                fill = 0.0
            if mask_from is not None and key0 >= mask_from and \
                    (key0 - mask_from) // CHUNK > (qry1 - 1) // CHUNK:
                fill = NEG_INF
            tiles.append((row0, col0, fill))
    return pl.pallas_call(
        functools.partial(_bias_kernel, rel0=rel0, mask_from=mask_from, keys_on_rows=keys_on_rows,
                          tile=(R, W), tiles=tuple(tiles)),
        grid=(N_HEADS,),
        in_specs=[pl.BlockSpec(memory_space=pltpu.SMEM)],
        out_specs=pl.BlockSpec((1, rows, width), lambda h: (h, 0, 0)),
        out_shape=jax.ShapeDtypeStruct((N_HEADS, rows, width), F32),
        name="bias_tile",
    )(rel_bias)


def _diff_lambda(lp, lam_init):
    s1 = jnp.sum(lp[0:1] * lp[1:2], axis=-1, keepdims=True)
    s2 = jnp.sum(lp[2:3] * lp[3:4], axis=-1, keepdims=True)
    return jnp.exp(s1) - jnp.exp(s2) + lam_init


def _attn_finish(acc, l, nq, lam, subg, lam_init):
    o = acc / l
    o = o[:nq] - lam * o[nq:]
    ms = jnp.mean(o * o, axis=-1, keepdims=True)
    return ((o * lax.rsqrt(ms + EPS)) * subg) * (1.0 - lam_init)


def _attn_prompt_kernel(q_ref, k_ref, vt_ref, bias_ref, lam_ref, subg_ref, o_ref,
                        qt_scr, s0_scr, s1_scr, mx0_scr, mx1_scr, m_scr, acc_scr, *, lam_init):
    qi = pl.program_id(2)
    tq, tk = ATT_TQ, ATT_TK
    width = 2 * tq
    n_blocks = (qi + 1) * (tq // tk)
    qt_scr[:, :tq] = q_ref[0, 0].astype(F32).T.astype(BF16)
    qt_scr[:, tq:] = q_ref[0, 1].astype(F32).T.astype(BF16)
    m_scr[...] = jnp.full(m_scr.shape, NEG_INF, F32)
    acc_scr[...] = jnp.zeros(acc_scr.shape, F32)

    def block_start(c):
        return jnp.clip(c, 0, n_blocks - 1) * tk

    def scores(c, s_ref, mx_ref):
        sel = jnp.clip(c - (n_blocks - 4), 0, 3)
        bias = bias_ref[0, sel]
        start = pl.multiple_of(block_start(c), tk)
        s = _dot(k_ref[0, pl.ds(start, tk), :], qt_scr[...]) + jnp.concatenate([bias, bias], axis=1)
        s_ref[...] = s
        mx_ref[...] = jnp.max(s.reshape(tk // SUBLANES, SUBLANES, width), axis=0)

    def absorb(c, s_ref, mx_ref):
        m_old = m_scr[...]
        m_new = jnp.maximum(m_old, jnp.max(mx_ref[...], axis=0, keepdims=True))
        alpha = jnp.exp2(m_old - m_new)
        m_scr[...] = m_new
        pv = None
        for r0 in range(0, tk, ATT_ROW_CHUNK):
            p = jnp.exp2((s_ref[r0:r0 + ATT_ROW_CHUNK, :] - m_new).astype(BF16))
            start = pl.multiple_of(block_start(c) + r0, ATT_ROW_CHUNK)
            part = _dot(vt_ref[0, 0, :, pl.ds(start, ATT_ROW_CHUNK)], p)
            pv = part if pv is None else pv + part
        acc_scr[...] = alpha * acc_scr[...] + pv

    scores(0, s0_scr, mx0_scr)

    def trip(t, carry):
        c = 2 * t
        scores(c + 1, s1_scr, mx1_scr)
        absorb(c, s0_scr, mx0_scr)
        scores(c + 2, s0_scr, mx0_scr)
        absorb(c + 1, s1_scr, mx1_scr)
        return carry

    lax.fori_loop(0, n_blocks // 2, trip, 0)

    lam = _diff_lambda(lam_ref[0], lam_init)
    acc = acc_scr[...]
    o = acc[:HEAD_W] / acc[HEAD_W:HEAD_W + 1]
    o = o[:, :tq] - lam * o[:, tq:]
    ms = jnp.mean(o * o, axis=0, keepdims=True)
    on = ((o * lax.rsqrt(ms + EPS)) * subg_ref[...]) * (1.0 - lam_init)
    o_ref[0] = on.T.astype(BF16)


def _attn_prompt(q, kb, vt, bias, lam_p, subg_col, j, lam_init):
    nbt, _, seq, _ = q.shape
    tq, tk = ATT_TQ, ATT_TK
    return pl.pallas_call(
        functools.partial(_attn_prompt_kernel, lam_init=lam_init),
        grid=(nbt, N_HEADS, seq // tq),
        in_specs=[
            pl.BlockSpec((1, 2, tq, HEAD_W), lambda b, h, i: (b, 0, i, h)),
            pl.BlockSpec((1, seq, HEAD_W), lambda b, h, i: (b, 0, h)),
            pl.BlockSpec((1, 1, VT_ROWS, seq), lambda b, h, i: (b, h, 0, 0)),
            pl.BlockSpec((1, 4, tk, tq), lambda b, h, i: (h, 0, 0, 0)),
            pl.BlockSpec((1, 4, HEAD_DIM), lambda b, h, i: (j, 0, 0)),
            pl.BlockSpec((HEAD_W, 1), lambda b, h, i: (0, 0)),
        ],
        out_specs=pl.BlockSpec((1, tq, HEAD_W), lambda b, h, i: (b, i, h)),
        out_shape=jax.ShapeDtypeStruct((nbt, seq, D_MODEL), BF16),
        scratch_shapes=[pltpu.VMEM((HEAD_W, 2 * tq), BF16),
                        pltpu.VMEM((tk, 2 * tq), F32), pltpu.VMEM((tk, 2 * tq), F32),
                        pltpu.VMEM((SUBLANES, 2 * tq), F32), pltpu.VMEM((SUBLANES, 2 * tq), F32),
                        pltpu.VMEM((1, 2 * tq), F32),
                        pltpu.VMEM((VT_ROWS, 2 * tq), F32)],
        compiler_params=pltpu.CompilerParams(
            dimension_semantics=("arbitrary", "arbitrary", "arbitrary"),
            vmem_limit_bytes=VMEM_LIMIT_BYTES),
        name="attn_prompt",
    )(q, kb, vt, bias, lam_p, subg_col)


def _attn_sample_kernel(q_ref, ck_ref, cv_ref, kn_ref, vn_ref, bc_ref, bn_ref, lam_ref, subg_ref,
                        o_ref, *, lam_init):
    nq = q_ref.shape[2]
    lam = _diff_lambda(lam_ref[0], lam_init)
    for h in range(N_HEADS):
        cols = slice(h * HEAD_W, (h + 1) * HEAD_W)
        qq = q_ref[0, :, :, cols].reshape(2 * nq, HEAD_W)
        kc = ck_ref[0, :, cols].astype(BF16)
        vc = cv_ref[0, :, cols].astype(BF16)
        kn = kn_ref[0, :, cols]
        vn = vn_ref[0, :, cols]
        past = kc.shape[0]
        s_c = (_dot_nt(qq, kc).reshape(2, nq, past) + bc_ref[h][None]).reshape(2 * nq, past)
        s_n = (_dot_nt(qq, kn).reshape(2, nq, nq) + bn_ref[h][None]).reshape(2 * nq, nq)
        m = jnp.maximum(jnp.max(s_c, axis=-1, keepdims=True), jnp.max(s_n, axis=-1, keepdims=True))
        p_c = jnp.exp2(s_c - m)
        p_n = jnp.exp2(s_n - m)
        l = jnp.sum(p_c, axis=-1, keepdims=True) + jnp.sum(p_n, axis=-1, keepdims=True)
        acc = _dot(p_c.astype(BF16), vc) + _dot(p_n.astype(BF16), vn)
        o_ref[0, :, cols] = _attn_finish(acc, l, nq, lam, subg_ref[...], lam_init).astype(BF16)


def _attn_sample(q, ck, cv, kb, vb, bias_c, bias_n, lam_p, subg, j, lam_init):
    nbt, _, nq, _ = q.shape
    past = ck.shape[1]
    return pl.pallas_call(
        functools.partial(_attn_sample_kernel, lam_init=lam_init),
        grid=(nbt,),
        in_specs=[
            pl.BlockSpec((1, 2, nq, D_MODEL), lambda b: (b, 0, 0, 0)),
            pl.BlockSpec((1, past, D_MODEL), lambda b: (b, 0, 0)),
            pl.BlockSpec((1, past, D_MODEL), lambda b: (b, 0, 0)),
            pl.BlockSpec((1, nq, D_MODEL), lambda b: (b, 0, 0)),
            pl.BlockSpec((1, nq, D_MODEL), lambda b: (b, 0, 0)),
            pl.BlockSpec((N_HEADS, nq, past), lambda b: (0, 0, 0)),
            pl.BlockSpec((N_HEADS, nq, nq), lambda b: (0, 0, 0)),
            pl.BlockSpec((1, 4, HEAD_DIM), lambda b: (j, 0, 0)),
            pl.BlockSpec((1, HEAD_W), lambda b: (0, 0)),
        ],
        out_specs=pl.BlockSpec((1, nq, D_MODEL), lambda b: (b, 0, 0)),
        out_shape=jax.ShapeDtypeStruct((nbt, nq, D_MODEL), BF16),
        compiler_params=pltpu.CompilerParams(
            dimension_semantics=("arbitrary",), vmem_limit_bytes=VMEM_LIMIT_BYTES),
        name="attn_sample",
    )(q, ck, cv, kb, vb, bias_c, bias_n, lam_p, subg)


def _proj_res_kernel(x_ref, a_ref, gt_ref, w_ref, o_ref):
    x = x_ref[...]
    nb, r, _ = x.shape
    a = a_ref[...].reshape(nb * r, a_ref.shape[-1])
    y = _dot(a, w_ref[0])
    o_ref[...] = x + gt_ref[0, 0] * y.reshape(nb, r, D_MODEL)


def _proj_res(x, a, rows, mods, layer, j, w):
    return pl.pallas_call(
        _proj_res_kernel,
        grid=rows.grid,
        in_specs=[
            rows.act(D_MODEL), rows.act(a.shape[-1]), rows.mod(layer, 5),
            _const_spec((1,) + w.shape[1:], (j, 0, 0)),
        ],
        out_specs=rows.act(D_MODEL),
        out_shape=jax.ShapeDtypeStruct(x.shape, F32),
        compiler_params=_cparams(),
        name="proj_res",
    )(x, a, mods, w)


def kernel(x_prompt, x_sample, c_prompt, c_sample, state_lru_h, state_lru_conv, state_pool, cache_k, cache_v, ada_w, ada_b, norm_g, ffn_w_in, ffn_w_out, lru_w_in, lru_conv_w, lru_conv_b, lru_ga_w, lru_ga_b, lru_gx_w, lru_gx_b, lru_lambda, lru_w_out, pool_w, pool_b, pool_scale, attn_w_in, attn_q_g, attn_k_g, attn_lambda, attn_sub_g, attn_w_out, rel_bias):
    B, S, _ = x_prompt.shape
    SB, T, _ = x_sample.shape
    P = cache_k.shape[2]
    assert SB == ADA_PROMPT_ROW0 and ADA_PROMPT_ROW0 + B <= ADA_ROWS
    n_a, n_b, n_c = lru_w_in.shape[0], pool_w.shape[0], attn_w_in.shape[0]

    c_all = jnp.concatenate(
        [c_sample, c_prompt, jnp.zeros((ADA_ROWS - SB - B, D_MODEL), F32)], axis=0)
    mods = _adaln(c_all, ada_w, ada_b)
    norm_g4 = norm_g.reshape(DEPTH, 3, 1, D_MODEL)

    ffn_in_b = ffn_w_in.astype(BF16)
    ffn_out_b = ffn_w_out.astype(BF16)
    lru_in_b = lru_w_in.astype(BF16)
    lru_out_b = lru_w_out.astype(BF16)
    per_super = LRU_BLOCKS // LRU_SUPER
    eye = jnp.eye(per_super, dtype=F32)

    def super_blocks(w):
        w5 = w.reshape(w.shape[0], LRU_SUPER, per_super, LRU_BS, LRU_BS)
        full = jnp.einsum('nshij,hg->nshigj', w5, eye)
        return full.reshape(w.shape[0], LRU_SUPER, per_super * LRU_BS, per_super * LRU_BS)

    lru_gate_b = jnp.concatenate([super_blocks(lru_ga_w), super_blocks(lru_gx_w)], axis=-1).astype(BF16)
    vec3 = lambda v: v.reshape(v.shape[0], 1, v.shape[-1])
    pool_w_b = pool_w.astype(BF16)
    attn_in_b = attn_w_in.astype(BF16)
    attn_out_b = attn_w_out.astype(BF16)
    seg = jnp.kron(jnp.eye(D_MODEL // HEAD_DIM, dtype=F32),
                   jnp.full((HEAD_DIM, HEAD_DIM), 1.0 / HEAD_DIM, F32)).astype(BF16)

    rows_p = {r: _Rows(B, S, r, True) for r in {FFN_ROWS, POOL_ROWS, QKV_ROWS, OUT_ROWS}}
    rows_lru = _Rows(B, S, LRU_ROWS, True, seqs_per_block=B)
    rows_s = _Rows(SB, T, T, False)

    xp, xs = x_prompt, x_sample
    h_p, h_s, cv_p, cv_s, pl_p, pl_s, k_p, v_p, k_s, v_s = ([] for _ in range(10))
    for l in range(DEPTH):
        kind, j = l % 3, l // 3
        xp = _ffn(xp, rows_p[FFN_ROWS], mods, norm_g4, ffn_in_b, ffn_out_b, l, 0)
        xs = _ffn(xs, rows_s, mods, norm_g4, ffn_in_b, ffn_out_b, l, 0)
        if kind == 0:
            prm = (lru_in_b, lru_conv_w, vec3(lru_conv_b), lru_gate_b, vec3(lru_ga_b),
                   vec3(lru_gx_b), vec3(lru_lambda), lru_out_b)
            xp, hp, bp = _lru(xp, rows_lru, mods, norm_g4, l, j, *prm,
                              jnp.zeros((B, LRU_TAIL, D_RNN), F32), jnp.zeros((B, 1, D_RNN), F32))
            conv0 = jnp.zeros((SB, LRU_TAIL, D_RNN), F32).at[:, SUBLANES - 1::SUBLANES].set(
                state_lru_conv[j])
            xs, hs, bs = _lru(xs, rows_s, mods, norm_g4, l, j, *prm, conv0,
                              state_lru_h[j][:, None, :])
            h_p.append(hp), h_s.append(hs), cv_p.append(bp), cv_s.append(bs)
        elif kind == 1:
            prm = (pool_w_b, vec3(pool_b), vec3(pool_scale))
            xp, sp = _pool(xp, rows_p[POOL_ROWS], mods, norm_g4, l, j,
                           jnp.zeros((B, POOL_PAD, D_MODEL), F32), *prm, 0)
            hist0 = jnp.pad(state_pool[j], ((0, 0), (POOL_PAD - POOL_HIST, 0), (0, 0)))
            xs, ss = _pool(xs, rows_s, mods, norm_g4, l, j, hist0, *prm, POOL_HIST)
            pl_p.append(sp), pl_s.append(ss)
        else:
            lam_init = 0.8 - 0.6 * math.exp(-0.3 * l)
            qg = jnp.tile(attn_q_g[j], D_MODEL // HEAD_DIM)[None]
            kg = jnp.tile(attn_k_g[j], D_MODEL // HEAD_DIM)[None]
            subg = attn_sub_g[j][None]
            q, kf, kb, vf, vt = _qkv(xp, rows_p[QKV_ROWS], mods, norm_g4, l, j, attn_in_b, qg, kg, seg,
                                     True)
            assert ATT_TQ == 2 * ATT_TK
            bias_p = _bias_tile(rel_bias, 4 * ATT_TK, ATT_TQ, -2 * ATT_TK, 2 * ATT_TK,
                                keys_on_rows=True)
            bias_p = bias_p.reshape(N_HEADS, 4, ATT_TK, ATT_TQ)
            o = _attn_prompt(q, kb, vt, bias_p, attn_lambda, attn_sub_g[j][:, None], j, lam_init)
            xp = _proj_res(xp, o, rows_p[OUT_ROWS], mods, l, j, attn_out_b)
            k_p.append(kf.reshape(B, S, N_HEADS, HEAD_W)), v_p.append(vf.reshape(B, S, N_HEADS, HEAD_W))
            q, kf, kb, vf, vb = _qkv(xs, rows_s, mods, norm_g4, l, j, attn_in_b, qg, kg, seg, False)
            bias_c = _bias_tile(rel_bias, T, P, -P, None)
            bias_n = _bias_tile(rel_bias, T, T, 0, None)
            o = _attn_sample(q, cache_k[j].reshape(SB, P, D_MODEL), cache_v[j].reshape(SB, P, D_MODEL),
                             kb, vb, bias_c, bias_n, attn_lambda, subg, j, lam_init)
            xs = _proj_res(xs, o, rows_s, mods, l, j, attn_out_b)
            k_s.append(kf.reshape(SB, T, N_HEADS, HEAD_W)), v_s.append(vf.reshape(SB, T, N_HEADS, HEAD_W))
        xp = _ffn(xp, rows_p[FFN_ROWS], mods, norm_g4, ffn_in_b, ffn_out_b, l, 1)
        xs = _ffn(xs, rows_s, mods, norm_g4, ffn_in_b, ffn_out_b, l, 1)
    return (xp, xs, jnp.stack(h_p), jnp.stack(h_s), jnp.stack(cv_p), jnp.stack(cv_s),
            jnp.stack(pl_p), jnp.stack(pl_s), jnp.stack(k_p), jnp.stack(v_p),
            jnp.stack(k_s), jnp.stack(v_s))
```

```python
import functools
import math

import jax
import jax.numpy as jnp
import numpy as np
from jax import lax
from jax.experimental import pallas as pl
from jax.experimental.pallas import tpu as pltpu

F32 = jnp.float32
BF16 = jnp.bfloat16

D_MODEL = 1024
DEPTH = 4
D_FF = 2816
D_RNN = 1280
LRU_BLOCKS = 16
LRU_BS = D_RNN // LRU_BLOCKS
LRU_SUPER = 2
CONV_W = 4
LRU_C = 8.0
POOL_WINDOWS = (2, 4, 8, 16)
POOL_GW = D_MODEL // len(POOL_WINDOWS)
POOL_HIST = 15
N_HEADS = 8
HEAD_DIM = D_MODEL // (2 * N_HEADS)
HEAD_W = 2 * HEAD_DIM
NUM_BUCKETS = 32
MAX_DISTANCE = 128
CHUNK = 64
EPS = 1e-6
NEG_INF = -1e30

ADA_ROWS = 16
ADA_PROMPT_ROW0 = 8

VMEM_LIMIT_BYTES = 60 * 1024 * 1024
SUBLANES = 8
LANES = 128

FFN_ROWS = 1024
FFN_CHUNK = 256
LRU_ROWS = 256
LRU_PERM_ROWS = 256
POOL_ROWS = 512
QKV_ROWS = 512
ATT_TQ = 1024
ATT_TK = 512
BIAS_TILE_ROWS = 128
ATT_ROW_CHUNK = 256
VT_ROWS = HEAD_W + 16
LOG2E = math.log2(math.e)
OUT_ROWS = 512


def _cparams():
    return pltpu.CompilerParams(
        dimension_semantics=("arbitrary", "arbitrary"),
        vmem_limit_bytes=VMEM_LIMIT_BYTES,
    )


def _sigmoid(x):
    return 0.5 * jnp.tanh(0.5 * x) + 0.5


def _modulate(x, g, shift, scale):
    ms = jnp.mean(x * x, axis=-1, keepdims=True)
    y = x * lax.rsqrt(ms + EPS)
    return (y * g) * (1.0 + scale) + shift


def _dot(a, b):
    return jnp.dot(a, b, preferred_element_type=F32)


def _dot_nt(a, b):
    return lax.dot_general(a, b, (((1,), (1,)), ((), ())), preferred_element_type=F32)


def _adaln_kernel(c_ref, w_ref, b_ref, o_ref):
    c = c_ref[...]
    s = c * _sigmoid(c)
    y = _dot(s.astype(BF16), w_ref[0].astype(BF16))
    o_ref[0, 0] = y + b_ref[0, 0]


def _adaln(c_all, ada_w, ada_b):
    n_vec = 9
    out = pl.pallas_call(
        _adaln_kernel,
        grid=(DEPTH, n_vec),
        in_specs=[
            pl.BlockSpec((ADA_ROWS, D_MODEL), lambda l, k: (0, 0)),
            pl.BlockSpec((1, D_MODEL, D_MODEL), lambda l, k: (l, 0, k)),
            pl.BlockSpec((1, 1, 1, D_MODEL), lambda l, k: (l, k, 0, 0)),
        ],
        out_specs=pl.BlockSpec((1, 1, ADA_ROWS, D_MODEL), lambda l, k: (l, k, 0, 0)),
        out_shape=jax.ShapeDtypeStruct((DEPTH, n_vec, ADA_ROWS, D_MODEL), F32),
        compiler_params=_cparams(),
        name="adaln",
    )(c_all, ada_w, ada_b.reshape(DEPTH, n_vec, 1, D_MODEL))
    return out.reshape(DEPTH, n_vec, ADA_ROWS, 1, D_MODEL)


class _Rows:
    def __init__(self, batch, seq, rows, prompt, seqs_per_block=1):
        self.prompt = prompt
        if prompt:
            assert seq % rows == 0 and batch % seqs_per_block == 0
            assert ADA_PROMPT_ROW0 % seqs_per_block == 0
            self.nb, self.r = seqs_per_block, rows
            self.grid = (batch // seqs_per_block, seq // rows)
        else:
            self.nb, self.r = batch, seq
            self.grid = (1, 1)

    def act(self, width):
        return pl.BlockSpec((self.nb, self.r, width), lambda b, t: (b, t, 0))

    def per_seq(self, rows, width):
        return pl.BlockSpec((self.nb, rows, width), lambda b, t: (b, 0, 0))

    def mod(self, layer, k):
        if self.prompt:
            first = ADA_PROMPT_ROW0 // self.nb
            return pl.BlockSpec((1, 1, self.nb, 1, D_MODEL), lambda b, t: (layer, k, first + b, 0, 0))
        return pl.BlockSpec((1, 1, self.nb, 1, D_MODEL), lambda b, t: (layer, k, 0, 0, 0))


def _const_spec(shape, index, single_buffer=False):
    if single_buffer:
        return pl.BlockSpec(shape, lambda b, t: index, pipeline_mode=pl.Buffered(1))
    return pl.BlockSpec(shape, lambda b, t: index)


def _ffn_rows(x_ref, sh_ref, sc_ref, gt_ref, g_ref, win_ref, wout_ref, o_ref):
    x = x_ref[...]
    nb, r, _ = x.shape
    u = _modulate(x, g_ref[0, 0], sh_ref[0, 0], sc_ref[0, 0])
    ub = u.reshape(nb * r, D_MODEL).astype(BF16)
    acc = None
    for c in range(D_FF // FFN_CHUNK):
        lo = c * FFN_CHUNK
        a = _dot(ub, win_ref[0, 0, :, lo:lo + FFN_CHUNK])
        b = _dot(ub, win_ref[0, 0, :, D_FF + lo:D_FF + lo + FFN_CHUNK])
        h = ((a * _sigmoid(a)) * b).astype(BF16)
        y = _dot(h, wout_ref[0, 0, lo:lo + FFN_CHUNK, :])
        acc = y if acc is None else acc + y
    o_ref[...] = x + (0.5 * gt_ref[0, 0]) * acc.reshape(nb, r, D_MODEL)


def _ffn_kernel(xp_ref, xs_ref, shp_ref, scp_ref, gtp_ref, shs_ref, scs_ref, gts_ref, g_ref,
                win_ref, wout_ref, op_ref, os_ref, *, n_prompt_tiles):
    i = pl.program_id(0)

    @pl.when(i < n_prompt_tiles)
    def _():
        _ffn_rows(xp_ref, shp_ref, scp_ref, gtp_ref, g_ref, win_ref, wout_ref, op_ref)

    @pl.when(i == n_prompt_tiles)
    def _():
        _ffn_rows(xs_ref, shs_ref, scs_ref, gts_ref, g_ref, win_ref, wout_ref, os_ref)


def _ffn(xp, xs, mods, norm_g4, w_in, w_out, layer, which):
    k0 = 0 if which == 0 else 6
    g_idx = 0 if which == 0 else 2
    B, S, _ = xp.shape
    SB, T, _ = xs.shape
    per_seq = S // FFN_ROWS
    n = B * per_seq
    tile = lambda i: jnp.minimum(i, n - 1)
    prompt_act = pl.BlockSpec((1, FFN_ROWS, D_MODEL), lambda i: (tile(i) // per_seq, tile(i) % per_seq, 0))
    sample_act = pl.BlockSpec((SB, T, D_MODEL), lambda i: (0, 0, 0))
    prompt_mod = lambda k: pl.BlockSpec(
        (1, 1, 1, 1, D_MODEL), lambda i: (layer, k, ADA_PROMPT_ROW0 + tile(i) // per_seq, 0, 0))
    sample_mod = lambda k: pl.BlockSpec((1, 1, SB, 1, D_MODEL), lambda i: (layer, k, 0, 0, 0))
    const = lambda shape, index: pl.BlockSpec(shape, lambda i: index, pipeline_mode=pl.Buffered(1))
    return pl.pallas_call(
        functools.partial(_ffn_kernel, n_prompt_tiles=n),
        grid=(n + 1,),
        in_specs=[
            prompt_act, sample_act,
            prompt_mod(k0), prompt_mod(k0 + 1), prompt_mod(k0 + 2),
            sample_mod(k0), sample_mod(k0 + 1), sample_mod(k0 + 2),
            pl.BlockSpec((1, 1, 1, D_MODEL), lambda i: (layer, g_idx, 0, 0)),
            const((1, 1, D_MODEL, 2 * D_FF), (layer, which, 0, 0)),
            const((1, 1, D_FF, D_MODEL), (layer, which, 0, 0)),
        ],
        out_specs=[prompt_act, sample_act],
        out_shape=[jax.ShapeDtypeStruct(xp.shape, F32), jax.ShapeDtypeStruct(xs.shape, F32)],
        compiler_params=pltpu.CompilerParams(
            dimension_semantics=("arbitrary",), vmem_limit_bytes=VMEM_LIMIT_BYTES),
        name="ffn",
    )(xp, xs, mods, mods, mods, mods, mods, mods, norm_g4, w_in, w_out)


def _group_scan(a, b):
    row = lax.broadcasted_iota(jnp.int32, a.shape, 1)
    d = 1
    while d < SUBLANES:
        keep = row >= d
        a_sh = jnp.where(keep, pltpu.roll(a, d, axis=1), 1.0)
        b_sh = jnp.where(keep, pltpu.roll(b, d, axis=1), 0.0)
        b = a * b_sh + b
        a = a * a_sh
        d *= 2
    return a, b


def _linear_scan(a, b, h_prev, a_ref, h_ref):
    nb, r, C = a.shape
    first = slice(0, SUBLANES)
    a_run, h_run = a[:, first], b[:, first]
    a_ref[:, first, :] = a_run
    h_ref[:, first, :] = h_run
    for g0 in range(SUBLANES, r, SUBLANES):
        rows = slice(g0, g0 + SUBLANES)
        h_run = a[:, rows] * h_run + b[:, rows]
        a_run = a[:, rows] * a_run
        a_ref[:, rows, :] = a_run
        h_ref[:, rows, :] = h_run
    a_inc, h_inc = _group_scan(a_run, h_run)
    seg_end = a_inc * h_prev + h_inc
    row = lax.broadcasted_iota(jnp.int32, seg_end.shape, 1)
    seg_in = jnp.where(row == 0, h_prev, pltpu.roll(seg_end, 1, axis=1))
    for g0 in range(0, r, SUBLANES):
        rows = slice(g0, g0 + SUBLANES)
        h_ref[:, rows, :] = h_ref[:, rows, :] + a_ref[:, rows, :] * seg_in
    return seg_end[:, SUBLANES - 1:, :]


LRU_TAIL = (CONV_W - 1) * SUBLANES


def _lru_kernel(x_ref, sh_ref, sc_ref, gt_ref, g_ref, win_ref, cw_ref, cb_ref, gw_ref, gab_ref,
                gxb_ref, lam_ref, wout_ref, conv0_ref, h0_ref, perm_ref, unperm_ref,
                o_ref, hlast_ref, tail_ref, h_scr, tail_scr, arun_scr, hall_scr):
    @pl.when(pl.program_id(1) == 0)
    def _():
        h_scr[...] = h0_ref[...]
        tail_scr[...] = conv0_ref[...]

    x = x_ref[...]
    nb, r, _ = x.shape
    assert r // SUBLANES >= CONV_W - 1
    C = D_RNN
    u = _modulate(x, g_ref[0, 0], sh_ref[0, 0], sc_ref[0, 0])
    n_perm = perm_ref.shape[0]

    def permute(p_ref, v):
        parts = [_dot(p_ref[...], v[lo:lo + n_perm]) for lo in range(0, nb * r, n_perm)]
        return jnp.concatenate(parts, axis=0).astype(BF16)

    ub = permute(perm_ref, u.reshape(nb * r, D_MODEL).astype(BF16))
    gate_br = _dot(ub, win_ref[0, :, :C])
    x_br = _dot(ub, win_ref[0, :, C:]).reshape(nb, r, C)

    last = x_br[:, r - LRU_TAIL:, :]
    groups = (nb * (CONV_W - 1), SUBLANES, C)
    row = lax.broadcasted_iota(jnp.int32, groups, 1)
    wrap = jnp.where(row == 0, pltpu.roll(tail_scr[...].reshape(groups), 1, axis=1),
                     pltpu.roll(last.reshape(groups), 1, axis=1)).reshape(nb, LRU_TAIL, C)
    tail_scr[...] = last
    tail_ref[...] = last
    cw = cw_ref[0]
    xc = cb_ref[0] + cw[CONV_W - 1:CONV_W] * x_br
    for d in range(1, CONV_W):
        shifted = jnp.concatenate([wrap[:, LRU_TAIL - d * SUBLANES:], x_br[:, :r - d * SUBLANES]], axis=1)
        xc = xc + cw[CONV_W - 1 - d:CONV_W - d] * shifted

    xc2 = xc.reshape(nb * r, C)
    xcb = xc2.astype(BF16)
    sw = C // LRU_SUPER
    gates = [_dot(xcb[:, s * sw:(s + 1) * sw], gw_ref[0, s]) for s in range(LRU_SUPER)]
    rg = _sigmoid(jnp.concatenate([g[:, :sw] for g in gates], axis=-1) + gab_ref[0])
    ig = _sigmoid(jnp.concatenate([g[:, sw:] for g in gates], axis=-1) + gxb_ref[0])
    nl = -lam_ref[0]
    softplus = jnp.maximum(nl, 0.0) + jnp.log1p(jnp.exp(-jnp.abs(nl)))
    log_a = (-LRU_C * rg) * softplus
    a = jnp.exp(log_a)
    th = jnp.tanh(log_a)
    one_minus_a2 = (-2.0 * th) / (1.0 - th)
    root = jnp.where(one_minus_a2 > 0.0, one_minus_a2 * lax.rsqrt(one_minus_a2), 0.0)
    b_in = root * (ig * xc2)

    h_last = _linear_scan(a.reshape(nb, r, C), b_in.reshape(nb, r, C), h_scr[...], arun_scr, hall_scr)
    h_scr[...] = h_last
    hlast_ref[...] = h_last

    gb = gate_br
    gelu = 0.5 * gb * (1.0 + jnp.tanh(math.sqrt(2.0 / math.pi) * (gb + 0.044715 * (gb * gb * gb))))
    z = (gelu * hall_scr[...].reshape(nb * r, C)).astype(BF16)
    y = _dot(permute(unperm_ref, z), wout_ref[0])
    o_ref[...] = x + gt_ref[0, 0] * y.reshape(nb, r, D_MODEL)


def _lru(x, rows, mods, norm_g4, layer, j, w_in, conv_w, conv_b, gate_w, ga_b, gx_b, lam, w_out,
         conv0, h0):
    nbt = x.shape[0]
    C = D_RNN
    seqs = max(1, LRU_PERM_ROWS // rows.r)
    assert rows.nb % seqs == 0
    n_rows, G = seqs * rows.r, rows.r // SUBLANES
    b_idx, g_idx, i_idx = np.meshgrid(np.arange(seqs), np.arange(G), np.arange(SUBLANES), indexing='ij')
    source = (b_idx * rows.r + i_idx * G + g_idx).reshape(n_rows)
    perm = np.zeros((n_rows, n_rows), np.float32)
    perm[np.arange(n_rows), source] = 1.0
    vec = lambda: _const_spec((1, 1, C), (j, 0, 0))
    out, h_last, tail = pl.pallas_call(
        _lru_kernel,
        grid=rows.grid,
        in_specs=[
            rows.act(D_MODEL),
            rows.mod(layer, 3), rows.mod(layer, 4), rows.mod(layer, 5),
            _const_spec((1, 1, 1, D_MODEL), (layer, 1, 0, 0)),
            _const_spec((1, D_MODEL, 2 * C), (j, 0, 0)),
            _const_spec((1, CONV_W, C), (j, 0, 0)),
            vec(),
            _const_spec((1, LRU_SUPER, C // LRU_SUPER, 2 * C // LRU_SUPER), (j, 0, 0, 0)),
            vec(), vec(), vec(),
            _const_spec((1, C, D_MODEL), (j, 0, 0)),
            rows.per_seq(LRU_TAIL, C),
            rows.per_seq(1, C),
            _const_spec((n_rows, n_rows), (0, 0)),
            _const_spec((n_rows, n_rows), (0, 0)),
        ],
        out_specs=[rows.act(D_MODEL), rows.per_seq(1, C), rows.per_seq(LRU_TAIL, C)],
        out_shape=[
            jax.ShapeDtypeStruct(x.shape, F32),
            jax.ShapeDtypeStruct((nbt, 1, C), F32),
            jax.ShapeDtypeStruct((nbt, LRU_TAIL, C), F32),
        ],
        scratch_shapes=[pltpu.VMEM((rows.nb, 1, C), F32), pltpu.VMEM((rows.nb, LRU_TAIL, C), F32),
                        pltpu.VMEM((rows.nb, rows.r, C), F32), pltpu.VMEM((rows.nb, rows.r, C), F32)],
        compiler_params=_cparams(),
        name="lru",
    )(x, mods, mods, mods, norm_g4, w_in, conv_w, conv_b, gate_w, ga_b, gx_b, lam, w_out, conv0, h0,
      jnp.asarray(perm, BF16), jnp.asarray(perm.T, BF16))
    return out, h_last[:, 0], tail[:, SUBLANES - 1::SUBLANES]


POOL_PAD = 16


def _pool_kernel(x_ref, sh_ref, sc_ref, gt_ref, g_ref, hist0_ref, w_ref, b_ref, ps_ref,
                 o_ref, st_ref, hist_scr, *, t_base):
    t = pl.program_id(1)

    @pl.when(t == 0)
    def _():
        hist_scr[...] = hist0_ref[...]

    x = x_ref[...]
    nb, r, _ = x.shape
    u = _modulate(x, g_ref[0, 0], sh_ref[0, 0], sc_ref[0, 0])
    ext = jnp.concatenate([hist_scr[...], u], axis=1)
    new_hist = u[:, r - POOL_PAD:, :]
    hist_scr[...] = new_hist
    st_ref[...] = new_hist

    row = lax.broadcasted_iota(jnp.int32, (1, r, 1), 1)
    t_glob = t_base + t * r + row
    gate = gt_ref[0, 0]
    s = ext
    for g, wnd in enumerate(POOL_WINDOWS):
        lo = g * POOL_GW
        s = s[:, :, (POOL_GW if g > 0 else 0):]
        s = s + pltpu.roll(s, wnd // 2, axis=1)
        cnt = jnp.minimum(t_glob + 1, wnd).astype(F32)
        mean = s[:, POOL_PAD:, :POOL_GW] / cnt
        d = mean - u[:, :, lo:lo + POOL_GW]
        y = _dot(d.reshape(nb * r, POOL_GW).astype(BF16), w_ref[0, g]).reshape(nb, r, POOL_GW)
        y = (y + b_ref[0, :, lo:lo + POOL_GW]) * ps_ref[0, :, lo:lo + POOL_GW]
        o_ref[:, :, lo:lo + POOL_GW] = x[:, :, lo:lo + POOL_GW] + gate[:, :, lo:lo + POOL_GW] * y


def _pool(x, rows, mods, norm_g4, layer, j, hist0, w, b, scale, t_base):
    nbt = x.shape[0]
    out, st = pl.pallas_call(
        functools.partial(_pool_kernel, t_base=t_base),
        grid=rows.grid,
        in_specs=[
            rows.act(D_MODEL),
            rows.mod(layer, 3), rows.mod(layer, 4), rows.mod(layer, 5),
            _const_spec((1, 1, 1, D_MODEL), (layer, 1, 0, 0)),
            rows.per_seq(POOL_PAD, D_MODEL),
            _const_spec((1, len(POOL_WINDOWS), POOL_GW, POOL_GW), (j, 0, 0, 0)),
            _const_spec((1, 1, D_MODEL), (j, 0, 0)),
            _const_spec((1, 1, D_MODEL), (j, 0, 0)),
        ],
        out_specs=[rows.act(D_MODEL), rows.per_seq(POOL_PAD, D_MODEL)],
        out_shape=[jax.ShapeDtypeStruct(x.shape, F32),
                   jax.ShapeDtypeStruct((nbt, POOL_PAD, D_MODEL), F32)],
        scratch_shapes=[pltpu.VMEM((rows.nb, POOL_PAD, D_MODEL), F32)],
        compiler_params=_cparams(),
        name="pool",
    )(x, mods, mods, mods, norm_g4, hist0, w, b, scale)
    return out, st[:, POOL_PAD - POOL_HIST:]


def _qkv_kernel(x_ref, sh_ref, sc_ref, g_ref, w_ref, qg_ref, kg_ref, seg_ref,
                q_ref, kf_ref, kb_ref, vf_ref, vb_ref, *, v_transposed):
    x = x_ref[...]
    nb, r, _ = x.shape
    u = _modulate(x, g_ref[0, 0], sh_ref[0, 0], sc_ref[0, 0])
    ub = u.reshape(nb * r, D_MODEL).astype(BF16)
    q = _dot(ub, w_ref[0, :, :D_MODEL])
    k = _dot(ub, w_ref[0, :, D_MODEL:2 * D_MODEL])
    v = _dot(ub, w_ref[0, :, 2 * D_MODEL:])

    def head_norm(z, gain):
        ms = _dot((z * z).astype(BF16), seg_ref[...])
        return (z * lax.rsqrt(ms + EPS)) * gain

    qn = head_norm(q, qg_ref[...]) * (HEAD_DIM ** -0.5 * LOG2E)
    kn = head_norm(k, kg_ref[...])
    lane = lax.broadcasted_iota(jnp.int32, (1, D_MODEL), 1)
    first = (lane % HEAD_W) < HEAD_DIM
    q_ref[:, 0] = jnp.where(first, qn, 0.0).astype(BF16).reshape(nb, r, D_MODEL)
    q_ref[:, 1] = jnp.where(first, 0.0, qn).astype(BF16).reshape(nb, r, D_MODEL)
    kf_ref[...] = kn.reshape(nb, r, D_MODEL)
    kb_ref[...] = kn.astype(BF16).reshape(nb, r, D_MODEL)
    vf_ref[...] = v.reshape(nb, r, D_MODEL)
    if v_transposed:
        for h in range(N_HEADS):
            vb_ref[0, h, :HEAD_W] = v[:, h * HEAD_W:(h + 1) * HEAD_W].T.astype(BF16)
            vb_ref[0, h, HEAD_W:] = jnp.ones((VT_ROWS - HEAD_W, r), BF16)
    else:
        vb_ref[...] = v.astype(BF16).reshape(nb, r, D_MODEL)


def _qkv(x, rows, mods, norm_g4, layer, j, w_in, qg, kg, seg, v_transposed):
    nbt, seq, _ = x.shape
    q_spec = pl.BlockSpec((rows.nb, 2, rows.r, D_MODEL), lambda b, t: (b, 0, t, 0))
    if v_transposed:
        assert rows.nb == 1
        vb_spec = pl.BlockSpec((1, N_HEADS, VT_ROWS, rows.r), lambda b, t: (b, 0, 0, t))
        vb_shape = jax.ShapeDtypeStruct((nbt, N_HEADS, VT_ROWS, seq), BF16)
    else:
        vb_spec = rows.act(D_MODEL)
        vb_shape = jax.ShapeDtypeStruct(x.shape, BF16)
    return pl.pallas_call(
        functools.partial(_qkv_kernel, v_transposed=v_transposed),
        grid=rows.grid,
        in_specs=[
            rows.act(D_MODEL),
            rows.mod(layer, 3), rows.mod(layer, 4),
            _const_spec((1, 1, 1, D_MODEL), (layer, 1, 0, 0)),
            _const_spec((1, D_MODEL, 3 * D_MODEL), (j, 0, 0)),
            _const_spec((1, D_MODEL), (0, 0)),
            _const_spec((1, D_MODEL), (0, 0)),
            _const_spec((D_MODEL, D_MODEL), (0, 0)),
        ],
        out_specs=[q_spec, rows.act(D_MODEL), rows.act(D_MODEL), rows.act(D_MODEL), vb_spec],
        out_shape=[
            jax.ShapeDtypeStruct((nbt, 2, seq, D_MODEL), BF16),
            jax.ShapeDtypeStruct(x.shape, F32),
            jax.ShapeDtypeStruct(x.shape, BF16),
            jax.ShapeDtypeStruct(x.shape, F32),
            vb_shape,
        ],
        compiler_params=_cparams(),
        name="qkv",
    )(x, mods, mods, norm_g4, w_in, qg, kg, seg)


_BUCKET_EDGES = (12, 16, 23, 32, 46, 64, 91)
_FAR_BUCKET = NUM_BUCKETS // 2 - 1


def _bucket_of(rel):
    n = abs(rel)
    small = NUM_BUCKETS // 4
    b = n if n < small else small + sum(n >= e for e in _BUCKET_EDGES)
    return b + (NUM_BUCKETS // 2 if rel > 0 else 0)


_BUCKET_RUNS = tuple((r, _bucket_of(r)) for r in range(-_BUCKET_EDGES[-1] + 1, _BUCKET_EDGES[-1] + 1)
                     if _bucket_of(r) != _bucket_of(r - 1))


def _bias_kernel(rb_ref, o_ref, *, rel0, mask_from, keys_on_rows, tile, tiles):
    h = pl.program_id(0)
    R, W = tile
    for row0, col0, fill in tiles:
        window = (0, slice(row0, row0 + R), slice(col0, col0 + W))
        if fill is not None:
            o_ref[window] = jnp.full((R, W), fill, F32)
            continue
        i = lax.broadcasted_iota(jnp.int32, (R, W), 0) + row0
        jj = lax.broadcasted_iota(jnp.int32, (R, W), 1) + col0
        key, qry = (i, jj) if keys_on_rows else (jj, i)
        rel = key - qry + rel0
        out = jnp.full((R, W), rb_ref[_FAR_BUCKET, h], F32)
        for first_rel, bucket in _BUCKET_RUNS:
            out = jnp.where(rel >= first_rel, rb_ref[bucket, h], out)
        out = (out - rb_ref[_FAR_BUCKET, h]) * LOG2E
        if mask_from is not None:
            visible = ((key - mask_from) // CHUNK <= qry // CHUNK) | (key < mask_from)
            out = jnp.where(visible, out, NEG_INF)
        o_ref[window] = out


def _bias_tile(rel_bias, rows, width, rel0, mask_from, keys_on_rows=False):
    R = BIAS_TILE_ROWS if rows % BIAS_TILE_ROWS == 0 else rows
    W = LANES if width % LANES == 0 else width
    tiles = []
    for row0 in range(0, rows, R):
        for col0 in range(0, width, W):
            key0, qry0 = (row0, col0) if keys_on_rows else (col0, row0)
            key1, qry1 = (row0 + R, col0 + W) if keys_on_rows else (col0 + W, row0 + R)
            fill = None
            unmasked = mask_from is None or key1 <= mask_from
            if unmasked and (key1 - 1) - qry0 + rel0 <= -_BUCKET_EDGES[-1]:
                fill = 0.0
            if mask_from is not None and key0 >= mask_from and \
                    (key0 - mask_from) // CHUNK > (qry1 - 1) // CHUNK:
                fill = NEG_INF
            tiles.append((row0, col0, fill))
    return pl.pallas_call(
        functools.partial(_bias_kernel, rel0=rel0, mask_from=mask_from, keys_on_rows=keys_on_rows,
                          tile=(R, W), tiles=tuple(tiles)),
        grid=(N_HEADS,),
        in_specs=[pl.BlockSpec(memory_space=pltpu.SMEM)],
        out_specs=pl.BlockSpec((1, rows, width), lambda h: (h, 0, 0)),
        out_shape=jax.ShapeDtypeStruct((N_HEADS, rows, width), F32),
        name="bias_tile",
    )(rel_bias)


def _diff_lambda(lp, lam_init):
    s1 = jnp.sum(lp[0:1] * lp[1:2], axis=-1, keepdims=True)
    s2 = jnp.sum(lp[2:3] * lp[3:4], axis=-1, keepdims=True)
    return jnp.exp(s1) - jnp.exp(s2) + lam_init


def _attn_finish(acc, l, nq, lam, subg, lam_init):
    o = acc / l
    o = o[:nq] - lam * o[nq:]
    ms = jnp.mean(o * o, axis=-1, keepdims=True)
    return ((o * lax.rsqrt(ms + EPS)) * subg) * (1.0 - lam_init)


def _attn_prompt_kernel(q_ref, k_ref, vt_ref, bias_ref, lam_ref, subg_ref, o_ref,
                        qt_scr, s0_scr, s1_scr, mx0_scr, mx1_scr, m_scr, acc_scr, *, lam_init):
    qi = pl.program_id(2)
    tq, tk = ATT_TQ, ATT_TK
    width = 2 * tq
    n_blocks = (qi + 1) * (tq // tk)
    qt_scr[:, :tq] = q_ref[0, 0].astype(F32).T.astype(BF16)
    qt_scr[:, tq:] = q_ref[0, 1].astype(F32).T.astype(BF16)
    m_scr[...] = jnp.full(m_scr.shape, NEG_INF, F32)
    acc_scr[...] = jnp.zeros(acc_scr.shape, F32)

    def block_start(c):
        return c * tk

    def scores_rows(c, s_ref, near, r0):
        rows = slice(r0, r0 + ATT_ROW_CHUNK)
        start = pl.multiple_of(block_start(c) + r0, ATT_ROW_CHUNK)
        s = _dot(k_ref[0, pl.ds(start, ATT_ROW_CHUNK), :], qt_scr[...])
        if near is not None:
            bias = bias_ref[0, near, rows, :]
            s = s + jnp.concatenate([bias, bias], axis=1)
        s_ref[rows, :] = s
        return jnp.max(s.reshape(ATT_ROW_CHUNK // SUBLANES, SUBLANES, width), axis=0)

    def new_max(mx_ref):
        m_old = m_scr[...]
        m_new = jnp.maximum(m_old, jnp.max(mx_ref[...], axis=0, keepdims=True))
        m_scr[...] = m_new
        return m_new, jnp.exp2(m_old - m_new)

    def absorb_rows(c, s_ref, m_new, r0):
        p = jnp.exp2((s_ref[r0:r0 + ATT_ROW_CHUNK, :] - m_new).astype(BF16))
        start = pl.multiple_of(block_start(c) + r0, ATT_ROW_CHUNK)
        return _dot(vt_ref[0, 0, :, pl.ds(start, ATT_ROW_CHUNK)], p)

    def rescale_add(alpha, parts):
        acc_scr[...] = alpha * acc_scr[...] + functools.reduce(jnp.add, parts)

    chunks = range(0, tk, ATT_ROW_CHUNK)

    def scores(c, s_ref, mx_ref, near):
        mx_ref[...] = functools.reduce(jnp.maximum, [scores_rows(c, s_ref, near, r0) for r0 in chunks])

    def absorb(c, s_ref, mx_ref):
        m_new, alpha = new_max(mx_ref)
        rescale_add(alpha, [absorb_rows(c, s_ref, m_new, r0) for r0 in chunks])

    def overlap(c_new, s_new, mx_new, near, c_old, s_old, mx_old):
        scores(c_new, s_new, mx_new, near)
        absorb(c_old, s_old, mx_old)

    @pl.when(qi == 0)
    def _():
        scores(0, s0_scr, mx0_scr, 1)

    @pl.when(qi > 0)
    def _():
        scores(0, s0_scr, mx0_scr, None)

    def far_trip(t, carry):
        c = 2 * t
        overlap(c + 1, s1_scr, mx1_scr, None, c, s0_scr, mx0_scr)
        overlap(c + 2, s0_scr, mx0_scr, None, c + 1, s1_scr, mx1_scr)
        return carry

    lax.fori_loop(0, jnp.maximum(n_blocks // 2 - 2, 0), far_trip, 0)

    @pl.when(qi > 0)
    def _():
        c = n_blocks - 4
        overlap(c + 1, s1_scr, mx1_scr, 0, c, s0_scr, mx0_scr)
        overlap(c + 2, s0_scr, mx0_scr, 1, c + 1, s1_scr, mx1_scr)

    overlap(n_blocks - 1, s1_scr, mx1_scr, 2, n_blocks - 2, s0_scr, mx0_scr)
    absorb(n_blocks - 1, s1_scr, mx1_scr)

    lam = _diff_lambda(lam_ref[0], lam_init)
    acc = acc_scr[...]
    o = acc[:HEAD_W] / acc[HEAD_W:HEAD_W + 1]
    o = o[:, :tq] - lam * o[:, tq:]
    ms = jnp.mean(o * o, axis=0, keepdims=True)
    on = ((o * lax.rsqrt(ms + EPS)) * subg_ref[...]) * (1.0 - lam_init)
    o_ref[0] = on.T.astype(BF16)


def _attn_prompt(q, kb, vt, bias, lam_p, subg_col, j, lam_init):
    nbt, _, seq, _ = q.shape
    tq, tk = ATT_TQ, ATT_TK
    return pl.pallas_call(
        functools.partial(_attn_prompt_kernel, lam_init=lam_init),
        grid=(nbt, N_HEADS, seq // tq),
        in_specs=[
            pl.BlockSpec((1, 2, tq, HEAD_W), lambda b, h, i: (b, 0, i, h)),
            pl.BlockSpec((1, seq, HEAD_W), lambda b, h, i: (b, 0, h)),
            pl.BlockSpec((1, 1, VT_ROWS, seq), lambda b, h, i: (b, h, 0, 0)),
            pl.BlockSpec((1, 3, tk, tq), lambda b, h, i: (h, 0, 0, 0)),
            pl.BlockSpec((1, 4, HEAD_DIM), lambda b, h, i: (j, 0, 0)),
            pl.BlockSpec((HEAD_W, 1), lambda b, h, i: (0, 0)),
        ],
        out_specs=pl.BlockSpec((1, tq, HEAD_W), lambda b, h, i: (b, i, h)),
        out_shape=jax.ShapeDtypeStruct((nbt, seq, D_MODEL), BF16),
        scratch_shapes=[pltpu.VMEM((HEAD_W, 2 * tq), BF16),
                        pltpu.VMEM((tk, 2 * tq), F32), pltpu.VMEM((tk, 2 * tq), F32),
                        pltpu.VMEM((SUBLANES, 2 * tq), F32), pltpu.VMEM((SUBLANES, 2 * tq), F32),
                        pltpu.VMEM((1, 2 * tq), F32),
                        pltpu.VMEM((VT_ROWS, 2 * tq), F32)],
        compiler_params=pltpu.CompilerParams(
            dimension_semantics=("arbitrary", "arbitrary", "arbitrary"),
            vmem_limit_bytes=VMEM_LIMIT_BYTES),
        name="attn_prompt",
    )(q, kb, vt, bias, lam_p, subg_col)


def _attn_sample_kernel(q_ref, ck_ref, cv_ref, kn_ref, vn_ref, bc_ref, bn_ref, lam_ref, subg_ref,
                        o_ref, *, lam_init):
    nq = q_ref.shape[2]
    lam = _diff_lambda(lam_ref[0], lam_init)
    for h in range(N_HEADS):
        cols = slice(h * HEAD_W, (h + 1) * HEAD_W)
        qq = q_ref[0, :, :, cols].reshape(2 * nq, HEAD_W)
        kc = ck_ref[0, :, cols].astype(BF16)
        vc = cv_ref[0, :, cols].astype(BF16)
        kn = kn_ref[0, :, cols]
        vn = vn_ref[0, :, cols]
        past = kc.shape[0]
        s_c = (_dot_nt(qq, kc).reshape(2, nq, past) + bc_ref[h][None]).reshape(2 * nq, past)
        s_n = (_dot_nt(qq, kn).reshape(2, nq, nq) + bn_ref[h][None]).reshape(2 * nq, nq)
        m = jnp.maximum(jnp.max(s_c, axis=-1, keepdims=True), jnp.max(s_n, axis=-1, keepdims=True))
        p_c = jnp.exp2(s_c - m)
        p_n = jnp.exp2(s_n - m)
        l = jnp.sum(p_c, axis=-1, keepdims=True) + jnp.sum(p_n, axis=-1, keepdims=True)
        acc = _dot(p_c.astype(BF16), vc) + _dot(p_n.astype(BF16), vn)
        o_ref[0, :, cols] = _attn_finish(acc, l, nq, lam, subg_ref[...], lam_init).astype(BF16)


def _attn_sample(q, ck, cv, kb, vb, bias_c, bias_n, lam_p, subg, j, lam_init):
    nbt, _, nq, _ = q.shape
    past = ck.shape[1]
    return pl.pallas_call(
        functools.partial(_attn_sample_kernel, lam_init=lam_init),
        grid=(nbt,),
        in_specs=[
            pl.BlockSpec((1, 2, nq, D_MODEL), lambda b: (b, 0, 0, 0)),
            pl.BlockSpec((1, past, D_MODEL), lambda b: (b, 0, 0)),
            pl.BlockSpec((1, past, D_MODEL), lambda b: (b, 0, 0)),
            pl.BlockSpec((1, nq, D_MODEL), lambda b: (b, 0, 0)),
            pl.BlockSpec((1, nq, D_MODEL), lambda b: (b, 0, 0)),
            pl.BlockSpec((N_HEADS, nq, past), lambda b: (0, 0, 0)),
            pl.BlockSpec((N_HEADS, nq, nq), lambda b: (0, 0, 0)),
            pl.BlockSpec((1, 4, HEAD_DIM), lambda b: (j, 0, 0)),
            pl.BlockSpec((1, HEAD_W), lambda b: (0, 0)),
        ],
        out_specs=pl.BlockSpec((1, nq, D_MODEL), lambda b: (b, 0, 0)),
        out_shape=jax.ShapeDtypeStruct((nbt, nq, D_MODEL), BF16),
        compiler_params=pltpu.CompilerParams(
            dimension_semantics=("arbitrary",), vmem_limit_bytes=VMEM_LIMIT_BYTES),
        name="attn_sample",
    )(q, ck, cv, kb, vb, bias_c, bias_n, lam_p, subg)


def _proj_res_kernel(x_ref, a_ref, gt_ref, w_ref, o_ref):
    x = x_ref[...]
    nb, r, _ = x.shape
    a = a_ref[...].reshape(nb * r, a_ref.shape[-1])
    y = _dot(a, w_ref[0])
    o_ref[...] = x + gt_ref[0, 0] * y.reshape(nb, r, D_MODEL)


def _proj_res(x, a, rows, mods, layer, j, w):
    return pl.pallas_call(
        _proj_res_kernel,
        grid=rows.grid,
        in_specs=[
            rows.act(D_MODEL), rows.act(a.shape[-1]), rows.mod(layer, 5),
            _const_spec((1,) + w.shape[1:], (j, 0, 0)),
        ],
        out_specs=rows.act(D_MODEL),
        out_shape=jax.ShapeDtypeStruct(x.shape, F32),
        compiler_params=_cparams(),
        name="proj_res",
    )(x, a, mods, w)


def kernel(x_prompt, x_sample, c_prompt, c_sample, state_lru_h, state_lru_conv, state_pool, cache_k, cache_v, ada_w, ada_b, norm_g, ffn_w_in, ffn_w_out, lru_w_in, lru_conv_w, lru_conv_b, lru_ga_w, lru_ga_b, lru_gx_w, lru_gx_b, lru_lambda, lru_w_out, pool_w, pool_b, pool_scale, attn_w_in, attn_q_g, attn_k_g, attn_lambda, attn_sub_g, attn_w_out, rel_bias):
    B, S, _ = x_prompt.shape
    SB, T, _ = x_sample.shape
    P = cache_k.shape[2]
    assert SB == ADA_PROMPT_ROW0 and ADA_PROMPT_ROW0 + B <= ADA_ROWS
    n_a, n_b, n_c = lru_w_in.shape[0], pool_w.shape[0], attn_w_in.shape[0]

    c_all = jnp.concatenate(
        [c_sample, c_prompt, jnp.zeros((ADA_ROWS - SB - B, D_MODEL), F32)], axis=0)
    mods = _adaln(c_all, ada_w, ada_b)
    norm_g4 = norm_g.reshape(DEPTH, 3, 1, D_MODEL)

    ffn_in_b = ffn_w_in.astype(BF16)
    ffn_out_b = ffn_w_out.astype(BF16)
    lru_in_b = lru_w_in.astype(BF16)
    lru_out_b = lru_w_out.astype(BF16)
    per_super = LRU_BLOCKS // LRU_SUPER
    eye = jnp.eye(per_super, dtype=F32)

    def super_blocks(w):
        w5 = w.reshape(w.shape[0], LRU_SUPER, per_super, LRU_BS, LRU_BS)
        full = jnp.einsum('nshij,hg->nshigj', w5, eye)
        return full.reshape(w.shape[0], LRU_SUPER, per_super * LRU_BS, per_super * LRU_BS)

    lru_gate_b = jnp.concatenate([super_blocks(lru_ga_w), super_blocks(lru_gx_w)], axis=-1).astype(BF16)
    vec3 = lambda v: v.reshape(v.shape[0], 1, v.shape[-1])
    pool_w_b = pool_w.astype(BF16)
    attn_in_b = attn_w_in.astype(BF16)
    attn_out_b = attn_w_out.astype(BF16)
    seg = jnp.kron(jnp.eye(D_MODEL // HEAD_DIM, dtype=F32),
                   jnp.full((HEAD_DIM, HEAD_DIM), 1.0 / HEAD_DIM, F32)).astype(BF16)

    rows_p = {r: _Rows(B, S, r, True) for r in {POOL_ROWS, QKV_ROWS, OUT_ROWS}}
    rows_lru = _Rows(B, S, LRU_ROWS, True, seqs_per_block=B)
    rows_s = _Rows(SB, T, T, False)

    xp, xs = x_prompt, x_sample
    h_p, h_s, cv_p, cv_s, pl_p, pl_s, k_p, v_p, k_s, v_s = ([] for _ in range(10))
    for l in range(DEPTH):
        kind, j = l % 3, l // 3
        xp, xs = _ffn(xp, xs, mods, norm_g4, ffn_in_b, ffn_out_b, l, 0)
        if kind == 0:
            prm = (lru_in_b, lru_conv_w, vec3(lru_conv_b), lru_gate_b, vec3(lru_ga_b),
                   vec3(lru_gx_b), vec3(lru_lambda), lru_out_b)
            xp, hp, bp = _lru(xp, rows_lru, mods, norm_g4, l, j, *prm,
                              jnp.zeros((B, LRU_TAIL, D_RNN), F32), jnp.zeros((B, 1, D_RNN), F32))
            conv0 = jnp.zeros((SB, LRU_TAIL, D_RNN), F32).at[:, SUBLANES - 1::SUBLANES].set(
                state_lru_conv[j])
            xs, hs, bs = _lru(xs, rows_s, mods, norm_g4, l, j, *prm, conv0,
                              state_lru_h[j][:, None, :])
            h_p.append(hp), h_s.append(hs), cv_p.append(bp), cv_s.append(bs)
        elif kind == 1:
            prm = (pool_w_b, vec3(pool_b), vec3(pool_scale))
            xp, sp = _pool(xp, rows_p[POOL_ROWS], mods, norm_g4, l, j,
                           jnp.zeros((B, POOL_PAD, D_MODEL), F32), *prm, 0)
            hist0 = jnp.pad(state_pool[j], ((0, 0), (POOL_PAD - POOL_HIST, 0), (0, 0)))
            xs, ss = _pool(xs, rows_s, mods, norm_g4, l, j, hist0, *prm, POOL_HIST)
            pl_p.append(sp), pl_s.append(ss)
        else:
            lam_init = 0.8 - 0.6 * math.exp(-0.3 * l)
            qg = jnp.tile(attn_q_g[j], D_MODEL // HEAD_DIM)[None]
            kg = jnp.tile(attn_k_g[j], D_MODEL // HEAD_DIM)[None]
            subg = attn_sub_g[j][None]
            q, kf, kb, vf, vt = _qkv(xp, rows_p[QKV_ROWS], mods, norm_g4, l, j, attn_in_b, qg, kg, seg,
                                     True)
            assert ATT_TQ == 2 * ATT_TK
            bias_p = _bias_tile(rel_bias, 3 * ATT_TK, ATT_TQ, -ATT_TK, ATT_TK, keys_on_rows=True)
            bias_p = bias_p.reshape(N_HEADS, 3, ATT_TK, ATT_TQ)
            o = _attn_prompt(q, kb, vt, bias_p, attn_lambda, attn_sub_g[j][:, None], j, lam_init)
            xp = _proj_res(xp, o, rows_p[OUT_ROWS], mods, l, j, attn_out_b)
            k_p.append(kf.reshape(B, S, N_HEADS, HEAD_W)), v_p.append(vf.reshape(B, S, N_HEADS, HEAD_W))
            q, kf, kb, vf, vb = _qkv(xs, rows_s, mods, norm_g4, l, j, attn_in_b, qg, kg, seg, False)
            bias_c = _bias_tile(rel_bias, T, P, -P, None)
            bias_n = _bias_tile(rel_bias, T, T, 0, None)
            o = _attn_sample(q, cache_k[j].reshape(SB, P, D_MODEL), cache_v[j].reshape(SB, P, D_MODEL),
                             kb, vb, bias_c, bias_n, attn_lambda, subg, j, lam_init)
            xs = _proj_res(xs, o, rows_s, mods, l, j, attn_out_b)
            k_s.append(kf.reshape(SB, T, N_HEADS, HEAD_W)), v_s.append(vf.reshape(SB, T, N_HEADS, HEAD_W))
        xp, xs = _ffn(xp, xs, mods, norm_g4, ffn_in_b, ffn_out_b, l, 1)
    return (xp, xs, jnp.stack(h_p), jnp.stack(h_s), jnp.stack(cv_p), jnp.stack(cv_s),
            jnp.stack(pl_p), jnp.stack(pl_s), jnp.stack(k_p), jnp.stack(v_p),
            jnp.stack(k_s), jnp.stack(v_s))
```

```python
import functools
import math

import jax
import jax.numpy as jnp
import numpy as np
from jax import lax
from jax.experimental import pallas as pl
from jax.experimental.pallas import tpu as pltpu

F32 = jnp.float32
BF16 = jnp.bfloat16

D_MODEL = 1024
DEPTH = 4
D_FF = 2816
D_RNN = 1280
LRU_BLOCKS = 16
LRU_BS = D_RNN // LRU_BLOCKS
LRU_SUPER = 2
CONV_W = 4
LRU_C = 8.0
POOL_WINDOWS = (2, 4, 8, 16)
POOL_GW = D_MODEL // len(POOL_WINDOWS)
POOL_HIST = 15
N_HEADS = 8
HEAD_DIM = D_MODEL // (2 * N_HEADS)
HEAD_W = 2 * HEAD_DIM
NUM_BUCKETS = 32
MAX_DISTANCE = 128
CHUNK = 64
EPS = 1e-6
NEG_INF = -1e30

ADA_ROWS = 16
ADA_PROMPT_ROW0 = 8

VMEM_LIMIT_BYTES = 60 * 1024 * 1024
SUBLANES = 8
LANES = 128

FFN_ROWS = 512
FFN_CHUNK = 256
LRU_ROWS = 256
LRU_PERM_ROWS = 256
POOL_ROWS = 512
QKV_ROWS = 512
ATT_TQ = 1024
ATT_TK = 512
BIAS_TILE_ROWS = 128
ATT_ROW_CHUNK = 256
VT_ROWS = HEAD_W + 16
LOG2E = math.log2(math.e)
OUT_ROWS = 512


def _cparams():
    return pltpu.CompilerParams(
        dimension_semantics=("arbitrary", "arbitrary"),
        vmem_limit_bytes=VMEM_LIMIT_BYTES,
    )


def _sigmoid(x):
    return 0.5 * jnp.tanh(0.5 * x) + 0.5


def _modulate(x, g, shift, scale):
    ms = jnp.mean(x * x, axis=-1, keepdims=True)
    y = x * lax.rsqrt(ms + EPS)
    return (y * g) * (1.0 + scale) + shift


def _dot(a, b):
    return jnp.dot(a, b, preferred_element_type=F32)


def _dot_nt(a, b):
    return lax.dot_general(a, b, (((1,), (1,)), ((), ())), preferred_element_type=F32)


def _adaln_kernel(c_ref, w_ref, b_ref, o_ref):
    c = c_ref[...]
    s = c * _sigmoid(c)
    y = _dot(s.astype(BF16), w_ref[0].astype(BF16))
    o_ref[0, 0] = y + b_ref[0, 0]


def _adaln(c_all, ada_w, ada_b):
    n_vec = 9
    out = pl.pallas_call(
        _adaln_kernel,
        grid=(DEPTH, n_vec),
        in_specs=[
            pl.BlockSpec((ADA_ROWS, D_MODEL), lambda l, k: (0, 0)),
            pl.BlockSpec((1, D_MODEL, D_MODEL), lambda l, k: (l, 0, k)),
            pl.BlockSpec((1, 1, 1, D_MODEL), lambda l, k: (l, k, 0, 0)),
        ],
        out_specs=pl.BlockSpec((1, 1, ADA_ROWS, D_MODEL), lambda l, k: (l, k, 0, 0)),
        out_shape=jax.ShapeDtypeStruct((DEPTH, n_vec, ADA_ROWS, D_MODEL), F32),
        compiler_params=_cparams(),
        name="adaln",
    )(c_all, ada_w, ada_b.reshape(DEPTH, n_vec, 1, D_MODEL))
    return out.reshape(DEPTH, n_vec, ADA_ROWS, 1, D_MODEL)


class _Rows:
    def __init__(self, batch, seq, rows, prompt, seqs_per_block=1):
        self.prompt = prompt
        if prompt:
            assert seq % rows == 0 and batch % seqs_per_block == 0
            assert ADA_PROMPT_ROW0 % seqs_per_block == 0
            self.nb, self.r = seqs_per_block, rows
            self.grid = (batch // seqs_per_block, seq // rows)
        else:
            self.nb, self.r = batch, seq
            self.grid = (1, 1)

    def act(self, width):
        return pl.BlockSpec((self.nb, self.r, width), lambda b, t: (b, t, 0))

    def per_seq(self, rows, width):
        return pl.BlockSpec((self.nb, rows, width), lambda b, t: (b, 0, 0))

    def mod(self, layer, k):
        if self.prompt:
            first = ADA_PROMPT_ROW0 // self.nb
            return pl.BlockSpec((1, 1, self.nb, 1, D_MODEL), lambda b, t: (layer, k, first + b, 0, 0))
        return pl.BlockSpec((1, 1, self.nb, 1, D_MODEL), lambda b, t: (layer, k, 0, 0, 0))


def _const_spec(shape, index, single_buffer=False):
    if single_buffer:
        return pl.BlockSpec(shape, lambda b, t: index, pipeline_mode=pl.Buffered(1))
    return pl.BlockSpec(shape, lambda b, t: index)


def _ffn_rows(x_ref, sh_ref, sc_ref, gt_ref, g_ref, win_b, wout_b, o_ref, before_chunk=None):
    x = x_ref[...]
    nb, r, _ = x.shape
    u = _modulate(x, g_ref[0, 0], sh_ref[0, 0], sc_ref[0, 0])
    ub = u.reshape(nb * r, D_MODEL).astype(BF16)
    acc = None
    for c in range(D_FF // FFN_CHUNK):
        if before_chunk is not None:
            before_chunk(c)
        lo = c * FFN_CHUNK
        a = _dot(ub, win_b[:, lo:lo + FFN_CHUNK])
        b = _dot(ub, win_b[:, D_FF + lo:D_FF + lo + FFN_CHUNK])
        h = ((a * _sigmoid(a)) * b).astype(BF16)
        y = _dot(h, wout_b[lo:lo + FFN_CHUNK, :])
        acc = y if acc is None else acc + y
    o_ref[...] = x + (0.5 * gt_ref[0, 0]) * acc.reshape(nb, r, D_MODEL)


def _ffn_kernel(xp_ref, xs_ref, shp_ref, scp_ref, gtp_ref, shs_ref, scs_ref, gts_ref, g_ref,
                win_hbm, wout_hbm, op_ref, os_ref, win_b, wout_b, gate_stage, up_stage, down_stage,
                sems, *, n_prompt_tiles, layer, which):
    i = pl.program_id(0)
    n_chunks = D_FF // FFN_CHUNK

    def chunk_copies(c, slot):
        lo = c * FFN_CHUNK
        return (
            pltpu.make_async_copy(win_hbm.at[layer, which, :, pl.ds(lo, FFN_CHUNK)],
                                  gate_stage.at[slot], sems.at[0, slot]),
            pltpu.make_async_copy(win_hbm.at[layer, which, :, pl.ds(D_FF + lo, FFN_CHUNK)],
                                  up_stage.at[slot], sems.at[1, slot]),
            pltpu.make_async_copy(wout_hbm.at[layer, which, pl.ds(lo, FFN_CHUNK), :],
                                  down_stage.at[slot], sems.at[2, slot]),
        )

    def fetch_chunk(c):
        slot = c % 2
        if c + 1 < n_chunks:
            for copy in chunk_copies(c + 1, 1 - slot):
                copy.start()
        for copy in chunk_copies(c, slot):
            copy.wait()
        lo = c * FFN_CHUNK
        win_b[:, lo:lo + FFN_CHUNK] = gate_stage[slot].astype(BF16)
        win_b[:, D_FF + lo:D_FF + lo + FFN_CHUNK] = up_stage[slot].astype(BF16)
        wout_b[lo:lo + FFN_CHUNK, :] = down_stage[slot].astype(BF16)

    @pl.when(i == 0)
    def _():
        for copy in chunk_copies(0, 0):
            copy.start()
        _ffn_rows(xp_ref, shp_ref, scp_ref, gtp_ref, g_ref, win_b, wout_b, op_ref, fetch_chunk)

    @pl.when((i > 0) & (i < n_prompt_tiles))
    def _():
        _ffn_rows(xp_ref, shp_ref, scp_ref, gtp_ref, g_ref, win_b, wout_b, op_ref)

    @pl.when(i == n_prompt_tiles)
    def _():
        _ffn_rows(xs_ref, shs_ref, scs_ref, gts_ref, g_ref, win_b, wout_b, os_ref)


def _ffn(xp, xs, mods, norm_g4, w_in, w_out, layer, which):
    k0 = 0 if which == 0 else 6
    g_idx = 0 if which == 0 else 2
    B, S, _ = xp.shape
    SB, T, _ = xs.shape
    per_seq = S // FFN_ROWS
    n = B * per_seq
    assert n > 1
    tile = lambda i: jnp.minimum(i, n - 1)
    prompt_act = pl.BlockSpec((1, FFN_ROWS, D_MODEL), lambda i: (tile(i) // per_seq, tile(i) % per_seq, 0))
    sample_act = pl.BlockSpec((SB, T, D_MODEL), lambda i: (0, 0, 0))
    prompt_mod = lambda k: pl.BlockSpec(
        (1, 1, 1, 1, D_MODEL), lambda i: (layer, k, ADA_PROMPT_ROW0 + tile(i) // per_seq, 0, 0))
    sample_mod = lambda k: pl.BlockSpec((1, 1, SB, 1, D_MODEL), lambda i: (layer, k, 0, 0, 0))
    return pl.pallas_call(
        functools.partial(_ffn_kernel, n_prompt_tiles=n, layer=layer, which=which),
        grid=(n + 1,),
        in_specs=[
            prompt_act, sample_act,
            prompt_mod(k0), prompt_mod(k0 + 1), prompt_mod(k0 + 2),
            sample_mod(k0), sample_mod(k0 + 1), sample_mod(k0 + 2),
            pl.BlockSpec((1, 1, 1, D_MODEL), lambda i: (layer, g_idx, 0, 0)),
            pl.BlockSpec(memory_space=pl.ANY),
            pl.BlockSpec(memory_space=pl.ANY),
        ],
        out_specs=[prompt_act, sample_act],
        out_shape=[jax.ShapeDtypeStruct(xp.shape, F32), jax.ShapeDtypeStruct(xs.shape, F32)],
        scratch_shapes=[
            pltpu.VMEM((D_MODEL, 2 * D_FF), BF16), pltpu.VMEM((D_FF, D_MODEL), BF16),
            pltpu.VMEM((2, D_MODEL, FFN_CHUNK), F32), pltpu.VMEM((2, D_MODEL, FFN_CHUNK), F32),
            pltpu.VMEM((2, FFN_CHUNK, D_MODEL), F32),
            pltpu.SemaphoreType.DMA((3, 2)),
        ],
        compiler_params=pltpu.CompilerParams(
            dimension_semantics=("arbitrary",), vmem_limit_bytes=VMEM_LIMIT_BYTES),
        name="ffn",
    )(xp, xs, mods, mods, mods, mods, mods, mods, norm_g4, w_in, w_out)


def _group_scan(a, b):
    row = lax.broadcasted_iota(jnp.int32, a.shape, 1)
    d = 1
    while d < SUBLANES:
        keep = row >= d
        a_sh = jnp.where(keep, pltpu.roll(a, d, axis=1), 1.0)
        b_sh = jnp.where(keep, pltpu.roll(b, d, axis=1), 0.0)
        b = a * b_sh + b
        a = a * a_sh
        d *= 2
    return a, b


def _linear_scan(a, b, h_prev, a_ref, h_ref):
    nb, r, C = a.shape
    first = slice(0, SUBLANES)
    a_run, h_run = a[:, first], b[:, first]
    a_ref[:, first, :] = a_run
    h_ref[:, first, :] = h_run
    for g0 in range(SUBLANES, r, SUBLANES):
        rows = slice(g0, g0 + SUBLANES)
        h_run = a[:, rows] * h_run + b[:, rows]
        a_run = a[:, rows] * a_run
        a_ref[:, rows, :] = a_run
        h_ref[:, rows, :] = h_run
    a_inc, h_inc = _group_scan(a_run, h_run)
    seg_end = a_inc * h_prev + h_inc
    row = lax.broadcasted_iota(jnp.int32, seg_end.shape, 1)
    seg_in = jnp.where(row == 0, h_prev, pltpu.roll(seg_end, 1, axis=1))
    for g0 in range(0, r, SUBLANES):
        rows = slice(g0, g0 + SUBLANES)
        h_ref[:, rows, :] = h_ref[:, rows, :] + a_ref[:, rows, :] * seg_in
    return seg_end[:, SUBLANES - 1:, :]


LRU_TAIL = (CONV_W - 1) * SUBLANES


def _lru_kernel(x_ref, sh_ref, sc_ref, gt_ref, g_ref, win_ref, cw_ref, cb_ref, gw_ref, gab_ref,
                gxb_ref, lam_ref, wout_ref, conv0_ref, h0_ref, perm_ref, unperm_ref,
                o_ref, hlast_ref, tail_ref, h_scr, tail_scr, arun_scr, hall_scr):
    @pl.when(pl.program_id(1) == 0)
    def _():
        h_scr[...] = h0_ref[...]
        tail_scr[...] = conv0_ref[...]

    x = x_ref[...]
    nb, r, _ = x.shape
    assert r // SUBLANES >= CONV_W - 1
    C = D_RNN
    u = _modulate(x, g_ref[0, 0], sh_ref[0, 0], sc_ref[0, 0])
    n_perm = perm_ref.shape[0]

    def permute(p_ref, v):
        parts = [_dot(p_ref[...], v[lo:lo + n_perm]) for lo in range(0, nb * r, n_perm)]
        return jnp.concatenate(parts, axis=0).astype(BF16)

    ub = permute(perm_ref, u.reshape(nb * r, D_MODEL).astype(BF16))
    gate_br = _dot(ub, win_ref[0, :, :C])
    x_br = _dot(ub, win_ref[0, :, C:]).reshape(nb, r, C)

    last = x_br[:, r - LRU_TAIL:, :]
    groups = (nb * (CONV_W - 1), SUBLANES, C)
    row = lax.broadcasted_iota(jnp.int32, groups, 1)
    wrap = jnp.where(row == 0, pltpu.roll(tail_scr[...].reshape(groups), 1, axis=1),
                     pltpu.roll(last.reshape(groups), 1, axis=1)).reshape(nb, LRU_TAIL, C)
    tail_scr[...] = last
    tail_ref[...] = last
    cw = cw_ref[0]
    xc = cb_ref[0] + cw[CONV_W - 1:CONV_W] * x_br
    for d in range(1, CONV_W):
        shifted = jnp.concatenate([wrap[:, LRU_TAIL - d * SUBLANES:], x_br[:, :r - d * SUBLANES]], axis=1)
        xc = xc + cw[CONV_W - 1 - d:CONV_W - d] * shifted

    xc2 = xc.reshape(nb * r, C)
    xcb = xc2.astype(BF16)
    sw = C // LRU_SUPER
    gates = [_dot(xcb[:, s * sw:(s + 1) * sw], gw_ref[0, s]) for s in range(LRU_SUPER)]
    rg = _sigmoid(jnp.concatenate([g[:, :sw] for g in gates], axis=-1) + gab_ref[0])
    ig = _sigmoid(jnp.concatenate([g[:, sw:] for g in gates], axis=-1) + gxb_ref[0])
    nl = -lam_ref[0]
    softplus = jnp.maximum(nl, 0.0) + jnp.log1p(jnp.exp(-jnp.abs(nl)))
    log_a = (-LRU_C * rg) * softplus
    a = jnp.exp(log_a)
    th = jnp.tanh(log_a)
    one_minus_a2 = (-2.0 * th) / (1.0 - th)
    root = jnp.where(one_minus_a2 > 0.0, one_minus_a2 * lax.rsqrt(one_minus_a2), 0.0)
    b_in = root * (ig * xc2)

    h_last = _linear_scan(a.reshape(nb, r, C), b_in.reshape(nb, r, C), h_scr[...], arun_scr, hall_scr)
    h_scr[...] = h_last
    hlast_ref[...] = h_last

    gb = gate_br
    gelu = 0.5 * gb * (1.0 + jnp.tanh(math.sqrt(2.0 / math.pi) * (gb + 0.044715 * (gb * gb * gb))))
    z = (gelu * hall_scr[...].reshape(nb * r, C)).astype(BF16)
    y = _dot(permute(unperm_ref, z), wout_ref[0])
    o_ref[...] = x + gt_ref[0, 0] * y.reshape(nb, r, D_MODEL)


def _lru(x, rows, mods, norm_g4, layer, j, w_in, conv_w, conv_b, gate_w, ga_b, gx_b, lam, w_out,
         conv0, h0):
    nbt = x.shape[0]
    C = D_RNN
    seqs = max(1, LRU_PERM_ROWS // rows.r)
    assert rows.nb % seqs == 0
    n_rows, G = seqs * rows.r, rows.r // SUBLANES
    b_idx, g_idx, i_idx = np.meshgrid(np.arange(seqs), np.arange(G), np.arange(SUBLANES), indexing='ij')
    source = (b_idx * rows.r + i_idx * G + g_idx).reshape(n_rows)
    perm = np.zeros((n_rows, n_rows), np.float32)
    perm[np.arange(n_rows), source] = 1.0
    vec = lambda: _const_spec((1, 1, C), (j, 0, 0))
    out, h_last, tail = pl.pallas_call(
        _lru_kernel,
        grid=rows.grid,
        in_specs=[
            rows.act(D_MODEL),
            rows.mod(layer, 3), rows.mod(layer, 4), rows.mod(layer, 5),
            _const_spec((1, 1, 1, D_MODEL), (layer, 1, 0, 0)),
            _const_spec((1, D_MODEL, 2 * C), (j, 0, 0)),
            _const_spec((1, CONV_W, C), (j, 0, 0)),
            vec(),
            _const_spec((1, LRU_SUPER, C // LRU_SUPER, 2 * C // LRU_SUPER), (j, 0, 0, 0)),
            vec(), vec(), vec(),
            _const_spec((1, C, D_MODEL), (j, 0, 0)),
            rows.per_seq(LRU_TAIL, C),
            rows.per_seq(1, C),
            _const_spec((n_rows, n_rows), (0, 0)),
            _const_spec((n_rows, n_rows), (0, 0)),
        ],
        out_specs=[rows.act(D_MODEL), rows.per_seq(1, C), rows.per_seq(LRU_TAIL, C)],
        out_shape=[
            jax.ShapeDtypeStruct(x.shape, F32),
            jax.ShapeDtypeStruct((nbt, 1, C), F32),
            jax.ShapeDtypeStruct((nbt, LRU_TAIL, C), F32),
        ],
        scratch_shapes=[pltpu.VMEM((rows.nb, 1, C), F32), pltpu.VMEM((rows.nb, LRU_TAIL, C), F32),
                        pltpu.VMEM((rows.nb, rows.r, C), F32), pltpu.VMEM((rows.nb, rows.r, C), F32)],
        compiler_params=_cparams(),
        name="lru",
    )(x, mods, mods, mods, norm_g4, w_in, conv_w, conv_b, gate_w, ga_b, gx_b, lam, w_out, conv0, h0,
      jnp.asarray(perm, BF16), jnp.asarray(perm.T, BF16))
    return out, h_last[:, 0], tail[:, SUBLANES - 1::SUBLANES]


POOL_PAD = 16


def _pool_kernel(x_ref, sh_ref, sc_ref, gt_ref, g_ref, hist0_ref, w_ref, b_ref, ps_ref,
                 o_ref, st_ref, hist_scr, *, t_base):
    t = pl.program_id(1)

    @pl.when(t == 0)
    def _():
        hist_scr[...] = hist0_ref[...]

    x = x_ref[...]
    nb, r, _ = x.shape
    u = _modulate(x, g_ref[0, 0], sh_ref[0, 0], sc_ref[0, 0])
    ext = jnp.concatenate([hist_scr[...], u], axis=1)
    new_hist = u[:, r - POOL_PAD:, :]
    hist_scr[...] = new_hist
    st_ref[...] = new_hist

    row = lax.broadcasted_iota(jnp.int32, (1, r, 1), 1)
    t_glob = t_base + t * r + row
    gate = gt_ref[0, 0]
    s = ext
    for g, wnd in enumerate(POOL_WINDOWS):
        lo = g * POOL_GW
        s = s[:, :, (POOL_GW if g > 0 else 0):]
        s = s + pltpu.roll(s, wnd // 2, axis=1)
        cnt = jnp.minimum(t_glob + 1, wnd).astype(F32)
        mean = s[:, POOL_PAD:, :POOL_GW] / cnt
        d = mean - u[:, :, lo:lo + POOL_GW]
        y = _dot(d.reshape(nb * r, POOL_GW).astype(BF16), w_ref[0, g]).reshape(nb, r, POOL_GW)
        y = (y + b_ref[0, :, lo:lo + POOL_GW]) * ps_ref[0, :, lo:lo + POOL_GW]
        o_ref[:, :, lo:lo + POOL_GW] = x[:, :, lo:lo + POOL_GW] + gate[:, :, lo:lo + POOL_GW] * y


def _pool(x, rows, mods, norm_g4, layer, j, hist0, w, b, scale, t_base):
    nbt = x.shape[0]
    out, st = pl.pallas_call(
        functools.partial(_pool_kernel, t_base=t_base),
        grid=rows.grid,
        in_specs=[
            rows.act(D_MODEL),
            rows.mod(layer, 3), rows.mod(layer, 4), rows.mod(layer, 5),
            _const_spec((1, 1, 1, D_MODEL), (layer, 1, 0, 0)),
            rows.per_seq(POOL_PAD, D_MODEL),
            _const_spec((1, len(POOL_WINDOWS), POOL_GW, POOL_GW), (j, 0, 0, 0)),
            _const_spec((1, 1, D_MODEL), (j, 0, 0)),
            _const_spec((1, 1, D_MODEL), (j, 0, 0)),
        ],
        out_specs=[rows.act(D_MODEL), rows.per_seq(POOL_PAD, D_MODEL)],
        out_shape=[jax.ShapeDtypeStruct(x.shape, F32),
                   jax.ShapeDtypeStruct((nbt, POOL_PAD, D_MODEL), F32)],
        scratch_shapes=[pltpu.VMEM((rows.nb, POOL_PAD, D_MODEL), F32)],
        compiler_params=_cparams(),
        name="pool",
    )(x, mods, mods, mods, norm_g4, hist0, w, b, scale)
    return out, st[:, POOL_PAD - POOL_HIST:]


def _qkv_kernel(x_ref, sh_ref, sc_ref, g_ref, w_ref, qg_ref, kg_ref, seg_ref,
                q_ref, kf_ref, kb_ref, vf_ref, vb_ref, *, v_transposed):
    x = x_ref[...]
    nb, r, _ = x.shape
    u = _modulate(x, g_ref[0, 0], sh_ref[0, 0], sc_ref[0, 0])
    ub = u.reshape(nb * r, D_MODEL).astype(BF16)
    q = _dot(ub, w_ref[0, :, :D_MODEL])
    k = _dot(ub, w_ref[0, :, D_MODEL:2 * D_MODEL])
    v = _dot(ub, w_ref[0, :, 2 * D_MODEL:])

    def head_norm(z, gain):
        ms = _dot((z * z).astype(BF16), seg_ref[...])
        return (z * lax.rsqrt(ms + EPS)) * gain

    qn = head_norm(q, qg_ref[...]) * (HEAD_DIM ** -0.5 * LOG2E)
    kn = head_norm(k, kg_ref[...])
    lane = lax.broadcasted_iota(jnp.int32, (1, D_MODEL), 1)
    first = (lane % HEAD_W) < HEAD_DIM
    q_ref[:, 0] = jnp.where(first, qn, 0.0).astype(BF16).reshape(nb, r, D_MODEL)
    q_ref[:, 1] = jnp.where(first, 0.0, qn).astype(BF16).reshape(nb, r, D_MODEL)
    kf_ref[...] = kn.reshape(nb, r, D_MODEL)
    kb_ref[...] = kn.astype(BF16).reshape(nb, r, D_MODEL)
    vf_ref[...] = v.reshape(nb, r, D_MODEL)
    if v_transposed:
        for h in range(N_HEADS):
            vb_ref[0, h, :HEAD_W] = v[:, h * HEAD_W:(h + 1) * HEAD_W].T.astype(BF16)
            vb_ref[0, h, HEAD_W:] = jnp.ones((VT_ROWS - HEAD_W, r), BF16)
    else:
        vb_ref[...] = v.astype(BF16).reshape(nb, r, D_MODEL)


def _qkv(x, rows, mods, norm_g4, layer, j, w_in, qg, kg, seg, v_transposed):
    nbt, seq, _ = x.shape
    q_spec = pl.BlockSpec((rows.nb, 2, rows.r, D_MODEL), lambda b, t: (b, 0, t, 0))
    if v_transposed:
        assert rows.nb == 1
        vb_spec = pl.BlockSpec((1, N_HEADS, VT_ROWS, rows.r), lambda b, t: (b, 0, 0, t))
        vb_shape = jax.ShapeDtypeStruct((nbt, N_HEADS, VT_ROWS, seq), BF16)
    else:
        vb_spec = rows.act(D_MODEL)
        vb_shape = jax.ShapeDtypeStruct(x.shape, BF16)
    return pl.pallas_call(
        functools.partial(_qkv_kernel, v_transposed=v_transposed),
        grid=rows.grid,
        in_specs=[
            rows.act(D_MODEL),
            rows.mod(layer, 3), rows.mod(layer, 4),
            _const_spec((1, 1, 1, D_MODEL), (layer, 1, 0, 0)),
            _const_spec((1, D_MODEL, 3 * D_MODEL), (j, 0, 0)),
            _const_spec((1, D_MODEL), (0, 0)),
            _const_spec((1, D_MODEL), (0, 0)),
            _const_spec((D_MODEL, D_MODEL), (0, 0)),
        ],
        out_specs=[q_spec, rows.act(D_MODEL), rows.act(D_MODEL), rows.act(D_MODEL), vb_spec],
        out_shape=[
            jax.ShapeDtypeStruct((nbt, 2, seq, D_MODEL), BF16),
            jax.ShapeDtypeStruct(x.shape, F32),
            jax.ShapeDtypeStruct(x.shape, BF16),
            jax.ShapeDtypeStruct(x.shape, F32),
            vb_shape,
        ],
        compiler_params=_cparams(),
        name="qkv",
    )(x, mods, mods, norm_g4, w_in, qg, kg, seg)


_BUCKET_EDGES = (12, 16, 23, 32, 46, 64, 91)
_FAR_BUCKET = NUM_BUCKETS // 2 - 1


def _bucket_of(rel):
    n = abs(rel)
    small = NUM_BUCKETS // 4
    b = n if n < small else small + sum(n >= e for e in _BUCKET_EDGES)
    return b + (NUM_BUCKETS // 2 if rel > 0 else 0)


_BUCKET_RUNS = tuple((r, _bucket_of(r)) for r in range(-_BUCKET_EDGES[-1] + 1, _BUCKET_EDGES[-1] + 1)
                     if _bucket_of(r) != _bucket_of(r - 1))


def _bias_kernel(rb_ref, o_ref, *, rel0, mask_from, keys_on_rows, tile, tiles):
    h = pl.program_id(0)
    R, W = tile
    for row0, col0, fill in tiles:
        window = (0, slice(row0, row0 + R), slice(col0, col0 + W))
        if fill is not None:
            o_ref[window] = jnp.full((R, W), fill, F32)
            continue
        i = lax.broadcasted_iota(jnp.int32, (R, W), 0) + row0
        jj = lax.broadcasted_iota(jnp.int32, (R, W), 1) + col0
        key, qry = (i, jj) if keys_on_rows else (jj, i)
        rel = key - qry + rel0
        out = jnp.full((R, W), rb_ref[_FAR_BUCKET, h], F32)
        for first_rel, bucket in _BUCKET_RUNS:
            out = jnp.where(rel >= first_rel, rb_ref[bucket, h], out)
        out = (out - rb_ref[_FAR_BUCKET, h]) * LOG2E
        if mask_from is not None:
            visible = ((key - mask_from) // CHUNK <= qry // CHUNK) | (key < mask_from)
            out = jnp.where(visible, out, NEG_INF)
        o_ref[window] = out


def _bias_tile(rel_bias, rows, width, rel0, mask_from, keys_on_rows=False):
    R = BIAS_TILE_ROWS if rows % BIAS_TILE_ROWS == 0 else rows
    W = LANES if width % LANES == 0 else width
    tiles = []
    for row0 in range(0, rows, R):
        for col0 in range(0, width, W):
            key0, qry0 = (row0, col0) if keys_on_rows else (col0, row0)
            key1, qry1 = (row0 + R, col0 + W) if keys_on_rows else (col0 + W, row0 + R)
            fill = None
            unmasked = mask_from is None or key1 <= mask_from
            if unmasked and (key1 - 1) - qry0 + rel0 <= -_BUCKET_EDGES[-1]:
                fill = 0.0
            if mask_from is not None and key0 >= mask_from and \
                    (key0 - mask_from) // CHUNK > (qry1 - 1) // CHUNK:
                fill = NEG_INF
            tiles.append((row0, col0, fill))
    return pl.pallas_call(
        functools.partial(_bias_kernel, rel0=rel0, mask_from=mask_from, keys_on_rows=keys_on_rows,
                          tile=(R, W), tiles=tuple(tiles)),
        grid=(N_HEADS,),
        in_specs=[pl.BlockSpec(memory_space=pltpu.SMEM)],
        out_specs=pl.BlockSpec((1, rows, width), lambda h: (h, 0, 0)),
        out_shape=jax.ShapeDtypeStruct((N_HEADS, rows, width), F32),
        name="bias_tile",
    )(rel_bias)


def _diff_lambda(lp, lam_init):
    s1 = jnp.sum(lp[0:1] * lp[1:2], axis=-1, keepdims=True)
    s2 = jnp.sum(lp[2:3] * lp[3:4], axis=-1, keepdims=True)
    return jnp.exp(s1) - jnp.exp(s2) + lam_init


def _attn_finish(acc, l, nq, lam, subg, lam_init):
    o = acc / l
    o = o[:nq] - lam * o[nq:]
    ms = jnp.mean(o * o, axis=-1, keepdims=True)
    return ((o * lax.rsqrt(ms + EPS)) * subg) * (1.0 - lam_init)


def _attn_prompt_kernel(q_ref, k_ref, vt_ref, bias_ref, lam_ref, subg_ref, o_ref,
                        qt_scr, s0_scr, s1_scr, mx0_scr, mx1_scr, m_scr, acc_scr, *, lam_init):
    qi = pl.program_id(2)
    tq, tk = ATT_TQ, ATT_TK
    width = 2 * tq
    n_blocks = (qi + 1) * (tq // tk)
    qt_scr[:, :tq] = q_ref[0, 0].astype(F32).T.astype(BF16)
    qt_scr[:, tq:] = q_ref[0, 1].astype(F32).T.astype(BF16)
    m_scr[...] = jnp.full(m_scr.shape, NEG_INF, F32)
    acc_scr[...] = jnp.zeros(acc_scr.shape, F32)

    def block_start(c):
        return c * tk

    def scores_rows(c, s_ref, near, r0):
        rows = slice(r0, r0 + ATT_ROW_CHUNK)
        start = pl.multiple_of(block_start(c) + r0, ATT_ROW_CHUNK)
        s = _dot(k_ref[0, pl.ds(start, ATT_ROW_CHUNK), :], qt_scr[...])
        if near is not None:
            bias = bias_ref[0, near, rows, :]
            s = s + jnp.concatenate([bias, bias], axis=1)
        s_ref[rows, :] = s
        return jnp.max(s.reshape(ATT_ROW_CHUNK // SUBLANES, SUBLANES, width), axis=0)

    def new_max(mx_ref):
        m_old = m_scr[...]
        m_new = jnp.maximum(m_old, jnp.max(mx_ref[...], axis=0, keepdims=True))
        m_scr[...] = m_new
        return m_new, jnp.exp2(m_old - m_new)

    def absorb_rows(c, s_ref, m_new, r0):
        p = jnp.exp2((s_ref[r0:r0 + ATT_ROW_CHUNK, :] - m_new).astype(BF16))
        start = pl.multiple_of(block_start(c) + r0, ATT_ROW_CHUNK)
        return _dot(vt_ref[0, 0, :, pl.ds(start, ATT_ROW_CHUNK)], p)

    def rescale_add(alpha, parts):
        acc_scr[...] = alpha * acc_scr[...] + functools.reduce(jnp.add, parts)

    chunks = range(0, tk, ATT_ROW_CHUNK)

    def scores(c, s_ref, mx_ref, near):
        mx_ref[...] = functools.reduce(jnp.maximum, [scores_rows(c, s_ref, near, r0) for r0 in chunks])

    def absorb(c, s_ref, mx_ref):
        m_new, alpha = new_max(mx_ref)
        rescale_add(alpha, [absorb_rows(c, s_ref, m_new, r0) for r0 in chunks])

    def overlap(c_new, s_new, mx_new, near, c_old, s_old, mx_old):
        scores(c_new, s_new, mx_new, near)
        absorb(c_old, s_old, mx_old)

    @pl.when(qi == 0)
    def _():
        scores(0, s0_scr, mx0_scr, 1)

    @pl.when(qi > 0)
    def _():
        scores(0, s0_scr, mx0_scr, None)

    def far_trip(t, carry):
        c = 2 * t
        overlap(c + 1, s1_scr, mx1_scr, None, c, s0_scr, mx0_scr)
        overlap(c + 2, s0_scr, mx0_scr, None, c + 1, s1_scr, mx1_scr)
        return carry

    lax.fori_loop(0, jnp.maximum(n_blocks // 2 - 2, 0), far_trip, 0)

    @pl.when(qi > 0)
    def _():
        c = n_blocks - 4
        overlap(c + 1, s1_scr, mx1_scr, 0, c, s0_scr, mx0_scr)
        overlap(c + 2, s0_scr, mx0_scr, 1, c + 1, s1_scr, mx1_scr)

    overlap(n_blocks - 1, s1_scr, mx1_scr, 2, n_blocks - 2, s0_scr, mx0_scr)
    absorb(n_blocks - 1, s1_scr, mx1_scr)

    lam = _diff_lambda(lam_ref[0], lam_init)
    acc = acc_scr[...]
    o = acc[:HEAD_W] / acc[HEAD_W:HEAD_W + 1]
    o = o[:, :tq] - lam * o[:, tq:]
    ms = jnp.mean(o * o, axis=0, keepdims=True)
    on = ((o * lax.rsqrt(ms + EPS)) * subg_ref[...]) * (1.0 - lam_init)
    o_ref[0] = on.T.astype(BF16)


def _attn_prompt(q, kb, vt, bias, lam_p, subg_col, j, lam_init):
    nbt, _, seq, _ = q.shape
    tq, tk = ATT_TQ, ATT_TK
    return pl.pallas_call(
        functools.partial(_attn_prompt_kernel, lam_init=lam_init),
        grid=(nbt, N_HEADS, seq // tq),
        in_specs=[
            pl.BlockSpec((1, 2, tq, HEAD_W), lambda b, h, i: (b, 0, i, h)),
            pl.BlockSpec((1, seq, HEAD_W), lambda b, h, i: (b, 0, h)),
            pl.BlockSpec((1, 1, VT_ROWS, seq), lambda b, h, i: (b, h, 0, 0)),
            pl.BlockSpec((1, 3, tk, tq), lambda b, h, i: (h, 0, 0, 0)),
            pl.BlockSpec((1, 4, HEAD_DIM), lambda b, h, i: (j, 0, 0)),
            pl.BlockSpec((HEAD_W, 1), lambda b, h, i: (0, 0)),
        ],
        out_specs=pl.BlockSpec((1, tq, HEAD_W), lambda b, h, i: (b, i, h)),
        out_shape=jax.ShapeDtypeStruct((nbt, seq, D_MODEL), BF16),
        scratch_shapes=[pltpu.VMEM((HEAD_W, 2 * tq), BF16),
                        pltpu.VMEM((tk, 2 * tq), F32), pltpu.VMEM((tk, 2 * tq), F32),
                        pltpu.VMEM((SUBLANES, 2 * tq), F32), pltpu.VMEM((SUBLANES, 2 * tq), F32),
                        pltpu.VMEM((1, 2 * tq), F32),
                        pltpu.VMEM((VT_ROWS, 2 * tq), F32)],
        compiler_params=pltpu.CompilerParams(
            dimension_semantics=("arbitrary", "arbitrary", "arbitrary"),
            vmem_limit_bytes=VMEM_LIMIT_BYTES),
        name="attn_prompt",
    )(q, kb, vt, bias, lam_p, subg_col)


def _attn_sample_kernel(q_ref, ck_ref, cv_ref, kn_ref, vn_ref, bc_ref, bn_ref, lam_ref, subg_ref,
                        o_ref, *, lam_init):
    nq = q_ref.shape[2]
    lam = _diff_lambda(lam_ref[0], lam_init)
    for h in range(N_HEADS):
        cols = slice(h * HEAD_W, (h + 1) * HEAD_W)
        qq = q_ref[0, :, :, cols].reshape(2 * nq, HEAD_W)
        kc = ck_ref[0, :, cols].astype(BF16)
        vc = cv_ref[0, :, cols].astype(BF16)
        kn = kn_ref[0, :, cols]
        vn = vn_ref[0, :, cols]
        past = kc.shape[0]
        s_c = (_dot_nt(qq, kc).reshape(2, nq, past) + bc_ref[h][None]).reshape(2 * nq, past)
        s_n = (_dot_nt(qq, kn).reshape(2, nq, nq) + bn_ref[h][None]).reshape(2 * nq, nq)
        m = jnp.maximum(jnp.max(s_c, axis=-1, keepdims=True), jnp.max(s_n, axis=-1, keepdims=True))
        p_c = jnp.exp2(s_c - m)
        p_n = jnp.exp2(s_n - m)
        l = jnp.sum(p_c, axis=-1, keepdims=True) + jnp.sum(p_n, axis=-1, keepdims=True)
        acc = _dot(p_c.astype(BF16), vc) + _dot(p_n.astype(BF16), vn)
        o_ref[0, :, cols] = _attn_finish(acc, l, nq, lam, subg_ref[...], lam_init).astype(BF16)


def _attn_sample(q, ck, cv, kb, vb, bias_c, bias_n, lam_p, subg, j, lam_init):
    nbt, _, nq, _ = q.shape
    past = ck.shape[1]
    return pl.pallas_call(
        functools.partial(_attn_sample_kernel, lam_init=lam_init),
        grid=(nbt,),
        in_specs=[
            pl.BlockSpec((1, 2, nq, D_MODEL), lambda b: (b, 0, 0, 0)),
            pl.BlockSpec((1, past, D_MODEL), lambda b: (b, 0, 0)),
            pl.BlockSpec((1, past, D_MODEL), lambda b: (b, 0, 0)),
            pl.BlockSpec((1, nq, D_MODEL), lambda b: (b, 0, 0)),
            pl.BlockSpec((1, nq, D_MODEL), lambda b: (b, 0, 0)),
            pl.BlockSpec((N_HEADS, nq, past), lambda b: (0, 0, 0)),
            pl.BlockSpec((N_HEADS, nq, nq), lambda b: (0, 0, 0)),
            pl.BlockSpec((1, 4, HEAD_DIM), lambda b: (j, 0, 0)),
            pl.BlockSpec((1, HEAD_W), lambda b: (0, 0)),
        ],
        out_specs=pl.BlockSpec((1, nq, D_MODEL), lambda b: (b, 0, 0)),
        out_shape=jax.ShapeDtypeStruct((nbt, nq, D_MODEL), BF16),
        compiler_params=pltpu.CompilerParams(
            dimension_semantics=("arbitrary",), vmem_limit_bytes=VMEM_LIMIT_BYTES),
        name="attn_sample",
    )(q, ck, cv, kb, vb, bias_c, bias_n, lam_p, subg)


def _proj_res_kernel(x_ref, a_ref, gt_ref, w_ref, o_ref):
    x = x_ref[...]
    nb, r, _ = x.shape
    a = a_ref[...].reshape(nb * r, a_ref.shape[-1])
    y = _dot(a, w_ref[0])
    o_ref[...] = x + gt_ref[0, 0] * y.reshape(nb, r, D_MODEL)


def _proj_res(x, a, rows, mods, layer, j, w):
    return pl.pallas_call(
        _proj_res_kernel,
        grid=rows.grid,
        in_specs=[
            rows.act(D_MODEL), rows.act(a.shape[-1]), rows.mod(layer, 5),
            _const_spec((1,) + w.shape[1:], (j, 0, 0)),
        ],
        out_specs=rows.act(D_MODEL),
        out_shape=jax.ShapeDtypeStruct(x.shape, F32),
        compiler_params=_cparams(),
        name="proj_res",
    )(x, a, mods, w)


def kernel(x_prompt, x_sample, c_prompt, c_sample, state_lru_h, state_lru_conv, state_pool, cache_k, cache_v, ada_w, ada_b, norm_g, ffn_w_in, ffn_w_out, lru_w_in, lru_conv_w, lru_conv_b, lru_ga_w, lru_ga_b, lru_gx_w, lru_gx_b, lru_lambda, lru_w_out, pool_w, pool_b, pool_scale, attn_w_in, attn_q_g, attn_k_g, attn_lambda, attn_sub_g, attn_w_out, rel_bias):
    B, S, _ = x_prompt.shape
    SB, T, _ = x_sample.shape
    P = cache_k.shape[2]
    assert SB == ADA_PROMPT_ROW0 and ADA_PROMPT_ROW0 + B <= ADA_ROWS
    n_a, n_b, n_c = lru_w_in.shape[0], pool_w.shape[0], attn_w_in.shape[0]

    c_all = jnp.concatenate(
        [c_sample, c_prompt, jnp.zeros((ADA_ROWS - SB - B, D_MODEL), F32)], axis=0)
    mods = _adaln(c_all, ada_w, ada_b)
    norm_g4 = norm_g.reshape(DEPTH, 3, 1, D_MODEL)

    lru_in_b = lru_w_in.astype(BF16)
    lru_out_b = lru_w_out.astype(BF16)
    per_super = LRU_BLOCKS // LRU_SUPER
    eye = jnp.eye(per_super, dtype=F32)

    def super_blocks(w):
        w5 = w.reshape(w.shape[0], LRU_SUPER, per_super, LRU_BS, LRU_BS)
        full = jnp.einsum('nshij,hg->nshigj', w5, eye)
        return full.reshape(w.shape[0], LRU_SUPER, per_super * LRU_BS, per_super * LRU_BS)

    lru_gate_b = jnp.concatenate([super_blocks(lru_ga_w), super_blocks(lru_gx_w)], axis=-1).astype(BF16)
    vec3 = lambda v: v.reshape(v.shape[0], 1, v.shape[-1])
    pool_w_b = pool_w.astype(BF16)
    attn_in_b = attn_w_in.astype(BF16)
    attn_out_b = attn_w_out.astype(BF16)
    seg = jnp.kron(jnp.eye(D_MODEL // HEAD_DIM, dtype=F32),
                   jnp.full((HEAD_DIM, HEAD_DIM), 1.0 / HEAD_DIM, F32)).astype(BF16)

    rows_p = {r: _Rows(B, S, r, True) for r in {POOL_ROWS, QKV_ROWS, OUT_ROWS}}
    rows_lru = _Rows(B, S, LRU_ROWS, True, seqs_per_block=B)
    rows_s = _Rows(SB, T, T, False)

    xp, xs = x_prompt, x_sample
    h_p, h_s, cv_p, cv_s, pl_p, pl_s, k_p, v_p, k_s, v_s = ([] for _ in range(10))
    for l in range(DEPTH):
        kind, j = l % 3, l // 3
        xp, xs = _ffn(xp, xs, mods, norm_g4, ffn_w_in, ffn_w_out, l, 0)
        if kind == 0:
            prm = (lru_in_b, lru_conv_w, vec3(lru_conv_b), lru_gate_b, vec3(lru_ga_b),
                   vec3(lru_gx_b), vec3(lru_lambda), lru_out_b)
            xp, hp, bp = _lru(xp, rows_lru, mods, norm_g4, l, j, *prm,
                              jnp.zeros((B, LRU_TAIL, D_RNN), F32), jnp.zeros((B, 1, D_RNN), F32))
            conv0 = jnp.zeros((SB, LRU_TAIL, D_RNN), F32).at[:, SUBLANES - 1::SUBLANES].set(
                state_lru_conv[j])
            xs, hs, bs = _lru(xs, rows_s, mods, norm_g4, l, j, *prm, conv0,
                              state_lru_h[j][:, None, :])
            h_p.append(hp), h_s.append(hs), cv_p.append(bp), cv_s.append(bs)
        elif kind == 1:
            prm = (pool_w_b, vec3(pool_b), vec3(pool_scale))
            xp, sp = _pool(xp, rows_p[POOL_ROWS], mods, norm_g4, l, j,
                           jnp.zeros((B, POOL_PAD, D_MODEL), F32), *prm, 0)
            hist0 = jnp.pad(state_pool[j], ((0, 0), (POOL_PAD - POOL_HIST, 0), (0, 0)))
            xs, ss = _pool(xs, rows_s, mods, norm_g4, l, j, hist0, *prm, POOL_HIST)
            pl_p.append(sp), pl_s.append(ss)
        else:
            lam_init = 0.8 - 0.6 * math.exp(-0.3 * l)
            qg = jnp.tile(attn_q_g[j], D_MODEL // HEAD_DIM)[None]
            kg = jnp.tile(attn_k_g[j], D_MODEL // HEAD_DIM)[None]
            subg = attn_sub_g[j][None]
            q, kf, kb, vf, vt = _qkv(xp, rows_p[QKV_ROWS], mods, norm_g4, l, j, attn_in_b, qg, kg, seg,
                                     True)
            assert ATT_TQ == 2 * ATT_TK
            bias_p = _bias_tile(rel_bias, 3 * ATT_TK, ATT_TQ, -ATT_TK, ATT_TK, keys_on_rows=True)
            bias_p = bias_p.reshape(N_HEADS, 3, ATT_TK, ATT_TQ)
            o = _attn_prompt(q, kb, vt, bias_p, attn_lambda, attn_sub_g[j][:, None], j, lam_init)
            xp = _proj_res(xp, o, rows_p[OUT_ROWS], mods, l, j, attn_out_b)
            k_p.append(kf.reshape(B, S, N_HEADS, HEAD_W)), v_p.append(vf.reshape(B, S, N_HEADS, HEAD_W))
            q, kf, kb, vf, vb = _qkv(xs, rows_s, mods, norm_g4, l, j, attn_in_b, qg, kg, seg, False)
            bias_c = _bias_tile(rel_bias, T, P, -P, None)
            bias_n = _bias_tile(rel_bias, T, T, 0, None)
            o = _attn_sample(q, cache_k[j].reshape(SB, P, D_MODEL), cache_v[j].reshape(SB, P, D_MODEL),
                             kb, vb, bias_c, bias_n, attn_lambda, subg, j, lam_init)
            xs = _proj_res(xs, o, rows_s, mods, l, j, attn_out_b)
            k_s.append(kf.reshape(SB, T, N_HEADS, HEAD_W)), v_s.append(vf.reshape(SB, T, N_HEADS, HEAD_W))
        xp, xs = _ffn(xp, xs, mods, norm_g4, ffn_w_in, ffn_w_out, l, 1)
    return (xp, xs, jnp.stack(h_p), jnp.stack(h_s), jnp.stack(cv_p), jnp.stack(cv_s),
            jnp.stack(pl_p), jnp.stack(pl_s), jnp.stack(k_p), jnp.stack(v_p),
            jnp.stack(k_s), jnp.stack(v_s))
```

```python
import functools
import math

import jax
import jax.numpy as jnp
import numpy as np
from jax import lax
from jax.experimental import pallas as pl
from jax.experimental.pallas import tpu as pltpu

F32 = jnp.float32
BF16 = jnp.bfloat16

D_MODEL = 1024
DEPTH = 4
D_FF = 2816
D_RNN = 1280
LRU_BLOCKS = 16
LRU_BS = D_RNN // LRU_BLOCKS
LRU_SUPER = 2
CONV_W = 4
LRU_C = 8.0
POOL_WINDOWS = (2, 4, 8, 16)
POOL_GW = D_MODEL // len(POOL_WINDOWS)
POOL_HIST = 15
N_HEADS = 8
HEAD_DIM = D_MODEL // (2 * N_HEADS)
HEAD_W = 2 * HEAD_DIM
NUM_BUCKETS = 32
MAX_DISTANCE = 128
CHUNK = 64
EPS = 1e-6
NEG_INF = -1e30

ADA_ROWS = 16
ADA_PROMPT_ROW0 = 8

VMEM_LIMIT_BYTES = 60 * 1024 * 1024
SUBLANES = 8
LANES = 128

FFN_ROWS = 512
FFN_CHUNK = 256
LRU_ROWS = 256
LRU_PERM_ROWS = 256
POOL_ROWS = 1024
QKV_ROWS = 512
ATT_TQ = 1024
ATT_TK = 512
BIAS_TILE_ROWS = 128
ATT_ROW_CHUNK = 256
VT_ROWS = HEAD_W + 16
LOG2E = math.log2(math.e)
OUT_ROWS = 1024


def _cparams():
    return pltpu.CompilerParams(
        dimension_semantics=("arbitrary", "arbitrary"),
        vmem_limit_bytes=VMEM_LIMIT_BYTES,
    )


def _sigmoid(x):
    return 0.5 * jnp.tanh(0.5 * x) + 0.5


def _modulate(x, g, shift, scale):
    ms = jnp.mean(x * x, axis=-1, keepdims=True)
    y = x * lax.rsqrt(ms + EPS)
    return (y * g) * (1.0 + scale) + shift


def _dot(a, b):
    return jnp.dot(a, b, preferred_element_type=F32)


def _dot_nt(a, b):
    return lax.dot_general(a, b, (((1,), (1,)), ((), ())), preferred_element_type=F32)


def _adaln_kernel(c_ref, w_ref, b_ref, o_ref):
    c = c_ref[...]
    s = c * _sigmoid(c)
    y = _dot(s.astype(BF16), w_ref[0].astype(BF16))
    o_ref[0, 0] = y + b_ref[0, 0]


def _adaln(c_all, ada_w, ada_b):
    n_vec = 9
    out = pl.pallas_call(
        _adaln_kernel,
        grid=(DEPTH, n_vec),
        in_specs=[
            pl.BlockSpec((ADA_ROWS, D_MODEL), lambda l, k: (0, 0)),
            pl.BlockSpec((1, D_MODEL, D_MODEL), lambda l, k: (l, 0, k)),
            pl.BlockSpec((1, 1, 1, D_MODEL), lambda l, k: (l, k, 0, 0)),
        ],
        out_specs=pl.BlockSpec((1, 1, ADA_ROWS, D_MODEL), lambda l, k: (l, k, 0, 0)),
        out_shape=jax.ShapeDtypeStruct((DEPTH, n_vec, ADA_ROWS, D_MODEL), F32),
        compiler_params=_cparams(),
        name="adaln",
    )(c_all, ada_w, ada_b.reshape(DEPTH, n_vec, 1, D_MODEL))
    return out.reshape(DEPTH, n_vec, ADA_ROWS, 1, D_MODEL)


class _Rows:
    def __init__(self, batch, seq, rows, prompt, seqs_per_block=1):
        self.prompt = prompt
        if prompt:
            assert seq % rows == 0 and batch % seqs_per_block == 0
            assert ADA_PROMPT_ROW0 % seqs_per_block == 0
            self.nb, self.r = seqs_per_block, rows
            self.grid = (batch // seqs_per_block, seq // rows)
        else:
            self.nb, self.r = batch, seq
            self.grid = (1, 1)

    def act(self, width):
        return pl.BlockSpec((self.nb, self.r, width), lambda b, t: (b, t, 0))

    def per_seq(self, rows, width):
        return pl.BlockSpec((self.nb, rows, width), lambda b, t: (b, 0, 0))

    def mod(self, layer, k):
        if self.prompt:
            first = ADA_PROMPT_ROW0 // self.nb
            return pl.BlockSpec((1, 1, self.nb, 1, D_MODEL), lambda b, t: (layer, k, first + b, 0, 0))
        return pl.BlockSpec((1, 1, self.nb, 1, D_MODEL), lambda b, t: (layer, k, 0, 0, 0))


def _const_spec(shape, index, single_buffer=False):
    if single_buffer:
        return pl.BlockSpec(shape, lambda b, t: index, pipeline_mode=pl.Buffered(1))
    return pl.BlockSpec(shape, lambda b, t: index)


def _ffn_rows(x_ref, sh_ref, sc_ref, gt_ref, g_ref, win_b, wout_b, o_ref, before_chunk=None):
    x = x_ref[...]
    nb, r, _ = x.shape
    u = _modulate(x, g_ref[0, 0], sh_ref[0, 0], sc_ref[0, 0])
    ub = u.reshape(nb * r, D_MODEL).astype(BF16)
    acc = None
    for c in range(D_FF // FFN_CHUNK):
        if before_chunk is not None:
            before_chunk(c)
        lo = c * FFN_CHUNK
        a = _dot(ub, win_b[:, lo:lo + FFN_CHUNK])
        b = _dot(ub, win_b[:, D_FF + lo:D_FF + lo + FFN_CHUNK])
        h = ((a * _sigmoid(a)) * b).astype(BF16)
        y = _dot(h, wout_b[lo:lo + FFN_CHUNK, :])
        acc = y if acc is None else acc + y
    o_ref[...] = x + (0.5 * gt_ref[0, 0]) * acc.reshape(nb, r, D_MODEL)


def _ffn_kernel(xp_ref, xs_ref, shp_ref, scp_ref, gtp_ref, shs_ref, scs_ref, gts_ref, g_ref,
                win_hbm, wout_hbm, op_ref, os_ref, win_b, wout_b, gate_stage, up_stage, down_stage,
                sems, *, n_prompt_tiles, layer, which):
    i = pl.program_id(0)
    n_chunks = D_FF // FFN_CHUNK

    def chunk_copies(c, slot):
        lo = c * FFN_CHUNK
        return (
            pltpu.make_async_copy(win_hbm.at[layer, which, :, pl.ds(lo, FFN_CHUNK)],
                                  gate_stage.at[slot], sems.at[0, slot]),
            pltpu.make_async_copy(win_hbm.at[layer, which, :, pl.ds(D_FF + lo, FFN_CHUNK)],
                                  up_stage.at[slot], sems.at[1, slot]),
            pltpu.make_async_copy(wout_hbm.at[layer, which, pl.ds(lo, FFN_CHUNK), :],
                                  down_stage.at[slot], sems.at[2, slot]),
        )

    def fetch_chunk(c):
        slot = c % 2
        if c + 1 < n_chunks:
            for copy in chunk_copies(c + 1, 1 - slot):
                copy.start()
        for copy in chunk_copies(c, slot):
            copy.wait()
        lo = c * FFN_CHUNK
        win_b[:, lo:lo + FFN_CHUNK] = gate_stage[slot].astype(BF16)
        win_b[:, D_FF + lo:D_FF + lo + FFN_CHUNK] = up_stage[slot].astype(BF16)
        wout_b[lo:lo + FFN_CHUNK, :] = down_stage[slot].astype(BF16)

    @pl.when(i == 0)
    def _():
        for copy in chunk_copies(0, 0):
            copy.start()
        _ffn_rows(xp_ref, shp_ref, scp_ref, gtp_ref, g_ref, win_b, wout_b, op_ref, fetch_chunk)

    @pl.when((i > 0) & (i < n_prompt_tiles))
    def _():
        _ffn_rows(xp_ref, shp_ref, scp_ref, gtp_ref, g_ref, win_b, wout_b, op_ref)

    @pl.when(i == n_prompt_tiles)
    def _():
        _ffn_rows(xs_ref, shs_ref, scs_ref, gts_ref, g_ref, win_b, wout_b, os_ref)


def _ffn(xp, xs, mods, norm_g4, w_in, w_out, layer, which):
    k0 = 0 if which == 0 else 6
    g_idx = 0 if which == 0 else 2
    B, S, _ = xp.shape
    SB, T, _ = xs.shape
    per_seq = S // FFN_ROWS
    n = B * per_seq
    assert n > 1
    tile = lambda i: jnp.minimum(i, n - 1)
    prompt_act = pl.BlockSpec((1, FFN_ROWS, D_MODEL), lambda i: (tile(i) // per_seq, tile(i) % per_seq, 0))
    sample_act = pl.BlockSpec((SB, T, D_MODEL), lambda i: (0, 0, 0))
    prompt_mod = lambda k: pl.BlockSpec(
        (1, 1, 1, 1, D_MODEL), lambda i: (layer, k, ADA_PROMPT_ROW0 + tile(i) // per_seq, 0, 0))
    sample_mod = lambda k: pl.BlockSpec((1, 1, SB, 1, D_MODEL), lambda i: (layer, k, 0, 0, 0))
    return pl.pallas_call(
        functools.partial(_ffn_kernel, n_prompt_tiles=n, layer=layer, which=which),
        grid=(n + 1,),
        in_specs=[
            prompt_act, sample_act,
            prompt_mod(k0), prompt_mod(k0 + 1), prompt_mod(k0 + 2),
            sample_mod(k0), sample_mod(k0 + 1), sample_mod(k0 + 2),
            pl.BlockSpec((1, 1, 1, D_MODEL), lambda i: (layer, g_idx, 0, 0)),
            pl.BlockSpec(memory_space=pl.ANY),
            pl.BlockSpec(memory_space=pl.ANY),
        ],
        out_specs=[prompt_act, sample_act],
        out_shape=[jax.ShapeDtypeStruct(xp.shape, F32), jax.ShapeDtypeStruct(xs.shape, F32)],
        scratch_shapes=[
            pltpu.VMEM((D_MODEL, 2 * D_FF), BF16), pltpu.VMEM((D_FF, D_MODEL), BF16),
            pltpu.VMEM((2, D_MODEL, FFN_CHUNK), F32), pltpu.VMEM((2, D_MODEL, FFN_CHUNK), F32),
            pltpu.VMEM((2, FFN_CHUNK, D_MODEL), F32),
            pltpu.SemaphoreType.DMA((3, 2)),
        ],
        compiler_params=pltpu.CompilerParams(
            dimension_semantics=("arbitrary",), vmem_limit_bytes=VMEM_LIMIT_BYTES),
        name="ffn",
    )(xp, xs, mods, mods, mods, mods, mods, mods, norm_g4, w_in, w_out)


def _group_scan(a, b):
    row = lax.broadcasted_iota(jnp.int32, a.shape, 1)
    d = 1
    while d < SUBLANES:
        keep = row >= d
        a_sh = jnp.where(keep, pltpu.roll(a, d, axis=1), 1.0)
        b_sh = jnp.where(keep, pltpu.roll(b, d, axis=1), 0.0)
        b = a * b_sh + b
        a = a * a_sh
        d *= 2
    return a, b


def _linear_scan(a, b, h_prev, a_ref, h_ref):
    nb, r, C = a.shape
    first = slice(0, SUBLANES)
    a_run, h_run = a[:, first], b[:, first]
    a_ref[:, first, :] = a_run
    h_ref[:, first, :] = h_run
    for g0 in range(SUBLANES, r, SUBLANES):
        rows = slice(g0, g0 + SUBLANES)
        h_run = a[:, rows] * h_run + b[:, rows]
        a_run = a[:, rows] * a_run
        a_ref[:, rows, :] = a_run
        h_ref[:, rows, :] = h_run
    a_inc, h_inc = _group_scan(a_run, h_run)
    seg_end = a_inc * h_prev + h_inc
    row = lax.broadcasted_iota(jnp.int32, seg_end.shape, 1)
    seg_in = jnp.where(row == 0, h_prev, pltpu.roll(seg_end, 1, axis=1))
    for g0 in range(0, r, SUBLANES):
        rows = slice(g0, g0 + SUBLANES)
        h_ref[:, rows, :] = h_ref[:, rows, :] + a_ref[:, rows, :] * seg_in
    return seg_end[:, SUBLANES - 1:, :]


LRU_TAIL = (CONV_W - 1) * SUBLANES


def _lru_kernel(x_ref, sh_ref, sc_ref, gt_ref, g_ref, win_ref, cw_ref, cb_ref, gw_ref, gab_ref,
                gxb_ref, lam_ref, wout_ref, conv0_ref, h0_ref, perm_ref, unperm_ref,
                o_ref, hlast_ref, tail_ref, h_scr, tail_scr, arun_scr, hall_scr):
    @pl.when(pl.program_id(1) == 0)
    def _():
        h_scr[...] = h0_ref[...]
        tail_scr[...] = conv0_ref[...]

    nb_all, r, _ = x_ref.shape
    assert r // SUBLANES >= CONV_W - 1
    C = D_RNN
    nb = perm_ref.shape[0] // r
    for b0 in range(0, nb_all, nb):
        seq = slice(b0, b0 + nb)
        x = x_ref[seq]
        u = _modulate(x, g_ref[0, 0], sh_ref[0, 0, seq], sc_ref[0, 0, seq])
        ub = _dot(perm_ref[...], u.reshape(nb * r, D_MODEL).astype(BF16)).astype(BF16)
        gate_br = _dot(ub, win_ref[0, :, :C])
        x_br = _dot(ub, win_ref[0, :, C:]).reshape(nb, r, C)

        last = x_br[:, r - LRU_TAIL:, :]
        groups = (nb * (CONV_W - 1), SUBLANES, C)
        row = lax.broadcasted_iota(jnp.int32, groups, 1)
        wrap = jnp.where(row == 0, pltpu.roll(tail_scr[seq].reshape(groups), 1, axis=1),
                         pltpu.roll(last.reshape(groups), 1, axis=1)).reshape(nb, LRU_TAIL, C)
        tail_scr[seq] = last
        tail_ref[seq] = last
        cw = cw_ref[0]
        xc = cb_ref[0] + cw[CONV_W - 1:CONV_W] * x_br
        for d in range(1, CONV_W):
            shifted = jnp.concatenate([wrap[:, LRU_TAIL - d * SUBLANES:], x_br[:, :r - d * SUBLANES]],
                                      axis=1)
            xc = xc + cw[CONV_W - 1 - d:CONV_W - d] * shifted

        xc2 = xc.reshape(nb * r, C)
        xcb = xc2.astype(BF16)
        sw = C // LRU_SUPER
        gates = [_dot(xcb[:, s * sw:(s + 1) * sw], gw_ref[0, s]) for s in range(LRU_SUPER)]
        rg = _sigmoid(jnp.concatenate([g[:, :sw] for g in gates], axis=-1) + gab_ref[0])
        ig = _sigmoid(jnp.concatenate([g[:, sw:] for g in gates], axis=-1) + gxb_ref[0])
        nl = -lam_ref[0]
        softplus = jnp.maximum(nl, 0.0) + jnp.log1p(jnp.exp(-jnp.abs(nl)))
        log_a = (-LRU_C * rg) * softplus
        a = jnp.exp(log_a)
        th = jnp.tanh(log_a)
        one_minus_a2 = (-2.0 * th) / (1.0 - th)
        root = jnp.where(one_minus_a2 > 0.0, one_minus_a2 * lax.rsqrt(one_minus_a2), 0.0)
        b_in = root * (ig * xc2)

        h_last = _linear_scan(a.reshape(nb, r, C), b_in.reshape(nb, r, C), h_scr[seq],
                              arun_scr.at[seq], hall_scr.at[seq])
        h_scr[seq] = h_last
        hlast_ref[seq] = h_last

        gb = gate_br
        gelu = 0.5 * gb * (1.0 + jnp.tanh(math.sqrt(2.0 / math.pi) * (gb + 0.044715 * (gb * gb * gb))))
        z = (gelu * hall_scr[seq].reshape(nb * r, C)).astype(BF16)
        y = _dot(_dot(unperm_ref[...], z).astype(BF16), wout_ref[0])
        o_ref[seq] = x + gt_ref[0, 0, seq] * y.reshape(nb, r, D_MODEL)


def _lru(x, rows, mods, norm_g4, layer, j, w_in, conv_w, conv_b, gate_w, ga_b, gx_b, lam, w_out,
         conv0, h0):
    nbt = x.shape[0]
    C = D_RNN
    seqs = max(1, LRU_PERM_ROWS // rows.r)
    assert rows.nb % seqs == 0
    n_rows, G = seqs * rows.r, rows.r // SUBLANES
    b_idx, g_idx, i_idx = np.meshgrid(np.arange(seqs), np.arange(G), np.arange(SUBLANES), indexing='ij')
    source = (b_idx * rows.r + i_idx * G + g_idx).reshape(n_rows)
    perm = np.zeros((n_rows, n_rows), np.float32)
    perm[np.arange(n_rows), source] = 1.0
    vec = lambda: _const_spec((1, 1, C), (j, 0, 0))
    out, h_last, tail = pl.pallas_call(
        _lru_kernel,
        grid=rows.grid,
        in_specs=[
            rows.act(D_MODEL),
            rows.mod(layer, 3), rows.mod(layer, 4), rows.mod(layer, 5),
            _const_spec((1, 1, 1, D_MODEL), (layer, 1, 0, 0)),
            _const_spec((1, D_MODEL, 2 * C), (j, 0, 0)),
            _const_spec((1, CONV_W, C), (j, 0, 0)),
            vec(),
            _const_spec((1, LRU_SUPER, C // LRU_SUPER, 2 * C // LRU_SUPER), (j, 0, 0, 0)),
            vec(), vec(), vec(),
            _const_spec((1, C, D_MODEL), (j, 0, 0)),
            rows.per_seq(LRU_TAIL, C),
            rows.per_seq(1, C),
            _const_spec((n_rows, n_rows), (0, 0)),
            _const_spec((n_rows, n_rows), (0, 0)),
        ],
        out_specs=[rows.act(D_MODEL), rows.per_seq(1, C), rows.per_seq(LRU_TAIL, C)],
        out_shape=[
            jax.ShapeDtypeStruct(x.shape, F32),
            jax.ShapeDtypeStruct((nbt, 1, C), F32),
            jax.ShapeDtypeStruct((nbt, LRU_TAIL, C), F32),
        ],
        scratch_shapes=[pltpu.VMEM((rows.nb, 1, C), F32), pltpu.VMEM((rows.nb, LRU_TAIL, C), F32),
                        pltpu.VMEM((rows.nb, rows.r, C), F32), pltpu.VMEM((rows.nb, rows.r, C), F32)],
        compiler_params=_cparams(),
        name="lru",
    )(x, mods, mods, mods, norm_g4, w_in, conv_w, conv_b, gate_w, ga_b, gx_b, lam, w_out, conv0, h0,
      jnp.asarray(perm, BF16), jnp.asarray(perm.T, BF16))
    return out, h_last[:, 0], tail[:, SUBLANES - 1::SUBLANES]


POOL_PAD = 16


def _pool_kernel(x_ref, sh_ref, sc_ref, gt_ref, g_ref, hist0_ref, w_ref, b_ref, ps_ref,
                 o_ref, st_ref, hist_scr, *, t_base):
    t = pl.program_id(1)

    @pl.when(t == 0)
    def _():
        hist_scr[...] = hist0_ref[...]

    x = x_ref[...]
    nb, r, _ = x.shape
    u = _modulate(x, g_ref[0, 0], sh_ref[0, 0], sc_ref[0, 0])
    ext = jnp.concatenate([hist_scr[...], u], axis=1)
    new_hist = u[:, r - POOL_PAD:, :]
    hist_scr[...] = new_hist
    st_ref[...] = new_hist

    row = lax.broadcasted_iota(jnp.int32, (1, r, 1), 1)
    t_glob = t_base + t * r + row
    gate = gt_ref[0, 0]
    s = ext
    for g, wnd in enumerate(POOL_WINDOWS):
        lo = g * POOL_GW
        s = s[:, :, (POOL_GW if g > 0 else 0):]
        s = s + pltpu.roll(s, wnd // 2, axis=1)
        cnt = jnp.minimum(t_glob + 1, wnd).astype(F32)
        mean = s[:, POOL_PAD:, :POOL_GW] / cnt
        d = mean - u[:, :, lo:lo + POOL_GW]
        y = _dot(d.reshape(nb * r, POOL_GW).astype(BF16), w_ref[0, g]).reshape(nb, r, POOL_GW)
        y = (y + b_ref[0, :, lo:lo + POOL_GW]) * ps_ref[0, :, lo:lo + POOL_GW]
        o_ref[:, :, lo:lo + POOL_GW] = x[:, :, lo:lo + POOL_GW] + gate[:, :, lo:lo + POOL_GW] * y


def _pool(x, rows, mods, norm_g4, layer, j, hist0, w, b, scale, t_base):
    nbt = x.shape[0]
    out, st = pl.pallas_call(
        functools.partial(_pool_kernel, t_base=t_base),
        grid=rows.grid,
        in_specs=[
            rows.act(D_MODEL),
            rows.mod(layer, 3), rows.mod(layer, 4), rows.mod(layer, 5),
            _const_spec((1, 1, 1, D_MODEL), (layer, 1, 0, 0)),
            rows.per_seq(POOL_PAD, D_MODEL),
            _const_spec((1, len(POOL_WINDOWS), POOL_GW, POOL_GW), (j, 0, 0, 0)),
            _const_spec((1, 1, D_MODEL), (j, 0, 0)),
            _const_spec((1, 1, D_MODEL), (j, 0, 0)),
        ],
        out_specs=[rows.act(D_MODEL), rows.per_seq(POOL_PAD, D_MODEL)],
        out_shape=[jax.ShapeDtypeStruct(x.shape, F32),
                   jax.ShapeDtypeStruct((nbt, POOL_PAD, D_MODEL), F32)],
        scratch_shapes=[pltpu.VMEM((rows.nb, POOL_PAD, D_MODEL), F32)],
        compiler_params=_cparams(),
        name="pool",
    )(x, mods, mods, mods, norm_g4, hist0, w, b, scale)
    return out, st[:, POOL_PAD - POOL_HIST:]


def _qkv_kernel(x_ref, sh_ref, sc_ref, g_ref, w_ref, qg_ref, kg_ref, seg_ref,
                q_ref, kf_ref, kb_ref, vf_ref, vb_ref, *, v_transposed):
    x = x_ref[...]
    nb, r, _ = x.shape
    u = _modulate(x, g_ref[0, 0], sh_ref[0, 0], sc_ref[0, 0])
    ub = u.reshape(nb * r, D_MODEL).astype(BF16)
    q = _dot(ub, w_ref[0, :, :D_MODEL])
    k = _dot(ub, w_ref[0, :, D_MODEL:2 * D_MODEL])
    v = _dot(ub, w_ref[0, :, 2 * D_MODEL:])

    def head_norm(z, gain):
        ms = _dot((z * z).astype(BF16), seg_ref[...])
        return (z * lax.rsqrt(ms + EPS)) * gain

    qn = head_norm(q, qg_ref[...]) * (HEAD_DIM ** -0.5 * LOG2E)
    kn = head_norm(k, kg_ref[...])
    lane = lax.broadcasted_iota(jnp.int32, (1, D_MODEL), 1)
    first = (lane % HEAD_W) < HEAD_DIM
    q_ref[:, 0] = jnp.where(first, qn, 0.0).astype(BF16).reshape(nb, r, D_MODEL)
    q_ref[:, 1] = jnp.where(first, 0.0, qn).astype(BF16).reshape(nb, r, D_MODEL)
    kn3 = kn.reshape(nb, r, D_MODEL)
    v3 = v.reshape(nb, r, D_MODEL)
    for h in range(N_HEADS):
        kf_ref[:, :, h, :] = kn3[:, :, h * HEAD_W:(h + 1) * HEAD_W]
        vf_ref[:, :, h, :] = v3[:, :, h * HEAD_W:(h + 1) * HEAD_W]
    kb_ref[...] = kn.astype(BF16).reshape(nb, r, D_MODEL)
    if v_transposed:
        for h in range(N_HEADS):
            vb_ref[0, h, :HEAD_W] = v[:, h * HEAD_W:(h + 1) * HEAD_W].T.astype(BF16)
            vb_ref[0, h, HEAD_W:] = jnp.ones((VT_ROWS - HEAD_W, r), BF16)
    else:
        vb_ref[...] = v.astype(BF16).reshape(nb, r, D_MODEL)


def _qkv(x, rows, mods, norm_g4, layer, j, w_in, qg, kg, seg, v_transposed):
    nbt, seq, _ = x.shape
    q_spec = pl.BlockSpec((rows.nb, 2, rows.r, D_MODEL), lambda b, t: (b, 0, t, 0))
    heads_spec = pl.BlockSpec((rows.nb, rows.r, N_HEADS, HEAD_W), lambda b, t: (b, t, 0, 0))
    if v_transposed:
        assert rows.nb == 1
        vb_spec = pl.BlockSpec((1, N_HEADS, VT_ROWS, rows.r), lambda b, t: (b, 0, 0, t))
        vb_shape = jax.ShapeDtypeStruct((nbt, N_HEADS, VT_ROWS, seq), BF16)
    else:
        vb_spec = rows.act(D_MODEL)
        vb_shape = jax.ShapeDtypeStruct(x.shape, BF16)
    return pl.pallas_call(
        functools.partial(_qkv_kernel, v_transposed=v_transposed),
        grid=rows.grid,
        in_specs=[
            rows.act(D_MODEL),
            rows.mod(layer, 3), rows.mod(layer, 4),
            _const_spec((1, 1, 1, D_MODEL), (layer, 1, 0, 0)),
            _const_spec((1, D_MODEL, 3 * D_MODEL), (j, 0, 0)),
            _const_spec((1, D_MODEL), (0, 0)),
            _const_spec((1, D_MODEL), (0, 0)),
            _const_spec((D_MODEL, D_MODEL), (0, 0)),
        ],
        out_specs=[q_spec, heads_spec, rows.act(D_MODEL), heads_spec, vb_spec],
        out_shape=[
            jax.ShapeDtypeStruct((nbt, 2, seq, D_MODEL), BF16),
            jax.ShapeDtypeStruct((nbt, seq, N_HEADS, HEAD_W), F32),
            jax.ShapeDtypeStruct(x.shape, BF16),
            jax.ShapeDtypeStruct((nbt, seq, N_HEADS, HEAD_W), F32),
            vb_shape,
        ],
        compiler_params=_cparams(),
        name="qkv",
    )(x, mods, mods, norm_g4, w_in, qg, kg, seg)


_BUCKET_EDGES = (12, 16, 23, 32, 46, 64, 91)
_FAR_BUCKET = NUM_BUCKETS // 2 - 1


def _bucket_of(rel):
    n = abs(rel)
    small = NUM_BUCKETS // 4
    b = n if n < small else small + sum(n >= e for e in _BUCKET_EDGES)
    return b + (NUM_BUCKETS // 2 if rel > 0 else 0)


_BUCKET_RUNS = tuple((r, _bucket_of(r)) for r in range(-_BUCKET_EDGES[-1] + 1, _BUCKET_EDGES[-1] + 1)
                     if _bucket_of(r) != _bucket_of(r - 1))


def _bias_kernel(rb_ref, o_ref, *, rel0, mask_from, keys_on_rows, tile, tiles):
    h = pl.program_id(0)
    R, W = tile
    for row0, col0, fill in tiles:
        window = (0, slice(row0, row0 + R), slice(col0, col0 + W))
        if fill is not None:
            o_ref[window] = jnp.full((R, W), fill, F32)
            continue
        i = lax.broadcasted_iota(jnp.int32, (R, W), 0) + row0
        jj = lax.broadcasted_iota(jnp.int32, (R, W), 1) + col0
        key, qry = (i, jj) if keys_on_rows else (jj, i)
        rel = key - qry + rel0
        out = jnp.full((R, W), rb_ref[_FAR_BUCKET, h], F32)
        for first_rel, bucket in _BUCKET_RUNS:
            out = jnp.where(rel >= first_rel, rb_ref[bucket, h], out)
        out = (out - rb_ref[_FAR_BUCKET, h]) * LOG2E
        if mask_from is not None:
            visible = ((key - mask_from) // CHUNK <= qry // CHUNK) | (key < mask_from)
            out = jnp.where(visible, out, NEG_INF)
        o_ref[window] = out


def _bias_tile(rel_bias, rows, width, rel0, mask_from, keys_on_rows=False):
    R = BIAS_TILE_ROWS if rows % BIAS_TILE_ROWS == 0 else rows
    W = LANES if width % LANES == 0 else width
    tiles = []
    for row0 in range(0, rows, R):
        for col0 in range(0, width, W):
            key0, qry0 = (row0, col0) if keys_on_rows else (col0, row0)
            key1, qry1 = (row0 + R, col0 + W) if keys_on_rows else (col0 + W, row0 + R)
            fill = None
            unmasked = mask_from is None or key1 <= mask_from
            if unmasked and (key1 - 1) - qry0 + rel0 <= -_BUCKET_EDGES[-1]:
                fill = 0.0
            if mask_from is not None and key0 >= mask_from and \
                    (key0 - mask_from) // CHUNK > (qry1 - 1) // CHUNK:
                fill = NEG_INF
            tiles.append((row0, col0, fill))
    return pl.pallas_call(
        functools.partial(_bias_kernel, rel0=rel0, mask_from=mask_from, keys_on_rows=keys_on_rows,
                          tile=(R, W), tiles=tuple(tiles)),
        grid=(N_HEADS,),
        in_specs=[pl.BlockSpec(memory_space=pltpu.SMEM)],
        out_specs=pl.BlockSpec((1, rows, width), lambda h: (h, 0, 0)),
        out_shape=jax.ShapeDtypeStruct((N_HEADS, rows, width), F32),
        name="bias_tile",
    )(rel_bias)


def _diff_lambda(lp, lam_init):
    s1 = jnp.sum(lp[0:1] * lp[1:2], axis=-1, keepdims=True)
    s2 = jnp.sum(lp[2:3] * lp[3:4], axis=-1, keepdims=True)
    return jnp.exp(s1) - jnp.exp(s2) + lam_init


def _attn_finish(acc, l, nq, lam, subg, lam_init):
    o = acc / l
    o = o[:nq] - lam * o[nq:]
    ms = jnp.mean(o * o, axis=-1, keepdims=True)
    return ((o * lax.rsqrt(ms + EPS)) * subg) * (1.0 - lam_init)


def _attn_prompt_kernel(q_ref, k_ref, vt_ref, bias_ref, lam_ref, subg_ref, o_ref,
                        qt_scr, s0_scr, s1_scr, mx0_scr, mx1_scr, m_scr, acc_scr, *, lam_init):
    qi = pl.program_id(2)
    tq, tk = ATT_TQ, ATT_TK
    width = 2 * tq
    n_blocks = (qi + 1) * (tq // tk)
    qt_scr[:, :tq] = q_ref[0, 0].astype(F32).T.astype(BF16)
    qt_scr[:, tq:] = q_ref[0, 1].astype(F32).T.astype(BF16)
    m_scr[...] = jnp.full(m_scr.shape, NEG_INF, F32)
    acc_scr[...] = jnp.zeros(acc_scr.shape, F32)

    def block_start(c):
        return c * tk

    def scores_rows(c, s_ref, near, r0):
        rows = slice(r0, r0 + ATT_ROW_CHUNK)
        start = pl.multiple_of(block_start(c) + r0, ATT_ROW_CHUNK)
        s = _dot(k_ref[0, pl.ds(start, ATT_ROW_CHUNK), :], qt_scr[...])
        if near is not None:
            bias = bias_ref[0, near, rows, :]
            s = s + jnp.concatenate([bias, bias], axis=1)
        s_ref[rows, :] = s
        return jnp.max(s.reshape(ATT_ROW_CHUNK // SUBLANES, SUBLANES, width), axis=0)

    def new_max(mx_ref):
        m_old = m_scr[...]
        m_new = jnp.maximum(m_old, jnp.max(mx_ref[...], axis=0, keepdims=True))
        m_scr[...] = m_new
        return m_new, jnp.exp2(m_old - m_new)

    def absorb_rows(c, s_ref, m_new, r0):
        p = jnp.exp2((s_ref[r0:r0 + ATT_ROW_CHUNK, :] - m_new).astype(BF16))
        start = pl.multiple_of(block_start(c) + r0, ATT_ROW_CHUNK)
        return _dot(vt_ref[0, 0, :, pl.ds(start, ATT_ROW_CHUNK)], p)

    def rescale_add(alpha, parts):
        acc_scr[...] = alpha * acc_scr[...] + functools.reduce(jnp.add, parts)

    chunks = range(0, tk, ATT_ROW_CHUNK)

    def scores(c, s_ref, mx_ref, near):
        mx_ref[...] = functools.reduce(jnp.maximum, [scores_rows(c, s_ref, near, r0) for r0 in chunks])

    def absorb(c, s_ref, mx_ref):
        m_new, alpha = new_max(mx_ref)
        rescale_add(alpha, [absorb_rows(c, s_ref, m_new, r0) for r0 in chunks])

    def overlap(c_new, s_new, mx_new, near, c_old, s_old, mx_old):
        scores(c_new, s_new, mx_new, near)
        absorb(c_old, s_old, mx_old)

    @pl.when(qi == 0)
    def _():
        scores(0, s0_scr, mx0_scr, 1)

    @pl.when(qi > 0)
    def _():
        scores(0, s0_scr, mx0_scr, None)

    def far_trip(t, carry):
        c = 2 * t
        overlap(c + 1, s1_scr, mx1_scr, None, c, s0_scr, mx0_scr)
        overlap(c + 2, s0_scr, mx0_scr, None, c + 1, s1_scr, mx1_scr)
        return carry

    lax.fori_loop(0, jnp.maximum(n_blocks // 2 - 2, 0), far_trip, 0)

    @pl.when(qi > 0)
    def _():
        c = n_blocks - 4
        overlap(c + 1, s1_scr, mx1_scr, 0, c, s0_scr, mx0_scr)
        overlap(c + 2, s0_scr, mx0_scr, 1, c + 1, s1_scr, mx1_scr)

    overlap(n_blocks - 1, s1_scr, mx1_scr, 2, n_blocks - 2, s0_scr, mx0_scr)
    absorb(n_blocks - 1, s1_scr, mx1_scr)

    lam = _diff_lambda(lam_ref[0], lam_init)
    acc = acc_scr[...]
    o = acc[:HEAD_W] / acc[HEAD_W:HEAD_W + 1]
    o = o[:, :tq] - lam * o[:, tq:]
    ms = jnp.mean(o * o, axis=0, keepdims=True)
    on = ((o * lax.rsqrt(ms + EPS)) * subg_ref[...]) * (1.0 - lam_init)
    o_ref[0] = on.T.astype(BF16)


def _attn_prompt(q, kb, vt, bias, lam_p, subg_col, j, lam_init):
    nbt, _, seq, _ = q.shape
    tq, tk = ATT_TQ, ATT_TK
    return pl.pallas_call(
        functools.partial(_attn_prompt_kernel, lam_init=lam_init),
        grid=(nbt, N_HEADS, seq // tq),
        in_specs=[
            pl.BlockSpec((1, 2, tq, HEAD_W), lambda b, h, i: (b, 0, i, h)),
            pl.BlockSpec((1, seq, HEAD_W), lambda b, h, i: (b, 0, h)),
            pl.BlockSpec((1, 1, VT_ROWS, seq), lambda b, h, i: (b, h, 0, 0)),
            pl.BlockSpec((1, 3, tk, tq), lambda b, h, i: (h, 0, 0, 0)),
            pl.BlockSpec((1, 4, HEAD_DIM), lambda b, h, i: (j, 0, 0)),
            pl.BlockSpec((HEAD_W, 1), lambda b, h, i: (0, 0)),
        ],
        out_specs=pl.BlockSpec((1, tq, HEAD_W), lambda b, h, i: (b, i, h)),
        out_shape=jax.ShapeDtypeStruct((nbt, seq, D_MODEL), BF16),
        scratch_shapes=[pltpu.VMEM((HEAD_W, 2 * tq), BF16),
                        pltpu.VMEM((tk, 2 * tq), F32), pltpu.VMEM((tk, 2 * tq), F32),
                        pltpu.VMEM((SUBLANES, 2 * tq), F32), pltpu.VMEM((SUBLANES, 2 * tq), F32),
                        pltpu.VMEM((1, 2 * tq), F32),
                        pltpu.VMEM((VT_ROWS, 2 * tq), F32)],
        compiler_params=pltpu.CompilerParams(
            dimension_semantics=("arbitrary", "arbitrary", "arbitrary"),
            vmem_limit_bytes=VMEM_LIMIT_BYTES),
        name="attn_prompt",
    )(q, kb, vt, bias, lam_p, subg_col)


def _attn_sample_kernel(q_ref, ck_ref, cv_ref, kn_ref, vn_ref, bc_ref, bn_ref, lam_ref, subg_ref,
                        o_ref, *, lam_init):
    nq = q_ref.shape[2]
    lam = _diff_lambda(lam_ref[0], lam_init)
    for h in range(N_HEADS):
        cols = slice(h * HEAD_W, (h + 1) * HEAD_W)
        qq = q_ref[0, :, :, cols].reshape(2 * nq, HEAD_W)
        kc = ck_ref[0, :, cols].astype(BF16)
        vc = cv_ref[0, :, cols].astype(BF16)
        kn = kn_ref[0, :, cols]
        vn = vn_ref[0, :, cols]
        past = kc.shape[0]
        s_c = (_dot_nt(qq, kc).reshape(2, nq, past) + bc_ref[h][None]).reshape(2 * nq, past)
        s_n = (_dot_nt(qq, kn).reshape(2, nq, nq) + bn_ref[h][None]).reshape(2 * nq, nq)
        m = jnp.maximum(jnp.max(s_c, axis=-1, keepdims=True), jnp.max(s_n, axis=-1, keepdims=True))
        p_c = jnp.exp2(s_c - m)
        p_n = jnp.exp2(s_n - m)
        l = jnp.sum(p_c, axis=-1, keepdims=True) + jnp.sum(p_n, axis=-1, keepdims=True)
        acc = _dot(p_c.astype(BF16), vc) + _dot(p_n.astype(BF16), vn)
        o_ref[0, :, cols] = _attn_finish(acc, l, nq, lam, subg_ref[...], lam_init).astype(BF16)


def _attn_sample(q, ck, cv, kb, vb, bias_c, bias_n, lam_p, subg, j, lam_init):
    nbt, _, nq, _ = q.shape
    past = ck.shape[1]
    return pl.pallas_call(
        functools.partial(_attn_sample_kernel, lam_init=lam_init),
        grid=(nbt,),
        in_specs=[
            pl.BlockSpec((1, 2, nq, D_MODEL), lambda b: (b, 0, 0, 0)),
            pl.BlockSpec((1, past, D_MODEL), lambda b: (b, 0, 0)),
            pl.BlockSpec((1, past, D_MODEL), lambda b: (b, 0, 0)),
            pl.BlockSpec((1, nq, D_MODEL), lambda b: (b, 0, 0)),
            pl.BlockSpec((1, nq, D_MODEL), lambda b: (b, 0, 0)),
            pl.BlockSpec((N_HEADS, nq, past), lambda b: (0, 0, 0)),
            pl.BlockSpec((N_HEADS, nq, nq), lambda b: (0, 0, 0)),
            pl.BlockSpec((1, 4, HEAD_DIM), lambda b: (j, 0, 0)),
            pl.BlockSpec((1, HEAD_W), lambda b: (0, 0)),
        ],
        out_specs=pl.BlockSpec((1, nq, D_MODEL), lambda b: (b, 0, 0)),
        out_shape=jax.ShapeDtypeStruct((nbt, nq, D_MODEL), BF16),
        compiler_params=pltpu.CompilerParams(
            dimension_semantics=("arbitrary",), vmem_limit_bytes=VMEM_LIMIT_BYTES),
        name="attn_sample",
    )(q, ck, cv, kb, vb, bias_c, bias_n, lam_p, subg)


def _proj_res_kernel(x_ref, a_ref, gt_ref, w_ref, o_ref):
    x = x_ref[...]
    nb, r, _ = x.shape
    a = a_ref[...].reshape(nb * r, a_ref.shape[-1])
    y = _dot(a, w_ref[0])
    o_ref[...] = x + gt_ref[0, 0] * y.reshape(nb, r, D_MODEL)


def _proj_res(x, a, rows, mods, layer, j, w):
    return pl.pallas_call(
        _proj_res_kernel,
        grid=rows.grid,
        in_specs=[
            rows.act(D_MODEL), rows.act(a.shape[-1]), rows.mod(layer, 5),
            _const_spec((1,) + w.shape[1:], (j, 0, 0)),
        ],
        out_specs=rows.act(D_MODEL),
        out_shape=jax.ShapeDtypeStruct(x.shape, F32),
        compiler_params=_cparams(),
        name="proj_res",
    )(x, a, mods, w)


def kernel(x_prompt, x_sample, c_prompt, c_sample, state_lru_h, state_lru_conv, state_pool, cache_k, cache_v, ada_w, ada_b, norm_g, ffn_w_in, ffn_w_out, lru_w_in, lru_conv_w, lru_conv_b, lru_ga_w, lru_ga_b, lru_gx_w, lru_gx_b, lru_lambda, lru_w_out, pool_w, pool_b, pool_scale, attn_w_in, attn_q_g, attn_k_g, attn_lambda, attn_sub_g, attn_w_out, rel_bias):
    B, S, _ = x_prompt.shape
    SB, T, _ = x_sample.shape
    P = cache_k.shape[2]
    assert SB == ADA_PROMPT_ROW0 and ADA_PROMPT_ROW0 + B <= ADA_ROWS
    n_a, n_b, n_c = lru_w_in.shape[0], pool_w.shape[0], attn_w_in.shape[0]

    c_all = jnp.concatenate(
        [c_sample, c_prompt, jnp.zeros((ADA_ROWS - SB - B, D_MODEL), F32)], axis=0)
    mods = _adaln(c_all, ada_w, ada_b)
    norm_g4 = norm_g.reshape(DEPTH, 3, 1, D_MODEL)

    lru_in_b = lru_w_in.astype(BF16)
    lru_out_b = lru_w_out.astype(BF16)
    per_super = LRU_BLOCKS // LRU_SUPER
    eye = jnp.eye(per_super, dtype=F32)

    def super_blocks(w):
        w5 = w.reshape(w.shape[0], LRU_SUPER, per_super, LRU_BS, LRU_BS)
        full = jnp.einsum('nshij,hg->nshigj', w5, eye)
        return full.reshape(w.shape[0], LRU_SUPER, per_super * LRU_BS, per_super * LRU_BS)

    lru_gate_b = jnp.concatenate([super_blocks(lru_ga_w), super_blocks(lru_gx_w)], axis=-1).astype(BF16)
    vec3 = lambda v: v.reshape(v.shape[0], 1, v.shape[-1])
    pool_w_b = pool_w.astype(BF16)
    attn_in_b = attn_w_in.astype(BF16)
    attn_out_b = attn_w_out.astype(BF16)
    seg = jnp.kron(jnp.eye(D_MODEL // HEAD_DIM, dtype=F32),
                   jnp.full((HEAD_DIM, HEAD_DIM), 1.0 / HEAD_DIM, F32)).astype(BF16)

    rows_p = {r: _Rows(B, S, r, True) for r in {POOL_ROWS, QKV_ROWS, OUT_ROWS}}
    rows_lru = _Rows(B, S, LRU_ROWS, True, seqs_per_block=B)
    rows_s = _Rows(SB, T, T, False)

    xp, xs = x_prompt, x_sample
    h_p, h_s, cv_p, cv_s, pl_p, pl_s, k_p, v_p, k_s, v_s = ([] for _ in range(10))
    for l in range(DEPTH):
        kind, j = l % 3, l // 3
        xp, xs = _ffn(xp, xs, mods, norm_g4, ffn_w_in, ffn_w_out, l, 0)
        if kind == 0:
            prm = (lru_in_b, lru_conv_w, vec3(lru_conv_b), lru_gate_b, vec3(lru_ga_b),
                   vec3(lru_gx_b), vec3(lru_lambda), lru_out_b)
            xp, hp, bp = _lru(xp, rows_lru, mods, norm_g4, l, j, *prm,
                              jnp.zeros((B, LRU_TAIL, D_RNN), F32), jnp.zeros((B, 1, D_RNN), F32))
            conv0 = jnp.zeros((SB, LRU_TAIL, D_RNN), F32).at[:, SUBLANES - 1::SUBLANES].set(
                state_lru_conv[j])
            xs, hs, bs = _lru(xs, rows_s, mods, norm_g4, l, j, *prm, conv0,
                              state_lru_h[j][:, None, :])
            h_p.append(hp), h_s.append(hs), cv_p.append(bp), cv_s.append(bs)
        elif kind == 1:
            prm = (pool_w_b, vec3(pool_b), vec3(pool_scale))
            xp, sp = _pool(xp, rows_p[POOL_ROWS], mods, norm_g4, l, j,
                           jnp.zeros((B, POOL_PAD, D_MODEL), F32), *prm, 0)
            hist0 = jnp.pad(state_pool[j], ((0, 0), (POOL_PAD - POOL_HIST, 0), (0, 0)))
            xs, ss = _pool(xs, rows_s, mods, norm_g4, l, j, hist0, *prm, POOL_HIST)
            pl_p.append(sp), pl_s.append(ss)
        else:
            lam_init = 0.8 - 0.6 * math.exp(-0.3 * l)
            qg = jnp.tile(attn_q_g[j], D_MODEL // HEAD_DIM)[None]
            kg = jnp.tile(attn_k_g[j], D_MODEL // HEAD_DIM)[None]
            subg = attn_sub_g[j][None]
            q, kf, kb, vf, vt = _qkv(xp, rows_p[QKV_ROWS], mods, norm_g4, l, j, attn_in_b, qg, kg, seg,
                                     True)
            assert ATT_TQ == 2 * ATT_TK
            bias_p = _bias_tile(rel_bias, 3 * ATT_TK, ATT_TQ, -ATT_TK, ATT_TK, keys_on_rows=True)
            bias_p = bias_p.reshape(N_HEADS, 3, ATT_TK, ATT_TQ)
            o = _attn_prompt(q, kb, vt, bias_p, attn_lambda, attn_sub_g[j][:, None], j, lam_init)
            xp = _proj_res(xp, o, rows_p[OUT_ROWS], mods, l, j, attn_out_b)
            k_p.append(kf), v_p.append(vf)
            q, kf, kb, vf, vb = _qkv(xs, rows_s, mods, norm_g4, l, j, attn_in_b, qg, kg, seg, False)
            bias_c = _bias_tile(rel_bias, T, P, -P, None)
            bias_n = _bias_tile(rel_bias, T, T, 0, None)
            o = _attn_sample(q, cache_k[j].reshape(SB, P, D_MODEL), cache_v[j].reshape(SB, P, D_MODEL),
                             kb, vb, bias_c, bias_n, attn_lambda, subg, j, lam_init)
            xs = _proj_res(xs, o, rows_s, mods, l, j, attn_out_b)
            k_s.append(kf), v_s.append(vf)
        xp, xs = _ffn(xp, xs, mods, norm_g4, ffn_w_in, ffn_w_out, l, 1)
    return (xp, xs, jnp.stack(h_p), jnp.stack(h_s), jnp.stack(cv_p), jnp.stack(cv_s),
            jnp.stack(pl_p), jnp.stack(pl_s), jnp.stack(k_p), jnp.stack(v_p),
            jnp.stack(k_s), jnp.stack(v_s))
```

```python
import functools
import math

import jax
import jax.numpy as jnp
import numpy as np
from jax import lax
from jax.experimental import pallas as pl
from jax.experimental.pallas import tpu as pltpu

F32 = jnp.float32
BF16 = jnp.bfloat16

D_MODEL = 1024
DEPTH = 4
D_FF = 2816
D_RNN = 1280
LRU_BLOCKS = 16
LRU_BS = D_RNN // LRU_BLOCKS
LRU_SUPER = 2
CONV_W = 4
LRU_C = 8.0
POOL_WINDOWS = (2, 4, 8, 16)
POOL_GW = D_MODEL // len(POOL_WINDOWS)
POOL_HIST = 15
N_HEADS = 8
HEAD_DIM = D_MODEL // (2 * N_HEADS)
HEAD_W = 2 * HEAD_DIM
NUM_BUCKETS = 32
MAX_DISTANCE = 128
CHUNK = 64
EPS = 1e-6
NEG_INF = -1e30

ADA_ROWS = 16
ADA_PROMPT_ROW0 = 8

VMEM_LIMIT_BYTES = 60 * 1024 * 1024
SUBLANES = 8
LANES = 128

FFN_ROWS = 512
FFN_CHUNK = 256
LRU_ROWS = 256
LRU_PERM_ROWS = 256
POOL_ROWS = 1024
QKV_ROWS = 512
ATT_TQ = 1024
ATT_TK = 512
BIAS_TILE_ROWS = 128
ATT_ROW_CHUNK = 256
VT_ROWS = HEAD_W + 16
LOG2E = math.log2(math.e)
OUT_ROWS = 1024


def _cparams():
    return pltpu.CompilerParams(
        dimension_semantics=("arbitrary", "arbitrary"),
        vmem_limit_bytes=VMEM_LIMIT_BYTES,
    )


def _sigmoid(x):
    return 0.5 * jnp.tanh(0.5 * x) + 0.5


def _modulate(x, g, shift, scale):
    ms = jnp.mean(x * x, axis=-1, keepdims=True)
    y = x * lax.rsqrt(ms + EPS)
    return (y * g) * (1.0 + scale) + shift


def _dot(a, b):
    return jnp.dot(a, b, preferred_element_type=F32)


def _dot_nt(a, b):
    return lax.dot_general(a, b, (((1,), (1,)), ((), ())), preferred_element_type=F32)


def _adaln_kernel(c_ref, w_ref, b_ref, o_ref):
    c = c_ref[...]
    s = c * _sigmoid(c)
    y = _dot(s.astype(BF16), w_ref[0].astype(BF16))
    o_ref[0, 0] = y + b_ref[0, 0]


def _adaln(c_all, ada_w, ada_b):
    n_vec = 9
    out = pl.pallas_call(
        _adaln_kernel,
        grid=(DEPTH, n_vec),
        in_specs=[
            pl.BlockSpec((ADA_ROWS, D_MODEL), lambda l, k: (0, 0)),
            pl.BlockSpec((1, D_MODEL, D_MODEL), lambda l, k: (l, 0, k)),
            pl.BlockSpec((1, 1, 1, D_MODEL), lambda l, k: (l, k, 0, 0)),
        ],
        out_specs=pl.BlockSpec((1, 1, ADA_ROWS, D_MODEL), lambda l, k: (l, k, 0, 0)),
        out_shape=jax.ShapeDtypeStruct((DEPTH, n_vec, ADA_ROWS, D_MODEL), F32),
        compiler_params=_cparams(),
        name="adaln",
    )(c_all, ada_w, ada_b.reshape(DEPTH, n_vec, 1, D_MODEL))
    return out.reshape(DEPTH, n_vec, ADA_ROWS, 1, D_MODEL)


class _Rows:
    def __init__(self, batch, seq, rows, prompt, seqs_per_block=1):
        self.prompt = prompt
        if prompt:
            assert seq % rows == 0 and batch % seqs_per_block == 0
            assert ADA_PROMPT_ROW0 % seqs_per_block == 0
            self.nb, self.r = seqs_per_block, rows
            self.grid = (batch // seqs_per_block, seq // rows)
        else:
            self.nb, self.r = batch, seq
            self.grid = (1, 1)

    def act(self, width):
        return pl.BlockSpec((self.nb, self.r, width), lambda b, t: (b, t, 0))

    def per_seq(self, rows, width):
        return pl.BlockSpec((self.nb, rows, width), lambda b, t: (b, 0, 0))

    def mod(self, layer, k):
        if self.prompt:
            first = ADA_PROMPT_ROW0 // self.nb
            return pl.BlockSpec((1, 1, self.nb, 1, D_MODEL), lambda b, t: (layer, k, first + b, 0, 0))
        return pl.BlockSpec((1, 1, self.nb, 1, D_MODEL), lambda b, t: (layer, k, 0, 0, 0))


def _const_spec(shape, index, single_buffer=False):
    if single_buffer:
        return pl.BlockSpec(shape, lambda b, t: index, pipeline_mode=pl.Buffered(1))
    return pl.BlockSpec(shape, lambda b, t: index)


def _ffn_rows(x_ref, sh_ref, sc_ref, gt_ref, g_ref, win_b, wout_b, o_ref, before_chunk=None):
    x = x_ref[...]
    nb, r, _ = x.shape
    u = _modulate(x, g_ref[0, 0], sh_ref[0, 0], sc_ref[0, 0])
    ub = u.reshape(nb * r, D_MODEL).astype(BF16)
    acc = None
    for c in range(D_FF // FFN_CHUNK):
        if before_chunk is not None:
            before_chunk(c)
        lo = c * FFN_CHUNK
        a = _dot(ub, win_b[:, lo:lo + FFN_CHUNK])
        b = _dot(ub, win_b[:, D_FF + lo:D_FF + lo + FFN_CHUNK])
        h = ((a * _sigmoid(a)) * b).astype(BF16)
        y = _dot(h, wout_b[lo:lo + FFN_CHUNK, :])
        acc = y if acc is None else acc + y
    o_ref[...] = x + (0.5 * gt_ref[0, 0]) * acc.reshape(nb, r, D_MODEL)


def _ffn_kernel(xp_ref, xs_ref, shp_ref, scp_ref, gtp_ref, shs_ref, scs_ref, gts_ref, g_ref,
                win_hbm, wout_hbm, op_ref, os_ref, win_b, wout_b, gate_stage, up_stage, down_stage,
                sems, *, n_prompt_tiles, layer, which):
    i = pl.program_id(0)
    n_chunks = D_FF // FFN_CHUNK

    def chunk_copies(c, slot):
        lo = c * FFN_CHUNK
        return (
            pltpu.make_async_copy(win_hbm.at[layer, which, :, pl.ds(lo, FFN_CHUNK)],
                                  gate_stage.at[slot], sems.at[0, slot]),
            pltpu.make_async_copy(win_hbm.at[layer, which, :, pl.ds(D_FF + lo, FFN_CHUNK)],
                                  up_stage.at[slot], sems.at[1, slot]),
            pltpu.make_async_copy(wout_hbm.at[layer, which, pl.ds(lo, FFN_CHUNK), :],
                                  down_stage.at[slot], sems.at[2, slot]),
        )

    def fetch_chunk(c):
        slot = c % 2
        if c + 1 < n_chunks:
            for copy in chunk_copies(c + 1, 1 - slot):
                copy.start()
        for copy in chunk_copies(c, slot):
            copy.wait()
        lo = c * FFN_CHUNK
        win_b[:, lo:lo + FFN_CHUNK] = gate_stage[slot].astype(BF16)
        win_b[:, D_FF + lo:D_FF + lo + FFN_CHUNK] = up_stage[slot].astype(BF16)
        wout_b[lo:lo + FFN_CHUNK, :] = down_stage[slot].astype(BF16)

    @pl.when(i == 0)
    def _():
        for copy in chunk_copies(0, 0):
            copy.start()
        _ffn_rows(xp_ref, shp_ref, scp_ref, gtp_ref, g_ref, win_b, wout_b, op_ref, fetch_chunk)

    @pl.when((i > 0) & (i < n_prompt_tiles))
    def _():
        _ffn_rows(xp_ref, shp_ref, scp_ref, gtp_ref, g_ref, win_b, wout_b, op_ref)

    @pl.when(i == n_prompt_tiles)
    def _():
        _ffn_rows(xs_ref, shs_ref, scs_ref, gts_ref, g_ref, win_b, wout_b, os_ref)


def _ffn(xp, xs, mods, norm_g4, w_in, w_out, layer, which):
    k0 = 0 if which == 0 else 6
    g_idx = 0 if which == 0 else 2
    B, S, _ = xp.shape
    SB, T, _ = xs.shape
    per_seq = S // FFN_ROWS
    n = B * per_seq
    assert n > 1
    tile = lambda i: jnp.minimum(i, n - 1)
    prompt_act = pl.BlockSpec((1, FFN_ROWS, D_MODEL), lambda i: (tile(i) // per_seq, tile(i) % per_seq, 0))
    sample_act = pl.BlockSpec((SB, T, D_MODEL), lambda i: (0, 0, 0))
    prompt_mod = lambda k: pl.BlockSpec(
        (1, 1, 1, 1, D_MODEL), lambda i: (layer, k, ADA_PROMPT_ROW0 + tile(i) // per_seq, 0, 0))
    sample_mod = lambda k: pl.BlockSpec((1, 1, SB, 1, D_MODEL), lambda i: (layer, k, 0, 0, 0))
    return pl.pallas_call(
        functools.partial(_ffn_kernel, n_prompt_tiles=n, layer=layer, which=which),
        grid=(n + 1,),
        in_specs=[
            prompt_act, sample_act,
            prompt_mod(k0), prompt_mod(k0 + 1), prompt_mod(k0 + 2),
            sample_mod(k0), sample_mod(k0 + 1), sample_mod(k0 + 2),
            pl.BlockSpec((1, 1, 1, D_MODEL), lambda i: (layer, g_idx, 0, 0)),
            pl.BlockSpec(memory_space=pl.ANY),
            pl.BlockSpec(memory_space=pl.ANY),
        ],
        out_specs=[prompt_act, sample_act],
        out_shape=[jax.ShapeDtypeStruct(xp.shape, F32), jax.ShapeDtypeStruct(xs.shape, F32)],
        scratch_shapes=[
            pltpu.VMEM((D_MODEL, 2 * D_FF), BF16), pltpu.VMEM((D_FF, D_MODEL), BF16),
            pltpu.VMEM((2, D_MODEL, FFN_CHUNK), F32), pltpu.VMEM((2, D_MODEL, FFN_CHUNK), F32),
            pltpu.VMEM((2, FFN_CHUNK, D_MODEL), F32),
            pltpu.SemaphoreType.DMA((3, 2)),
        ],
        compiler_params=pltpu.CompilerParams(
            dimension_semantics=("arbitrary",), vmem_limit_bytes=VMEM_LIMIT_BYTES),
        name="ffn",
    )(xp, xs, mods, mods, mods, mods, mods, mods, norm_g4, w_in, w_out)


def _group_scan(a, b):
    row = lax.broadcasted_iota(jnp.int32, a.shape, 1)
    d = 1
    while d < SUBLANES:
        keep = row >= d
        a_sh = jnp.where(keep, pltpu.roll(a, d, axis=1), 1.0)
        b_sh = jnp.where(keep, pltpu.roll(b, d, axis=1), 0.0)
        b = a * b_sh + b
        a = a * a_sh
        d *= 2
    return a, b


def _linear_scan(a, b, h_prev, a_ref, h_ref):
    nb, r, C = a.shape
    first = slice(0, SUBLANES)
    a_run, h_run = a[:, first], b[:, first]
    a_ref[:, first, :] = a_run
    h_ref[:, first, :] = h_run
    for g0 in range(SUBLANES, r, SUBLANES):
        rows = slice(g0, g0 + SUBLANES)
        h_run = a[:, rows] * h_run + b[:, rows]
        a_run = a[:, rows] * a_run
        a_ref[:, rows, :] = a_run
        h_ref[:, rows, :] = h_run
    a_inc, h_inc = _group_scan(a_run, h_run)
    seg_end = a_inc * h_prev + h_inc
    row = lax.broadcasted_iota(jnp.int32, seg_end.shape, 1)
    seg_in = jnp.where(row == 0, h_prev, pltpu.roll(seg_end, 1, axis=1))
    for g0 in range(0, r, SUBLANES):
        rows = slice(g0, g0 + SUBLANES)
        h_ref[:, rows, :] = h_ref[:, rows, :] + a_ref[:, rows, :] * seg_in
    return seg_end[:, SUBLANES - 1:, :]


LRU_TAIL = (CONV_W - 1) * SUBLANES


def _lru_kernel(x_ref, sh_ref, sc_ref, gt_ref, g_ref, win_ref, cw_ref, cb_ref, gw_ref, gab_ref,
                gxb_ref, lam_ref, wout_ref, conv0_ref, h0_ref, perm_ref, unperm_ref,
                o_ref, hlast_ref, tail_ref, h_scr, tail_scr, arun_scr, hall_scr):
    @pl.when(pl.program_id(1) == 0)
    def _():
        h_scr[...] = h0_ref[...]
        tail_scr[...] = conv0_ref[...]

    nb_all, r, _ = x_ref.shape
    assert r // SUBLANES >= CONV_W - 1
    C = D_RNN
    nb = perm_ref.shape[0] // r
    for b0 in range(0, nb_all, nb):
        seq = slice(b0, b0 + nb)
        x = x_ref[seq]
        u = _modulate(x, g_ref[0, 0], sh_ref[0, 0, seq], sc_ref[0, 0, seq])
        ub = _dot(perm_ref[...], u.reshape(nb * r, D_MODEL).astype(BF16)).astype(BF16)
        gate_br = _dot(ub, win_ref[0, :, :C])
        x_br = _dot(ub, win_ref[0, :, C:]).reshape(nb, r, C)

        last = x_br[:, r - LRU_TAIL:, :]
        groups = (nb * (CONV_W - 1), SUBLANES, C)
        row = lax.broadcasted_iota(jnp.int32, groups, 1)
        wrap = jnp.where(row == 0, pltpu.roll(tail_scr[seq].reshape(groups), 1, axis=1),
                         pltpu.roll(last.reshape(groups), 1, axis=1)).reshape(nb, LRU_TAIL, C)
        tail_scr[seq] = last
        tail_ref[seq] = last
        cw = cw_ref[0]
        xc = cb_ref[0] + cw[CONV_W - 1:CONV_W] * x_br
        for d in range(1, CONV_W):
            shifted = jnp.concatenate([wrap[:, LRU_TAIL - d * SUBLANES:], x_br[:, :r - d * SUBLANES]],
                                      axis=1)
            xc = xc + cw[CONV_W - 1 - d:CONV_W - d] * shifted

        xc2 = xc.reshape(nb * r, C)
        xcb = xc2.astype(BF16)
        sw = C // LRU_SUPER
        gates = [_dot(xcb[:, s * sw:(s + 1) * sw], gw_ref[0, s]) for s in range(LRU_SUPER)]
        rg = _sigmoid(jnp.concatenate([g[:, :sw] for g in gates], axis=-1) + gab_ref[0])
        ig = _sigmoid(jnp.concatenate([g[:, sw:] for g in gates], axis=-1) + gxb_ref[0])
        nl = -lam_ref[0]
        softplus = jnp.maximum(nl, 0.0) + jnp.log1p(jnp.exp(-jnp.abs(nl)))
        log_a = (-LRU_C * rg) * softplus
        a = jnp.exp(log_a)
        th = jnp.tanh(log_a)
        one_minus_a2 = (-2.0 * th) / (1.0 - th)
        root = jnp.where(one_minus_a2 > 0.0, one_minus_a2 * lax.rsqrt(one_minus_a2), 0.0)
        b_in = root * (ig * xc2)

        h_last = _linear_scan(a.reshape(nb, r, C), b_in.reshape(nb, r, C), h_scr[seq],
                              arun_scr.at[seq], hall_scr.at[seq])
        h_scr[seq] = h_last
        hlast_ref[seq] = h_last

        gb = gate_br
        gelu = 0.5 * gb * (1.0 + jnp.tanh(math.sqrt(2.0 / math.pi) * (gb + 0.044715 * (gb * gb * gb))))
        z = (gelu * hall_scr[seq].reshape(nb * r, C)).astype(BF16)
        y = _dot(_dot(unperm_ref[...], z).astype(BF16), wout_ref[0])
        o_ref[seq] = x + gt_ref[0, 0, seq] * y.reshape(nb, r, D_MODEL)


def _lru(x, rows, mods, norm_g4, layer, j, w_in, conv_w, conv_b, gate_w, ga_b, gx_b, lam, w_out,
         conv0, h0):
    nbt = x.shape[0]
    C = D_RNN
    seqs = max(1, LRU_PERM_ROWS // rows.r)
    assert rows.nb % seqs == 0
    n_rows, G = seqs * rows.r, rows.r // SUBLANES
    b_idx, g_idx, i_idx = np.meshgrid(np.arange(seqs), np.arange(G), np.arange(SUBLANES), indexing='ij')
    source = (b_idx * rows.r + i_idx * G + g_idx).reshape(n_rows)
    perm = np.zeros((n_rows, n_rows), np.float32)
    perm[np.arange(n_rows), source] = 1.0
    vec = lambda: _const_spec((1, 1, C), (j, 0, 0))
    out, h_last, tail = pl.pallas_call(
        _lru_kernel,
        grid=rows.grid,
        in_specs=[
            rows.act(D_MODEL),
            rows.mod(layer, 3), rows.mod(layer, 4), rows.mod(layer, 5),
            _const_spec((1, 1, 1, D_MODEL), (layer, 1, 0, 0)),
            _const_spec((1, D_MODEL, 2 * C), (j, 0, 0)),
            _const_spec((1, CONV_W, C), (j, 0, 0)),
            vec(),
            _const_spec((1, LRU_SUPER, C // LRU_SUPER, 2 * C // LRU_SUPER), (j, 0, 0, 0)),
            vec(), vec(), vec(),
            _const_spec((1, C, D_MODEL), (j, 0, 0)),
            rows.per_seq(LRU_TAIL, C),
            rows.per_seq(1, C),
            _const_spec((n_rows, n_rows), (0, 0)),
            _const_spec((n_rows, n_rows), (0, 0)),
        ],
        out_specs=[rows.act(D_MODEL), rows.per_seq(1, C), rows.per_seq(LRU_TAIL, C)],
        out_shape=[
            jax.ShapeDtypeStruct(x.shape, F32),
            jax.ShapeDtypeStruct((nbt, 1, C), F32),
            jax.ShapeDtypeStruct((nbt, LRU_TAIL, C), F32),
        ],
        scratch_shapes=[pltpu.VMEM((rows.nb, 1, C), F32), pltpu.VMEM((rows.nb, LRU_TAIL, C), F32),
                        pltpu.VMEM((rows.nb, rows.r, C), F32), pltpu.VMEM((rows.nb, rows.r, C), F32)],
        compiler_params=_cparams(),
        name="lru",
    )(x, mods, mods, mods, norm_g4, w_in, conv_w, conv_b, gate_w, ga_b, gx_b, lam, w_out, conv0, h0,
      jnp.asarray(perm, BF16), jnp.asarray(perm.T, BF16))
    return out, h_last[:, 0], tail[:, SUBLANES - 1::SUBLANES]


POOL_PAD = 16


def _pool_kernel(x_ref, sh_ref, sc_ref, gt_ref, g_ref, hist0_ref, w_ref, b_ref, ps_ref,
                 o_ref, st_ref, hist_scr, *, t_base):
    t = pl.program_id(1)

    @pl.when(t == 0)
    def _():
        hist_scr[...] = hist0_ref[...]

    x = x_ref[...]
    nb, r, _ = x.shape
    u = _modulate(x, g_ref[0, 0], sh_ref[0, 0], sc_ref[0, 0])
    ext = jnp.concatenate([hist_scr[...], u], axis=1)
    new_hist = u[:, r - POOL_PAD:, :]
    hist_scr[...] = new_hist
    st_ref[...] = new_hist

    row = lax.broadcasted_iota(jnp.int32, (1, r, 1), 1)
    t_glob = t_base + t * r + row
    gate = gt_ref[0, 0]
    s = ext
    for g, wnd in enumerate(POOL_WINDOWS):
        lo = g * POOL_GW
        s = s[:, :, (POOL_GW if g > 0 else 0):]
        s = s + pltpu.roll(s, wnd // 2, axis=1)
        cnt = jnp.minimum(t_glob + 1, wnd).astype(F32)
        mean = s[:, POOL_PAD:, :POOL_GW] / cnt
        d = mean - u[:, :, lo:lo + POOL_GW]
        y = _dot(d.reshape(nb * r, POOL_GW).astype(BF16), w_ref[0, g]).reshape(nb, r, POOL_GW)
        y = (y + b_ref[0, :, lo:lo + POOL_GW]) * ps_ref[0, :, lo:lo + POOL_GW]
        o_ref[:, :, lo:lo + POOL_GW] = x[:, :, lo:lo + POOL_GW] + gate[:, :, lo:lo + POOL_GW] * y


def _pool(x, rows, mods, norm_g4, layer, j, hist0, w, b, scale, t_base):
    nbt = x.shape[0]
    out, st = pl.pallas_call(
        functools.partial(_pool_kernel, t_base=t_base),
        grid=rows.grid,
        in_specs=[
            rows.act(D_MODEL),
            rows.mod(layer, 3), rows.mod(layer, 4), rows.mod(layer, 5),
            _const_spec((1, 1, 1, D_MODEL), (layer, 1, 0, 0)),
            rows.per_seq(POOL_PAD, D_MODEL),
            _const_spec((1, len(POOL_WINDOWS), POOL_GW, POOL_GW), (j, 0, 0, 0)),
            _const_spec((1, 1, D_MODEL), (j, 0, 0)),
            _const_spec((1, 1, D_MODEL), (j, 0, 0)),
        ],
        out_specs=[rows.act(D_MODEL), rows.per_seq(POOL_PAD, D_MODEL)],
        out_shape=[jax.ShapeDtypeStruct(x.shape, F32),
                   jax.ShapeDtypeStruct((nbt, POOL_PAD, D_MODEL), F32)],
        scratch_shapes=[pltpu.VMEM((rows.nb, POOL_PAD, D_MODEL), F32)],
        compiler_params=_cparams(),
        name="pool",
    )(x, mods, mods, mods, norm_g4, hist0, w, b, scale)
    return out, st[:, POOL_PAD - POOL_HIST:]


def _qkv_kernel(x_ref, sh_ref, sc_ref, g_ref, w_ref, qg_ref, kg_ref, seg_ref,
                q_ref, kf_ref, kb_ref, vf_ref, vb_ref, *, v_transposed):
    x = x_ref[...]
    nb, r, _ = x.shape
    u = _modulate(x, g_ref[0, 0], sh_ref[0, 0], sc_ref[0, 0])
    ub = u.reshape(nb * r, D_MODEL).astype(BF16)
    q = _dot(ub, w_ref[0, :, :D_MODEL])
    k = _dot(ub, w_ref[0, :, D_MODEL:2 * D_MODEL])
    v = _dot(ub, w_ref[0, :, 2 * D_MODEL:])

    def head_norm(z, gain):
        ms = _dot((z * z).astype(BF16), seg_ref[...])
        return (z * lax.rsqrt(ms + EPS)) * gain

    qn = head_norm(q, qg_ref[...]) * (HEAD_DIM ** -0.5 * LOG2E)
    kn = head_norm(k, kg_ref[...])
    lane = lax.broadcasted_iota(jnp.int32, (1, D_MODEL), 1)
    first = (lane % HEAD_W) < HEAD_DIM
    q_ref[:, 0] = jnp.where(first, qn, 0.0).astype(BF16).reshape(nb, r, D_MODEL)
    q_ref[:, 1] = jnp.where(first, 0.0, qn).astype(BF16).reshape(nb, r, D_MODEL)
    kf_ref[...] = kn.reshape(nb, r, D_MODEL)
    vf_ref[...] = v.reshape(nb, r, D_MODEL)
    kb_ref[...] = kn.astype(BF16).reshape(nb, r, D_MODEL)
    if v_transposed:
        for h in range(N_HEADS):
            vb_ref[0, h, :HEAD_W] = v[:, h * HEAD_W:(h + 1) * HEAD_W].T.astype(BF16)
            vb_ref[0, h, HEAD_W:] = jnp.ones((VT_ROWS - HEAD_W, r), BF16)
    else:
        vb_ref[...] = v.astype(BF16).reshape(nb, r, D_MODEL)


def _qkv(x, rows, mods, norm_g4, layer, j, w_in, qg, kg, seg, v_transposed):
    nbt, seq, _ = x.shape
    q_spec = pl.BlockSpec((rows.nb, 2, rows.r, D_MODEL), lambda b, t: (b, 0, t, 0))
    if v_transposed:
        assert rows.nb == 1
        vb_spec = pl.BlockSpec((1, N_HEADS, VT_ROWS, rows.r), lambda b, t: (b, 0, 0, t))
        vb_shape = jax.ShapeDtypeStruct((nbt, N_HEADS, VT_ROWS, seq), BF16)
    else:
        vb_spec = rows.act(D_MODEL)
        vb_shape = jax.ShapeDtypeStruct(x.shape, BF16)
    return pl.pallas_call(
        functools.partial(_qkv_kernel, v_transposed=v_transposed),
        grid=rows.grid,
        in_specs=[
            rows.act(D_MODEL),
            rows.mod(layer, 3), rows.mod(layer, 4),
            _const_spec((1, 1, 1, D_MODEL), (layer, 1, 0, 0)),
            _const_spec((1, D_MODEL, 3 * D_MODEL), (j, 0, 0)),
            _const_spec((1, D_MODEL), (0, 0)),
            _const_spec((1, D_MODEL), (0, 0)),
            _const_spec((D_MODEL, D_MODEL), (0, 0)),
        ],
        out_specs=[q_spec, rows.act(D_MODEL), rows.act(D_MODEL), rows.act(D_MODEL), vb_spec],
        out_shape=[
            jax.ShapeDtypeStruct((nbt, 2, seq, D_MODEL), BF16),
            jax.ShapeDtypeStruct(x.shape, F32),
            jax.ShapeDtypeStruct(x.shape, BF16),
            jax.ShapeDtypeStruct(x.shape, F32),
            vb_shape,
        ],
        compiler_params=_cparams(),
        name="qkv",
    )(x, mods, mods, norm_g4, w_in, qg, kg, seg)


_BUCKET_EDGES = (12, 16, 23, 32, 46, 64, 91)
_FAR_BUCKET = NUM_BUCKETS // 2 - 1


def _bucket_of(rel):
    n = abs(rel)
    small = NUM_BUCKETS // 4
    b = n if n < small else small + sum(n >= e for e in _BUCKET_EDGES)
    return b + (NUM_BUCKETS // 2 if rel > 0 else 0)


_BUCKET_RUNS = tuple((r, _bucket_of(r)) for r in range(-_BUCKET_EDGES[-1] + 1, _BUCKET_EDGES[-1] + 1)
                     if _bucket_of(r) != _bucket_of(r - 1))


def _bias_kernel(rb_ref, o_ref, *, rel0, mask_from, keys_on_rows, tile, tiles):
    h = pl.program_id(0)
    R, W = tile
    for row0, col0, fill in tiles:
        window = (0, slice(row0, row0 + R), slice(col0, col0 + W))
        if fill is not None:
            o_ref[window] = jnp.full((R, W), fill, F32)
            continue
        i = lax.broadcasted_iota(jnp.int32, (R, W), 0) + row0
        jj = lax.broadcasted_iota(jnp.int32, (R, W), 1) + col0
        key, qry = (i, jj) if keys_on_rows else (jj, i)
        rel = key - qry + rel0
        out = jnp.full((R, W), rb_ref[_FAR_BUCKET, h], F32)
        for first_rel, bucket in _BUCKET_RUNS:
            out = jnp.where(rel >= first_rel, rb_ref[bucket, h], out)
        out = (out - rb_ref[_FAR_BUCKET, h]) * LOG2E
        if mask_from is not None:
            visible = ((key - mask_from) // CHUNK <= qry // CHUNK) | (key < mask_from)
            out = jnp.where(visible, out, NEG_INF)
        o_ref[window] = out


def _bias_tile(rel_bias, rows, width, rel0, mask_from, keys_on_rows=False):
    R = BIAS_TILE_ROWS if rows % BIAS_TILE_ROWS == 0 else rows
    W = LANES if width % LANES == 0 else width
    tiles = []
    for row0 in range(0, rows, R):
        for col0 in range(0, width, W):
            key0, qry0 = (row0, col0) if keys_on_rows else (col0, row0)
            key1, qry1 = (row0 + R, col0 + W) if keys_on_rows else (col0 + W, row0 + R)
            fill = None
            unmasked = mask_from is None or key1 <= mask_from
            if unmasked and (key1 - 1) - qry0 + rel0 <= -_BUCKET_EDGES[-1]:
                fill = 0.0
            if mask_from is not None and key0 >= mask_from and \
                    (key0 - mask_from) // CHUNK > (qry1 - 1) // CHUNK:
                fill = NEG_INF
            tiles.append((row0, col0, fill))
    return pl.pallas_call(
        functools.partial(_bias_kernel, rel0=rel0, mask_from=mask_from, keys_on_rows=keys_on_rows,
                          tile=(R, W), tiles=tuple(tiles)),
        grid=(N_HEADS,),
        in_specs=[pl.BlockSpec(memory_space=pltpu.SMEM)],
        out_specs=pl.BlockSpec((1, rows, width), lambda h: (h, 0, 0)),
        out_shape=jax.ShapeDtypeStruct((N_HEADS, rows, width), F32),
        name="bias_tile",
    )(rel_bias)


def _diff_lambda(lp, lam_init):
    s1 = jnp.sum(lp[0:1] * lp[1:2], axis=-1, keepdims=True)
    s2 = jnp.sum(lp[2:3] * lp[3:4], axis=-1, keepdims=True)
    return jnp.exp(s1) - jnp.exp(s2) + lam_init


def _attn_finish(acc, l, nq, lam, subg, lam_init):
    o = acc / l
    o = o[:nq] - lam * o[nq:]
    ms = jnp.mean(o * o, axis=-1, keepdims=True)
    return ((o * lax.rsqrt(ms + EPS)) * subg) * (1.0 - lam_init)


def _attn_prompt_kernel(q_ref, k_ref, vt_ref, bias_ref, lam_ref, subg_ref, o_ref,
                        qt_scr, s0_scr, s1_scr, mx0_scr, mx1_scr, m_scr, acc_scr, *, lam_init):
    qi = pl.program_id(2)
    tq, tk = ATT_TQ, ATT_TK
    width = 2 * tq
    n_blocks = (qi + 1) * (tq // tk)
    qt_scr[:, :tq] = q_ref[0, 0].astype(F32).T.astype(BF16)
    qt_scr[:, tq:] = q_ref[0, 1].astype(F32).T.astype(BF16)
    m_scr[...] = jnp.full(m_scr.shape, NEG_INF, F32)
    acc_scr[...] = jnp.zeros(acc_scr.shape, F32)

    def block_start(c):
        return c * tk

    def scores_rows(c, s_ref, near, r0):
        rows = slice(r0, r0 + ATT_ROW_CHUNK)
        start = pl.multiple_of(block_start(c) + r0, ATT_ROW_CHUNK)
        s = _dot(k_ref[0, pl.ds(start, ATT_ROW_CHUNK), :], qt_scr[...])
        if near is not None:
            bias = bias_ref[0, near, rows, :]
            s = s + jnp.concatenate([bias, bias], axis=1)
        s_ref[rows, :] = s
        return jnp.max(s.reshape(ATT_ROW_CHUNK // SUBLANES, SUBLANES, width), axis=0)

    def new_max(mx_ref):
        m_old = m_scr[...]
        m_new = jnp.maximum(m_old, jnp.max(mx_ref[...], axis=0, keepdims=True))
        m_scr[...] = m_new
        return m_new, jnp.exp2(m_old - m_new)

    def absorb_rows(c, s_ref, m_new, r0):
        p = jnp.exp2((s_ref[r0:r0 + ATT_ROW_CHUNK, :] - m_new).astype(BF16))
        start = pl.multiple_of(block_start(c) + r0, ATT_ROW_CHUNK)
        return _dot(vt_ref[0, 0, :, pl.ds(start, ATT_ROW_CHUNK)], p)

    def rescale_add(alpha, parts):
        acc_scr[...] = alpha * acc_scr[...] + functools.reduce(jnp.add, parts)

    chunks = range(0, tk, ATT_ROW_CHUNK)

    def scores(c, s_ref, mx_ref, near):
        mx_ref[...] = functools.reduce(jnp.maximum, [scores_rows(c, s_ref, near, r0) for r0 in chunks])

    def absorb(c, s_ref, mx_ref):
        m_new, alpha = new_max(mx_ref)
        rescale_add(alpha, [absorb_rows(c, s_ref, m_new, r0) for r0 in chunks])

    def overlap(c_new, s_new, mx_new, near, c_old, s_old, mx_old):
        scores(c_new, s_new, mx_new, near)
        absorb(c_old, s_old, mx_old)

    @pl.when(qi == 0)
    def _():
        scores(0, s0_scr, mx0_scr, 1)

    @pl.when(qi > 0)
    def _():
        scores(0, s0_scr, mx0_scr, None)

    def far_trip(t, carry):
        c = 2 * t
        overlap(c + 1, s1_scr, mx1_scr, None, c, s0_scr, mx0_scr)
        overlap(c + 2, s0_scr, mx0_scr, None, c + 1, s1_scr, mx1_scr)
        return carry

    lax.fori_loop(0, jnp.maximum(n_blocks // 2 - 2, 0), far_trip, 0)

    @pl.when(qi > 0)
    def _():
        c = n_blocks - 4
        overlap(c + 1, s1_scr, mx1_scr, 0, c, s0_scr, mx0_scr)
        overlap(c + 2, s0_scr, mx0_scr, 1, c + 1, s1_scr, mx1_scr)

    overlap(n_blocks - 1, s1_scr, mx1_scr, 2, n_blocks - 2, s0_scr, mx0_scr)
    absorb(n_blocks - 1, s1_scr, mx1_scr)

    lam = _diff_lambda(lam_ref[0], lam_init)
    acc = acc_scr[...]
    o = acc[:HEAD_W] / acc[HEAD_W:HEAD_W + 1]
    o = o[:, :tq] - lam * o[:, tq:]
    ms = jnp.mean(o * o, axis=0, keepdims=True)
    on = ((o * lax.rsqrt(ms + EPS)) * subg_ref[...]) * (1.0 - lam_init)
    o_ref[0] = on.T.astype(BF16)


def _attn_prompt(q, kb, vt, bias, lam_p, subg_col, j, lam_init):
    nbt, _, seq, _ = q.shape
    tq, tk = ATT_TQ, ATT_TK
    return pl.pallas_call(
        functools.partial(_attn_prompt_kernel, lam_init=lam_init),
        grid=(nbt, N_HEADS, seq // tq),
        in_specs=[
            pl.BlockSpec((1, 2, tq, HEAD_W), lambda b, h, i: (b, 0, i, h)),
            pl.BlockSpec((1, seq, HEAD_W), lambda b, h, i: (b, 0, h)),
            pl.BlockSpec((1, 1, VT_ROWS, seq), lambda b, h, i: (b, h, 0, 0)),
            pl.BlockSpec((1, 3, tk, tq), lambda b, h, i: (h, 0, 0, 0)),
            pl.BlockSpec((1, 4, HEAD_DIM), lambda b, h, i: (j, 0, 0)),
            pl.BlockSpec((HEAD_W, 1), lambda b, h, i: (0, 0)),
        ],
        out_specs=pl.BlockSpec((1, tq, HEAD_W), lambda b, h, i: (b, i, h)),
        out_shape=jax.ShapeDtypeStruct((nbt, seq, D_MODEL), BF16),
        scratch_shapes=[pltpu.VMEM((HEAD_W, 2 * tq), BF16),
                        pltpu.VMEM((tk, 2 * tq), F32), pltpu.VMEM((tk, 2 * tq), F32),
                        pltpu.VMEM((SUBLANES, 2 * tq), F32), pltpu.VMEM((SUBLANES, 2 * tq), F32),
                        pltpu.VMEM((1, 2 * tq), F32),
                        pltpu.VMEM((VT_ROWS, 2 * tq), F32)],
        compiler_params=pltpu.CompilerParams(
            dimension_semantics=("arbitrary", "arbitrary", "arbitrary"),
            vmem_limit_bytes=VMEM_LIMIT_BYTES),
        name="attn_prompt",
    )(q, kb, vt, bias, lam_p, subg_col)


def _attn_sample_kernel(q_ref, ck_ref, cv_ref, kn_ref, vn_ref, bc_ref, bn_ref, lam_ref, subg_ref,
                        o_ref, *, lam_init):
    nq = q_ref.shape[2]
    lam = _diff_lambda(lam_ref[0], lam_init)
    for h in range(N_HEADS):
        cols = slice(h * HEAD_W, (h + 1) * HEAD_W)
        qq = q_ref[0, :, :, cols].reshape(2 * nq, HEAD_W)
        kc = ck_ref[0, 0, :, h, :].astype(BF16)
        vc = cv_ref[0, 0, :, h, :].astype(BF16)
        kn = kn_ref[0, :, cols]
        vn = vn_ref[0, :, cols]
        past = kc.shape[0]
        s_c = (_dot_nt(qq, kc).reshape(2, nq, past) + bc_ref[h][None]).reshape(2 * nq, past)
        s_n = (_dot_nt(qq, kn).reshape(2, nq, nq) + bn_ref[h][None]).reshape(2 * nq, nq)
        m = jnp.maximum(jnp.max(s_c, axis=-1, keepdims=True), jnp.max(s_n, axis=-1, keepdims=True))
        p_c = jnp.exp2(s_c - m)
        p_n = jnp.exp2(s_n - m)
        l = jnp.sum(p_c, axis=-1, keepdims=True) + jnp.sum(p_n, axis=-1, keepdims=True)
        acc = _dot(p_c.astype(BF16), vc) + _dot(p_n.astype(BF16), vn)
        o_ref[0, :, cols] = _attn_finish(acc, l, nq, lam, subg_ref[...], lam_init).astype(BF16)


def _attn_sample(q, ck, cv, kb, vb, bias_c, bias_n, lam_p, subg, j, lam_init):
    nbt, _, nq, _ = q.shape
    past = ck.shape[2]
    cache_spec = pl.BlockSpec((1, 1, past, N_HEADS, HEAD_W), lambda b: (j, b, 0, 0, 0))
    return pl.pallas_call(
        functools.partial(_attn_sample_kernel, lam_init=lam_init),
        grid=(nbt,),
        in_specs=[
            pl.BlockSpec((1, 2, nq, D_MODEL), lambda b: (b, 0, 0, 0)),
            cache_spec, cache_spec,
            pl.BlockSpec((1, nq, D_MODEL), lambda b: (b, 0, 0)),
            pl.BlockSpec((1, nq, D_MODEL), lambda b: (b, 0, 0)),
            pl.BlockSpec((N_HEADS, nq, past), lambda b: (0, 0, 0)),
            pl.BlockSpec((N_HEADS, nq, nq), lambda b: (0, 0, 0)),
            pl.BlockSpec((1, 4, HEAD_DIM), lambda b: (j, 0, 0)),
            pl.BlockSpec((1, HEAD_W), lambda b: (0, 0)),
        ],
        out_specs=pl.BlockSpec((1, nq, D_MODEL), lambda b: (b, 0, 0)),
        out_shape=jax.ShapeDtypeStruct((nbt, nq, D_MODEL), BF16),
        compiler_params=pltpu.CompilerParams(
            dimension_semantics=("arbitrary",), vmem_limit_bytes=VMEM_LIMIT_BYTES),
        name="attn_sample",
    )(q, ck, cv, kb, vb, bias_c, bias_n, lam_p, subg)


def _proj_res_kernel(x_ref, a_ref, gt_ref, w_ref, o_ref):
    x = x_ref[...]
    nb, r, _ = x.shape
    a = a_ref[...].reshape(nb * r, a_ref.shape[-1])
    y = _dot(a, w_ref[0])
    o_ref[...] = x + gt_ref[0, 0] * y.reshape(nb, r, D_MODEL)


def _proj_res(x, a, rows, mods, layer, j, w):
    return pl.pallas_call(
        _proj_res_kernel,
        grid=rows.grid,
        in_specs=[
            rows.act(D_MODEL), rows.act(a.shape[-1]), rows.mod(layer, 5),
            _const_spec((1,) + w.shape[1:], (j, 0, 0)),
        ],
        out_specs=rows.act(D_MODEL),
        out_shape=jax.ShapeDtypeStruct(x.shape, F32),
        compiler_params=_cparams(),
        name="proj_res",
    )(x, a, mods, w)


def kernel(x_prompt, x_sample, c_prompt, c_sample, state_lru_h, state_lru_conv, state_pool, cache_k, cache_v, ada_w, ada_b, norm_g, ffn_w_in, ffn_w_out, lru_w_in, lru_conv_w, lru_conv_b, lru_ga_w, lru_ga_b, lru_gx_w, lru_gx_b, lru_lambda, lru_w_out, pool_w, pool_b, pool_scale, attn_w_in, attn_q_g, attn_k_g, attn_lambda, attn_sub_g, attn_w_out, rel_bias):
    B, S, _ = x_prompt.shape
    SB, T, _ = x_sample.shape
    P = cache_k.shape[2]
    assert SB == ADA_PROMPT_ROW0 and ADA_PROMPT_ROW0 + B <= ADA_ROWS
    n_a, n_b, n_c = lru_w_in.shape[0], pool_w.shape[0], attn_w_in.shape[0]

    c_all = jnp.concatenate(
        [c_sample, c_prompt, jnp.zeros((ADA_ROWS - SB - B, D_MODEL), F32)], axis=0)
    mods = _adaln(c_all, ada_w, ada_b)
    norm_g4 = norm_g.reshape(DEPTH, 3, 1, D_MODEL)

    lru_in_b = lru_w_in.astype(BF16)
    lru_out_b = lru_w_out.astype(BF16)
    per_super = LRU_BLOCKS // LRU_SUPER
    eye = jnp.eye(per_super, dtype=F32)

    def super_blocks(w):
        w5 = w.reshape(w.shape[0], LRU_SUPER, per_super, LRU_BS, LRU_BS)
        full = jnp.einsum('nshij,hg->nshigj', w5, eye)
        return full.reshape(w.shape[0], LRU_SUPER, per_super * LRU_BS, per_super * LRU_BS)

    lru_gate_b = jnp.concatenate([super_blocks(lru_ga_w), super_blocks(lru_gx_w)], axis=-1).astype(BF16)
    vec3 = lambda v: v.reshape(v.shape[0], 1, v.shape[-1])
    pool_w_b = pool_w.astype(BF16)
    attn_in_b = attn_w_in.astype(BF16)
    attn_out_b = attn_w_out.astype(BF16)
    seg = jnp.kron(jnp.eye(D_MODEL // HEAD_DIM, dtype=F32),
                   jnp.full((HEAD_DIM, HEAD_DIM), 1.0 / HEAD_DIM, F32)).astype(BF16)

    rows_p = {r: _Rows(B, S, r, True) for r in {POOL_ROWS, QKV_ROWS, OUT_ROWS}}
    rows_lru = _Rows(B, S, LRU_ROWS, True, seqs_per_block=B)
    rows_s = _Rows(SB, T, T, False)

    xp, xs = x_prompt, x_sample
    h_p, h_s, cv_p, cv_s, pl_p, pl_s, k_p, v_p, k_s, v_s = ([] for _ in range(10))
    for l in range(DEPTH):
        kind, j = l % 3, l // 3
        xp, xs = _ffn(xp, xs, mods, norm_g4, ffn_w_in, ffn_w_out, l, 0)
        if kind == 0:
            prm = (lru_in_b, lru_conv_w, vec3(lru_conv_b), lru_gate_b, vec3(lru_ga_b),
                   vec3(lru_gx_b), vec3(lru_lambda), lru_out_b)
            xp, hp, bp = _lru(xp, rows_lru, mods, norm_g4, l, j, *prm,
                              jnp.zeros((B, LRU_TAIL, D_RNN), F32), jnp.zeros((B, 1, D_RNN), F32))
            conv0 = jnp.zeros((SB, LRU_TAIL, D_RNN), F32).at[:, SUBLANES - 1::SUBLANES].set(
                state_lru_conv[j])
            xs, hs, bs = _lru(xs, rows_s, mods, norm_g4, l, j, *prm, conv0,
                              state_lru_h[j][:, None, :])
            h_p.append(hp), h_s.append(hs), cv_p.append(bp), cv_s.append(bs)
        elif kind == 1:
            prm = (pool_w_b, vec3(pool_b), vec3(pool_scale))
            xp, sp = _pool(xp, rows_p[POOL_ROWS], mods, norm_g4, l, j,
                           jnp.zeros((B, POOL_PAD, D_MODEL), F32), *prm, 0)
            hist0 = jnp.pad(state_pool[j], ((0, 0), (POOL_PAD - POOL_HIST, 0), (0, 0)))
            xs, ss = _pool(xs, rows_s, mods, norm_g4, l, j, hist0, *prm, POOL_HIST)
            pl_p.append(sp), pl_s.append(ss)
        else:
            lam_init = 0.8 - 0.6 * math.exp(-0.3 * l)
            qg = jnp.tile(attn_q_g[j], D_MODEL // HEAD_DIM)[None]
            kg = jnp.tile(attn_k_g[j], D_MODEL // HEAD_DIM)[None]
            subg = attn_sub_g[j][None]
            q, kf, kb, vf, vt = _qkv(xp, rows_p[QKV_ROWS], mods, norm_g4, l, j, attn_in_b, qg, kg, seg,
                                     True)
            assert ATT_TQ == 2 * ATT_TK
            bias_p = _bias_tile(rel_bias, 3 * ATT_TK, ATT_TQ, -ATT_TK, ATT_TK, keys_on_rows=True)
            bias_p = bias_p.reshape(N_HEADS, 3, ATT_TK, ATT_TQ)
            o = _attn_prompt(q, kb, vt, bias_p, attn_lambda, attn_sub_g[j][:, None], j, lam_init)
            xp = _proj_res(xp, o, rows_p[OUT_ROWS], mods, l, j, attn_out_b)
            k_p.append(kf.reshape(B, S, N_HEADS, HEAD_W)), v_p.append(vf.reshape(B, S, N_HEADS, HEAD_W))
            q, kf, kb, vf, vb = _qkv(xs, rows_s, mods, norm_g4, l, j, attn_in_b, qg, kg, seg, False)
            bias_c = _bias_tile(rel_bias, T, P, -P, None)
            bias_n = _bias_tile(rel_bias, T, T, 0, None)
            o = _attn_sample(q, cache_k, cache_v,
                             kb, vb, bias_c, bias_n, attn_lambda, subg, j, lam_init)
            xs = _proj_res(xs, o, rows_s, mods, l, j, attn_out_b)
            k_s.append(kf.reshape(SB, T, N_HEADS, HEAD_W)), v_s.append(vf.reshape(SB, T, N_HEADS, HEAD_W))
        xp, xs = _ffn(xp, xs, mods, norm_g4, ffn_w_in, ffn_w_out, l, 1)
    return (xp, xs, jnp.stack(h_p), jnp.stack(h_s), jnp.stack(cv_p), jnp.stack(cv_s),
            jnp.stack(pl_p), jnp.stack(pl_s), jnp.stack(k_p), jnp.stack(v_p),
            jnp.stack(k_s), jnp.stack(v_s))
```

```python
import functools
import math

import jax
import jax.numpy as jnp
import numpy as np
from jax import lax
from jax.experimental import pallas as pl
from jax.experimental.pallas import tpu as pltpu

F32 = jnp.float32
BF16 = jnp.bfloat16

D_MODEL = 1024
DEPTH = 4
D_FF = 2816
D_RNN = 1280
LRU_BLOCKS = 16
LRU_BS = D_RNN // LRU_BLOCKS
LRU_SUPER = 2
CONV_W = 4
LRU_C = 8.0
POOL_WINDOWS = (2, 4, 8, 16)
POOL_GW = D_MODEL // len(POOL_WINDOWS)
POOL_HIST = 15
N_HEADS = 8
HEAD_DIM = D_MODEL // (2 * N_HEADS)
HEAD_W = 2 * HEAD_DIM
NUM_BUCKETS = 32
MAX_DISTANCE = 128
CHUNK = 64
EPS = 1e-6
NEG_INF = -1e30

ADA_ROWS = 16
ADA_PROMPT_ROW0 = 8

VMEM_LIMIT_BYTES = 60 * 1024 * 1024
SUBLANES = 8
LANES = 128

FFN_ROWS = 512
FFN_CHUNK = 256
LRU_ROWS = 256
LRU_PERM_ROWS = 256
POOL_ROWS = 1024
QKV_ROWS = 512
ATT_TQ = 1024
ATT_TK = 512
BIAS_TILE_ROWS = 128
ATT_ROW_CHUNK = 256
VT_ROWS = HEAD_W + 16
LOG2E = math.log2(math.e)
OUT_ROWS = 1024


def _cparams():
    return pltpu.CompilerParams(
        dimension_semantics=("arbitrary", "arbitrary"),
        vmem_limit_bytes=VMEM_LIMIT_BYTES,
    )


def _sigmoid(x):
    return 0.5 * jnp.tanh(0.5 * x) + 0.5


def _modulate(x, g, shift, scale):
    ms = jnp.mean(x * x, axis=-1, keepdims=True)
    y = x * lax.rsqrt(ms + EPS)
    return (y * g) * (1.0 + scale) + shift


def _dot(a, b):
    return jnp.dot(a, b, preferred_element_type=F32)


def _dot_nt(a, b):
    return lax.dot_general(a, b, (((1,), (1,)), ((), ())), preferred_element_type=F32)


def _adaln_kernel(c_ref, w_ref, b_ref, o_ref):
    c = c_ref[...]
    s = c * _sigmoid(c)
    y = _dot(s.astype(BF16), w_ref[0].astype(BF16))
    o_ref[0, 0] = y + b_ref[0, 0]


def _adaln(c_all, ada_w, ada_b):
    n_vec = 9
    out = pl.pallas_call(
        _adaln_kernel,
        grid=(DEPTH, n_vec),
        in_specs=[
            pl.BlockSpec((ADA_ROWS, D_MODEL), lambda l, k: (0, 0)),
            pl.BlockSpec((1, D_MODEL, D_MODEL), lambda l, k: (l, 0, k)),
            pl.BlockSpec((1, 1, 1, D_MODEL), lambda l, k: (l, k, 0, 0)),
        ],
        out_specs=pl.BlockSpec((1, 1, ADA_ROWS, D_MODEL), lambda l, k: (l, k, 0, 0)),
        out_shape=jax.ShapeDtypeStruct((DEPTH, n_vec, ADA_ROWS, D_MODEL), F32),
        compiler_params=_cparams(),
        name="adaln",
    )(c_all, ada_w, ada_b.reshape(DEPTH, n_vec, 1, D_MODEL))
    return out.reshape(DEPTH, n_vec, ADA_ROWS, 1, D_MODEL)


class _Rows:
    def __init__(self, batch, seq, rows, prompt, seqs_per_block=1):
        self.prompt = prompt
        if prompt:
            assert seq % rows == 0 and batch % seqs_per_block == 0
            assert ADA_PROMPT_ROW0 % seqs_per_block == 0
            self.nb, self.r = seqs_per_block, rows
            self.grid = (batch // seqs_per_block, seq // rows)
        else:
            self.nb, self.r = batch, seq
            self.grid = (1, 1)

    def act(self, width):
        return pl.BlockSpec((self.nb, self.r, width), lambda b, t: (b, t, 0))

    def per_seq(self, rows, width):
        return pl.BlockSpec((self.nb, rows, width), lambda b, t: (b, 0, 0))

    def mod(self, layer, k):
        if self.prompt:
            first = ADA_PROMPT_ROW0 // self.nb
            return pl.BlockSpec((1, 1, self.nb, 1, D_MODEL), lambda b, t: (layer, k, first + b, 0, 0))
        return pl.BlockSpec((1, 1, self.nb, 1, D_MODEL), lambda b, t: (layer, k, 0, 0, 0))


def _const_spec(shape, index, single_buffer=False):
    if single_buffer:
        return pl.BlockSpec(shape, lambda b, t: index, pipeline_mode=pl.Buffered(1))
    return pl.BlockSpec(shape, lambda b, t: index)


def _ffn_rows(x_ref, sh_ref, sc_ref, gt_ref, g_ref, win_b, wout_b, o_ref, before_chunk=None):
    x = x_ref[...]
    nb, r, _ = x.shape
    u = _modulate(x, g_ref[0, 0], sh_ref[0, 0], sc_ref[0, 0])
    ub = u.reshape(nb * r, D_MODEL).astype(BF16)
    acc = None
    for c in range(D_FF // FFN_CHUNK):
        if before_chunk is not None:
            before_chunk(c)
        lo = c * FFN_CHUNK
        a = _dot(ub, win_b[:, lo:lo + FFN_CHUNK])
        b = _dot(ub, win_b[:, D_FF + lo:D_FF + lo + FFN_CHUNK])
        h = ((a * _sigmoid(a)) * b).astype(BF16)
        y = _dot(h, wout_b[lo:lo + FFN_CHUNK, :])
        acc = y if acc is None else acc + y
    o_ref[...] = x + (0.5 * gt_ref[0, 0]) * acc.reshape(nb, r, D_MODEL)


def _ffn_kernel(xp_ref, xs_ref, shp_ref, scp_ref, gtp_ref, shs_ref, scs_ref, gts_ref, g_ref,
                win_hbm, wout_hbm, op_ref, os_ref, win_b, wout_b, gate_stage, up_stage, down_stage,
                sems, *, n_prompt_tiles, layer, which):
    i = pl.program_id(0)
    n_chunks = D_FF // FFN_CHUNK

    def chunk_copies(c, slot):
        lo = c * FFN_CHUNK
        return (
            pltpu.make_async_copy(win_hbm.at[layer, which, :, pl.ds(lo, FFN_CHUNK)],
                                  gate_stage.at[slot], sems.at[0, slot]),
            pltpu.make_async_copy(win_hbm.at[layer, which, :, pl.ds(D_FF + lo, FFN_CHUNK)],
                                  up_stage.at[slot], sems.at[1, slot]),
            pltpu.make_async_copy(wout_hbm.at[layer, which, pl.ds(lo, FFN_CHUNK), :],
                                  down_stage.at[slot], sems.at[2, slot]),
        )

    def fetch_chunk(c):
        slot = c % 2
        if c + 1 < n_chunks:
            for copy in chunk_copies(c + 1, 1 - slot):
                copy.start()
        for copy in chunk_copies(c, slot):
            copy.wait()
        lo = c * FFN_CHUNK
        win_b[:, lo:lo + FFN_CHUNK] = gate_stage[slot].astype(BF16)
        win_b[:, D_FF + lo:D_FF + lo + FFN_CHUNK] = up_stage[slot].astype(BF16)
        wout_b[lo:lo + FFN_CHUNK, :] = down_stage[slot].astype(BF16)

    @pl.when(i == 0)
    def _():
        for copy in chunk_copies(0, 0):
            copy.start()
        _ffn_rows(xp_ref, shp_ref, scp_ref, gtp_ref, g_ref, win_b, wout_b, op_ref, fetch_chunk)

    @pl.when((i > 0) & (i < n_prompt_tiles))
    def _():
        _ffn_rows(xp_ref, shp_ref, scp_ref, gtp_ref, g_ref, win_b, wout_b, op_ref)

    @pl.when(i == n_prompt_tiles)
    def _():
        _ffn_rows(xs_ref, shs_ref, scs_ref, gts_ref, g_ref, win_b, wout_b, os_ref)


def _ffn(xp, xs, mods, norm_g4, w_in, w_out, layer, which):
    k0 = 0 if which == 0 else 6
    g_idx = 0 if which == 0 else 2
    B, S, _ = xp.shape
    SB, T, _ = xs.shape
    per_seq = S // FFN_ROWS
    n = B * per_seq
    assert n > 1
    tile = lambda i: jnp.minimum(i, n - 1)
    prompt_act = pl.BlockSpec((1, FFN_ROWS, D_MODEL), lambda i: (tile(i) // per_seq, tile(i) % per_seq, 0))
    sample_act = pl.BlockSpec((SB, T, D_MODEL), lambda i: (0, 0, 0))
    prompt_mod = lambda k: pl.BlockSpec(
        (1, 1, 1, 1, D_MODEL), lambda i: (layer, k, ADA_PROMPT_ROW0 + tile(i) // per_seq, 0, 0))
    sample_mod = lambda k: pl.BlockSpec((1, 1, SB, 1, D_MODEL), lambda i: (layer, k, 0, 0, 0))
    return pl.pallas_call(
        functools.partial(_ffn_kernel, n_prompt_tiles=n, layer=layer, which=which),
        grid=(n + 1,),
        in_specs=[
            prompt_act, sample_act,
            prompt_mod(k0), prompt_mod(k0 + 1), prompt_mod(k0 + 2),
            sample_mod(k0), sample_mod(k0 + 1), sample_mod(k0 + 2),
            pl.BlockSpec((1, 1, 1, D_MODEL), lambda i: (layer, g_idx, 0, 0)),
            pl.BlockSpec(memory_space=pl.ANY),
            pl.BlockSpec(memory_space=pl.ANY),
        ],
        out_specs=[prompt_act, sample_act],
        out_shape=[jax.ShapeDtypeStruct(xp.shape, F32), jax.ShapeDtypeStruct(xs.shape, F32)],
        scratch_shapes=[
            pltpu.VMEM((D_MODEL, 2 * D_FF), BF16), pltpu.VMEM((D_FF, D_MODEL), BF16),
            pltpu.VMEM((2, D_MODEL, FFN_CHUNK), F32), pltpu.VMEM((2, D_MODEL, FFN_CHUNK), F32),
            pltpu.VMEM((2, FFN_CHUNK, D_MODEL), F32),
            pltpu.SemaphoreType.DMA((3, 2)),
        ],
        compiler_params=pltpu.CompilerParams(
            dimension_semantics=("arbitrary",), vmem_limit_bytes=VMEM_LIMIT_BYTES),
        name="ffn",
    )(xp, xs, mods, mods, mods, mods, mods, mods, norm_g4, w_in, w_out)


def _group_scan(a, b):
    row = lax.broadcasted_iota(jnp.int32, a.shape, 1)
    d = 1
    while d < SUBLANES:
        keep = row >= d
        a_sh = jnp.where(keep, pltpu.roll(a, d, axis=1), 1.0)
        b_sh = jnp.where(keep, pltpu.roll(b, d, axis=1), 0.0)
        b = a * b_sh + b
        a = a * a_sh
        d *= 2
    return a, b


def _linear_scan(a, b, h_prev, a_ref, h_ref):
    nb, r, C = a.shape
    first = slice(0, SUBLANES)
    a_run, h_run = a[:, first], b[:, first]
    a_ref[:, first, :] = a_run
    h_ref[:, first, :] = h_run
    for g0 in range(SUBLANES, r, SUBLANES):
        rows = slice(g0, g0 + SUBLANES)
        h_run = a[:, rows] * h_run + b[:, rows]
        a_run = a[:, rows] * a_run
        a_ref[:, rows, :] = a_run
        h_ref[:, rows, :] = h_run
    a_inc, h_inc = _group_scan(a_run, h_run)
    seg_end = a_inc * h_prev + h_inc
    row = lax.broadcasted_iota(jnp.int32, seg_end.shape, 1)
    seg_in = jnp.where(row == 0, h_prev, pltpu.roll(seg_end, 1, axis=1))
    for g0 in range(0, r, SUBLANES):
        rows = slice(g0, g0 + SUBLANES)
        h_ref[:, rows, :] = h_ref[:, rows, :] + a_ref[:, rows, :] * seg_in
    return seg_end[:, SUBLANES - 1:, :]


LRU_TAIL = (CONV_W - 1) * SUBLANES


def _lru_kernel(x_ref, sh_ref, sc_ref, gt_ref, g_ref, win_ref, cw_ref, cb_ref, gw_ref, gab_ref,
                gxb_ref, lam_ref, wout_ref, conv0_ref, h0_ref, perm_ref, unperm_ref,
                o_ref, hlast_ref, tail_ref, h_scr, tail_scr, arun_scr, hall_scr):
    @pl.when(pl.program_id(1) == 0)
    def _():
        h_scr[...] = h0_ref[...]
        tail_scr[...] = conv0_ref[...]

    nb_all, r, _ = x_ref.shape
    assert r // SUBLANES >= CONV_W - 1
    C = D_RNN
    nb = perm_ref.shape[0] // r
    for b0 in range(0, nb_all, nb):
        seq = slice(b0, b0 + nb)
        x = x_ref[seq]
        u = _modulate(x, g_ref[0, 0], sh_ref[0, 0, seq], sc_ref[0, 0, seq])
        ub = _dot(perm_ref[...], u.reshape(nb * r, D_MODEL).astype(BF16)).astype(BF16)
        gate_br = _dot(ub, win_ref[0, :, :C])
        x_br = _dot(ub, win_ref[0, :, C:]).reshape(nb, r, C)

        last = x_br[:, r - LRU_TAIL:, :]
        groups = (nb * (CONV_W - 1), SUBLANES, C)
        row = lax.broadcasted_iota(jnp.int32, groups, 1)
        wrap = jnp.where(row == 0, pltpu.roll(tail_scr[seq].reshape(groups), 1, axis=1),
                         pltpu.roll(last.reshape(groups), 1, axis=1)).reshape(nb, LRU_TAIL, C)
        tail_scr[seq] = last
        tail_ref[seq] = last
        cw = cw_ref[0]
        xc = cb_ref[0] + cw[CONV_W - 1:CONV_W] * x_br
        for d in range(1, CONV_W):
            shifted = jnp.concatenate([wrap[:, LRU_TAIL - d * SUBLANES:], x_br[:, :r - d * SUBLANES]],
                                      axis=1)
            xc = xc + cw[CONV_W - 1 - d:CONV_W - d] * shifted

        xc2 = xc.reshape(nb * r, C)
        xcb = xc2.astype(BF16)
        sw = C // LRU_SUPER
        gates = [_dot(xcb[:, s * sw:(s + 1) * sw], gw_ref[0, s]) for s in range(LRU_SUPER)]
        rg = _sigmoid(jnp.concatenate([g[:, :sw] for g in gates], axis=-1) + gab_ref[0])
        ig = _sigmoid(jnp.concatenate([g[:, sw:] for g in gates], axis=-1) + gxb_ref[0])
        nl = -lam_ref[0]
        softplus = jnp.maximum(nl, 0.0) + jnp.log1p(jnp.exp(-jnp.abs(nl)))
        log_a = (-LRU_C * rg) * softplus
        a = jnp.exp(log_a)
        th = jnp.tanh(log_a)
        one_minus_a2 = (-2.0 * th) / (1.0 - th)
        root = jnp.where(one_minus_a2 > 0.0, one_minus_a2 * lax.rsqrt(one_minus_a2), 0.0)
        b_in = root * (ig * xc2)

        h_last = _linear_scan(a.reshape(nb, r, C), b_in.reshape(nb, r, C), h_scr[seq],
                              arun_scr.at[seq], hall_scr.at[seq])
        h_scr[seq] = h_last
        hlast_ref[seq] = h_last

        gb = gate_br
        gelu = 0.5 * gb * (1.0 + jnp.tanh(math.sqrt(2.0 / math.pi) * (gb + 0.044715 * (gb * gb * gb))))
        z = (gelu * hall_scr[seq].reshape(nb * r, C)).astype(BF16)
        y = _dot(_dot(unperm_ref[...], z).astype(BF16), wout_ref[0])
        o_ref[seq] = x + gt_ref[0, 0, seq] * y.reshape(nb, r, D_MODEL)


def _lru(x, rows, mods, norm_g4, layer, j, w_in, conv_w, conv_b, gate_w, ga_b, gx_b, lam, w_out,
         conv0, h0):
    nbt = x.shape[0]
    C = D_RNN
    seqs = max(1, LRU_PERM_ROWS // rows.r)
    assert rows.nb % seqs == 0
    n_rows, G = seqs * rows.r, rows.r // SUBLANES
    b_idx, g_idx, i_idx = np.meshgrid(np.arange(seqs), np.arange(G), np.arange(SUBLANES), indexing='ij')
    source = (b_idx * rows.r + i_idx * G + g_idx).reshape(n_rows)
    perm = np.zeros((n_rows, n_rows), np.float32)
    perm[np.arange(n_rows), source] = 1.0
    vec = lambda: _const_spec((1, 1, C), (j, 0, 0))
    out, h_last, tail = pl.pallas_call(
        _lru_kernel,
        grid=rows.grid,
        in_specs=[
            rows.act(D_MODEL),
            rows.mod(layer, 3), rows.mod(layer, 4), rows.mod(layer, 5),
            _const_spec((1, 1, 1, D_MODEL), (layer, 1, 0, 0)),
            _const_spec((1, D_MODEL, 2 * C), (j, 0, 0)),
            _const_spec((1, CONV_W, C), (j, 0, 0)),
            vec(),
            _const_spec((1, LRU_SUPER, C // LRU_SUPER, 2 * C // LRU_SUPER), (j, 0, 0, 0)),
            vec(), vec(), vec(),
            _const_spec((1, C, D_MODEL), (j, 0, 0)),
            rows.per_seq(LRU_TAIL, C),
            rows.per_seq(1, C),
            _const_spec((n_rows, n_rows), (0, 0)),
            _const_spec((n_rows, n_rows), (0, 0)),
        ],
        out_specs=[rows.act(D_MODEL), rows.per_seq(1, C), rows.per_seq(LRU_TAIL, C)],
        out_shape=[
            jax.ShapeDtypeStruct(x.shape, F32),
            jax.ShapeDtypeStruct((nbt, 1, C), F32),
            jax.ShapeDtypeStruct((nbt, LRU_TAIL, C), F32),
        ],
        scratch_shapes=[pltpu.VMEM((rows.nb, 1, C), F32), pltpu.VMEM((rows.nb, LRU_TAIL, C), F32),
                        pltpu.VMEM((rows.nb, rows.r, C), F32), pltpu.VMEM((rows.nb, rows.r, C), F32)],
        compiler_params=_cparams(),
        name="lru",
    )(x, mods, mods, mods, norm_g4, w_in, conv_w, conv_b, gate_w, ga_b, gx_b, lam, w_out, conv0, h0,
      jnp.asarray(perm, BF16), jnp.asarray(perm.T, BF16))
    return out, h_last[:, 0], tail[:, SUBLANES - 1::SUBLANES]


POOL_PAD = 16


def _pool_kernel(x_ref, sh_ref, sc_ref, gt_ref, g_ref, hist0_ref, w_ref, b_ref, ps_ref,
                 o_ref, st_ref, hist_scr, *, t_base):
    t = pl.program_id(1)

    @pl.when(t == 0)
    def _():
        hist_scr[...] = hist0_ref[...]

    x = x_ref[...]
    nb, r, _ = x.shape
    u = _modulate(x, g_ref[0, 0], sh_ref[0, 0], sc_ref[0, 0])
    ext = jnp.concatenate([hist_scr[...], u], axis=1)
    new_hist = u[:, r - POOL_PAD:, :]
    hist_scr[...] = new_hist
    st_ref[...] = new_hist

    row = lax.broadcasted_iota(jnp.int32, (1, r, 1), 1)
    t_glob = t_base + t * r + row
    gate = gt_ref[0, 0]
    s = ext
    for g, wnd in enumerate(POOL_WINDOWS):
        lo = g * POOL_GW
        s = s[:, :, (POOL_GW if g > 0 else 0):]
        s = s + pltpu.roll(s, wnd // 2, axis=1)
        cnt = jnp.minimum(t_glob + 1, wnd).astype(F32)
        mean = s[:, POOL_PAD:, :POOL_GW] / cnt
        d = mean - u[:, :, lo:lo + POOL_GW]
        y = _dot(d.reshape(nb * r, POOL_GW).astype(BF16), w_ref[0, g]).reshape(nb, r, POOL_GW)
        y = (y + b_ref[0, :, lo:lo + POOL_GW]) * ps_ref[0, :, lo:lo + POOL_GW]
        o_ref[:, :, lo:lo + POOL_GW] = x[:, :, lo:lo + POOL_GW] + gate[:, :, lo:lo + POOL_GW] * y


def _pool(x, rows, mods, norm_g4, layer, j, hist0, w, b, scale, t_base):
    nbt = x.shape[0]
    out, st = pl.pallas_call(
        functools.partial(_pool_kernel, t_base=t_base),
        grid=rows.grid,
        in_specs=[
            rows.act(D_MODEL),
            rows.mod(layer, 3), rows.mod(layer, 4), rows.mod(layer, 5),
            _const_spec((1, 1, 1, D_MODEL), (layer, 1, 0, 0)),
            rows.per_seq(POOL_PAD, D_MODEL),
            _const_spec((1, len(POOL_WINDOWS), POOL_GW, POOL_GW), (j, 0, 0, 0)),
            _const_spec((1, 1, D_MODEL), (j, 0, 0)),
            _const_spec((1, 1, D_MODEL), (j, 0, 0)),
        ],
        out_specs=[rows.act(D_MODEL), rows.per_seq(POOL_PAD, D_MODEL)],
        out_shape=[jax.ShapeDtypeStruct(x.shape, F32),
                   jax.ShapeDtypeStruct((nbt, POOL_PAD, D_MODEL), F32)],
        scratch_shapes=[pltpu.VMEM((rows.nb, POOL_PAD, D_MODEL), F32)],
        compiler_params=_cparams(),
        name="pool",
    )(x, mods, mods, mods, norm_g4, hist0, w, b, scale)
    return out, st[:, POOL_PAD - POOL_HIST:]


def _qkv_kernel(x_ref, sh_ref, sc_ref, g_ref, w_ref, qg_ref, kg_ref, seg_ref,
                q_ref, kf_ref, kb_ref, vf_ref, vb_ref, *, v_transposed):
    x = x_ref[...]
    nb, r, _ = x.shape
    u = _modulate(x, g_ref[0, 0], sh_ref[0, 0], sc_ref[0, 0])
    ub = u.reshape(nb * r, D_MODEL).astype(BF16)
    q = _dot(ub, w_ref[0, :, :D_MODEL])
    k = _dot(ub, w_ref[0, :, D_MODEL:2 * D_MODEL])
    v = _dot(ub, w_ref[0, :, 2 * D_MODEL:])

    def head_norm(z, gain):
        ms = _dot((z * z).astype(BF16), seg_ref[...])
        return (z * lax.rsqrt(ms + EPS)) * gain

    qn = head_norm(q, qg_ref[...]) * (HEAD_DIM ** -0.5 * LOG2E)
    kn = head_norm(k, kg_ref[...])
    lane = lax.broadcasted_iota(jnp.int32, (1, D_MODEL), 1)
    first = (lane % HEAD_W) < HEAD_DIM
    q_ref[:, 0] = jnp.where(first, qn, 0.0).astype(BF16).reshape(nb, r, D_MODEL)
    q_ref[:, 1] = jnp.where(first, 0.0, qn).astype(BF16).reshape(nb, r, D_MODEL)
    kf_ref[...] = kn.reshape(nb, r, D_MODEL)
    vf_ref[...] = v.reshape(nb, r, D_MODEL)
    kb_ref[...] = kn.astype(BF16).reshape(nb, r, D_MODEL)
    if v_transposed:
        for h in range(N_HEADS):
            vb_ref[0, h, :HEAD_W] = v[:, h * HEAD_W:(h + 1) * HEAD_W].T.astype(BF16)
            vb_ref[0, h, HEAD_W:] = jnp.ones((VT_ROWS - HEAD_W, r), BF16)
    else:
        vb_ref[...] = v.astype(BF16).reshape(nb, r, D_MODEL)


def _qkv(x, rows, mods, norm_g4, layer, j, w_in, qg, kg, seg, v_transposed):
    nbt, seq, _ = x.shape
    q_spec = pl.BlockSpec((rows.nb, 2, rows.r, D_MODEL), lambda b, t: (b, 0, t, 0))
    if v_transposed:
        assert rows.nb == 1
        vb_spec = pl.BlockSpec((1, N_HEADS, VT_ROWS, rows.r), lambda b, t: (b, 0, 0, t))
        vb_shape = jax.ShapeDtypeStruct((nbt, N_HEADS, VT_ROWS, seq), BF16)
    else:
        vb_spec = rows.act(D_MODEL)
        vb_shape = jax.ShapeDtypeStruct(x.shape, BF16)
    return pl.pallas_call(
        functools.partial(_qkv_kernel, v_transposed=v_transposed),
        grid=rows.grid,
        in_specs=[
            rows.act(D_MODEL),
            rows.mod(layer, 3), rows.mod(layer, 4),
            _const_spec((1, 1, 1, D_MODEL), (layer, 1, 0, 0)),
            _const_spec((1, D_MODEL, 3 * D_MODEL), (j, 0, 0)),
            _const_spec((1, D_MODEL), (0, 0)),
            _const_spec((1, D_MODEL), (0, 0)),
            _const_spec((D_MODEL, D_MODEL), (0, 0)),
        ],
        out_specs=[q_spec, rows.act(D_MODEL), rows.act(D_MODEL), rows.act(D_MODEL), vb_spec],
        out_shape=[
            jax.ShapeDtypeStruct((nbt, 2, seq, D_MODEL), BF16),
            jax.ShapeDtypeStruct(x.shape, F32),
            jax.ShapeDtypeStruct(x.shape, BF16),
            jax.ShapeDtypeStruct(x.shape, F32),
            vb_shape,
        ],
        compiler_params=_cparams(),
        name="qkv",
    )(x, mods, mods, norm_g4, w_in, qg, kg, seg)


_BUCKET_EDGES = (12, 16, 23, 32, 46, 64, 91)
_FAR_BUCKET = NUM_BUCKETS // 2 - 1


def _bucket_of(rel):
    n = abs(rel)
    small = NUM_BUCKETS // 4
    b = n if n < small else small + sum(n >= e for e in _BUCKET_EDGES)
    return b + (NUM_BUCKETS // 2 if rel > 0 else 0)


_BUCKET_RUNS = tuple((r, _bucket_of(r)) for r in range(-_BUCKET_EDGES[-1] + 1, _BUCKET_EDGES[-1] + 1)
                     if _bucket_of(r) != _bucket_of(r - 1))


def _bias_kernel(rb_ref, o_ref, *, rel0, mask_from, keys_on_rows, tile, tiles):
    h = pl.program_id(0)
    R, W = tile
    for row0, col0, fill in tiles:
        window = (0, slice(row0, row0 + R), slice(col0, col0 + W))
        if fill is not None:
            o_ref[window] = jnp.full((R, W), fill, F32)
            continue
        i = lax.broadcasted_iota(jnp.int32, (R, W), 0) + row0
        jj = lax.broadcasted_iota(jnp.int32, (R, W), 1) + col0
        key, qry = (i, jj) if keys_on_rows else (jj, i)
        rel = key - qry + rel0
        out = jnp.full((R, W), rb_ref[_FAR_BUCKET, h], F32)
        for first_rel, bucket in _BUCKET_RUNS:
            out = jnp.where(rel >= first_rel, rb_ref[bucket, h], out)
        out = (out - rb_ref[_FAR_BUCKET, h]) * LOG2E
        if mask_from is not None:
            visible = ((key - mask_from) // CHUNK <= qry // CHUNK) | (key < mask_from)
            out = jnp.where(visible, out, NEG_INF)
        o_ref[window] = out


def _bias_tile(rel_bias, rows, width, rel0, mask_from, keys_on_rows=False):
    R = BIAS_TILE_ROWS if rows % BIAS_TILE_ROWS == 0 else rows
    W = LANES if width % LANES == 0 else width
    tiles = []
    for row0 in range(0, rows, R):
        for col0 in range(0, width, W):
            key0, qry0 = (row0, col0) if keys_on_rows else (col0, row0)
            key1, qry1 = (row0 + R, col0 + W) if keys_on_rows else (col0 + W, row0 + R)
            fill = None
            unmasked = (mask_from is None or key1 <= mask_from
                        or (key1 - 1 - mask_from) // CHUNK <= qry0 // CHUNK)
            if unmasked and (key1 - 1) - qry0 + rel0 <= -_BUCKET_EDGES[-1]:
                fill = 0.0
            if mask_from is not None and key0 >= mask_from and \
                    (key0 - mask_from) // CHUNK > (qry1 - 1) // CHUNK:
                fill = NEG_INF
            tiles.append((row0, col0, fill))
    return pl.pallas_call(
        functools.partial(_bias_kernel, rel0=rel0, mask_from=mask_from, keys_on_rows=keys_on_rows,
                          tile=(R, W), tiles=tuple(tiles)),
        grid=(N_HEADS,),
        in_specs=[pl.BlockSpec(memory_space=pltpu.SMEM)],
        out_specs=pl.BlockSpec((1, rows, width), lambda h: (h, 0, 0)),
        out_shape=jax.ShapeDtypeStruct((N_HEADS, rows, width), F32),
        name="bias_tile",
    )(rel_bias)


def _diff_lambda(lp, lam_init):
    s1 = jnp.sum(lp[0:1] * lp[1:2], axis=-1, keepdims=True)
    s2 = jnp.sum(lp[2:3] * lp[3:4], axis=-1, keepdims=True)
    return jnp.exp(s1) - jnp.exp(s2) + lam_init


def _attn_finish(acc, l, nq, lam, subg, lam_init):
    o = acc / l
    o = o[:nq] - lam * o[nq:]
    ms = jnp.mean(o * o, axis=-1, keepdims=True)
    return ((o * lax.rsqrt(ms + EPS)) * subg) * (1.0 - lam_init)


def _attn_prompt_kernel(q_ref, k_ref, vt_ref, bias_ref, lam_ref, subg_ref, o_ref,
                        qt_scr, s0_scr, s1_scr, mx0_scr, mx1_scr, m_scr, acc_scr, *, lam_init):
    qi = pl.program_id(2)
    tq, tk = ATT_TQ, ATT_TK
    width = 2 * tq
    n_blocks = (qi + 1) * (tq // tk)
    qt_scr[:, :tq] = q_ref[0, 0].astype(F32).T.astype(BF16)
    qt_scr[:, tq:] = q_ref[0, 1].astype(F32).T.astype(BF16)
    m_scr[...] = jnp.full(m_scr.shape, NEG_INF, F32)
    acc_scr[...] = jnp.zeros(acc_scr.shape, F32)

    def block_start(c):
        return c * tk

    def scores_rows(c, s_ref, near, r0):
        rows = slice(r0, r0 + ATT_ROW_CHUNK)
        start = pl.multiple_of(block_start(c) + r0, ATT_ROW_CHUNK)
        s = _dot(k_ref[0, pl.ds(start, ATT_ROW_CHUNK), :], qt_scr[...])
        if near is not None:
            bias = bias_ref[0, near, rows, :]
            s = s + jnp.concatenate([bias, bias], axis=1)
        s_ref[rows, :] = s
        return jnp.max(s.reshape(ATT_ROW_CHUNK // SUBLANES, SUBLANES, width), axis=0)

    def new_max(mx_ref):
        m_old = m_scr[...]
        m_new = jnp.maximum(m_old, jnp.max(mx_ref[...], axis=0, keepdims=True))
        m_scr[...] = m_new
        return m_new, jnp.exp2(m_old - m_new)

    def absorb_rows(c, s_ref, m_new, r0):
        p = jnp.exp2((s_ref[r0:r0 + ATT_ROW_CHUNK, :] - m_new).astype(BF16))
        start = pl.multiple_of(block_start(c) + r0, ATT_ROW_CHUNK)
        return _dot(vt_ref[0, 0, :, pl.ds(start, ATT_ROW_CHUNK)], p)

    def rescale_add(alpha, parts):
        acc_scr[...] = alpha * acc_scr[...] + functools.reduce(jnp.add, parts)

    chunks = range(0, tk, ATT_ROW_CHUNK)

    def scores(c, s_ref, mx_ref, near):
        mx_ref[...] = functools.reduce(jnp.maximum, [scores_rows(c, s_ref, near, r0) for r0 in chunks])

    def absorb(c, s_ref, mx_ref):
        m_new, alpha = new_max(mx_ref)
        rescale_add(alpha, [absorb_rows(c, s_ref, m_new, r0) for r0 in chunks])

    def overlap(c_new, s_new, mx_new, near, c_old, s_old, mx_old):
        scores(c_new, s_new, mx_new, near)
        absorb(c_old, s_old, mx_old)

    @pl.when(qi == 0)
    def _():
        scores(0, s0_scr, mx0_scr, 1)

    @pl.when(qi > 0)
    def _():
        scores(0, s0_scr, mx0_scr, None)

    def far_trip(t, carry):
        c = 2 * t
        overlap(c + 1, s1_scr, mx1_scr, None, c, s0_scr, mx0_scr)
        overlap(c + 2, s0_scr, mx0_scr, None, c + 1, s1_scr, mx1_scr)
        return carry

    lax.fori_loop(0, jnp.maximum(n_blocks // 2 - 2, 0), far_trip, 0)

    @pl.when(qi > 0)
    def _():
        c = n_blocks - 4
        overlap(c + 1, s1_scr, mx1_scr, 0, c, s0_scr, mx0_scr)
        overlap(c + 2, s0_scr, mx0_scr, 1, c + 1, s1_scr, mx1_scr)

    overlap(n_blocks - 1, s1_scr, mx1_scr, 2, n_blocks - 2, s0_scr, mx0_scr)
    absorb(n_blocks - 1, s1_scr, mx1_scr)

    lam = _diff_lambda(lam_ref[0], lam_init)
    acc = acc_scr[...]
    o = acc[:HEAD_W] / acc[HEAD_W:HEAD_W + 1]
    o = o[:, :tq] - lam * o[:, tq:]
    ms = jnp.mean(o * o, axis=0, keepdims=True)
    on = ((o * lax.rsqrt(ms + EPS)) * subg_ref[...]) * (1.0 - lam_init)
    o_ref[0] = on.T.astype(BF16)


def _attn_prompt(q, kb, vt, bias, lam_p, subg_col, j, lam_init):
    nbt, _, seq, _ = q.shape
    tq, tk = ATT_TQ, ATT_TK
    return pl.pallas_call(
        functools.partial(_attn_prompt_kernel, lam_init=lam_init),
        grid=(nbt, N_HEADS, seq // tq),
        in_specs=[
            pl.BlockSpec((1, 2, tq, HEAD_W), lambda b, h, i: (b, 0, i, h)),
            pl.BlockSpec((1, seq, HEAD_W), lambda b, h, i: (b, 0, h)),
            pl.BlockSpec((1, 1, VT_ROWS, seq), lambda b, h, i: (b, h, 0, 0)),
            pl.BlockSpec((1, 3, tk, tq), lambda b, h, i: (h, 0, 0, 0)),
            pl.BlockSpec((1, 4, HEAD_DIM), lambda b, h, i: (j, 0, 0)),
            pl.BlockSpec((HEAD_W, 1), lambda b, h, i: (0, 0)),
        ],
        out_specs=pl.BlockSpec((1, tq, HEAD_W), lambda b, h, i: (b, i, h)),
        out_shape=jax.ShapeDtypeStruct((nbt, seq, D_MODEL), BF16),
        scratch_shapes=[pltpu.VMEM((HEAD_W, 2 * tq), BF16),
                        pltpu.VMEM((tk, 2 * tq), F32), pltpu.VMEM((tk, 2 * tq), F32),
                        pltpu.VMEM((SUBLANES, 2 * tq), F32), pltpu.VMEM((SUBLANES, 2 * tq), F32),
                        pltpu.VMEM((1, 2 * tq), F32),
                        pltpu.VMEM((VT_ROWS, 2 * tq), F32)],
        compiler_params=pltpu.CompilerParams(
            dimension_semantics=("arbitrary", "arbitrary", "arbitrary"),
            vmem_limit_bytes=VMEM_LIMIT_BYTES),
        name="attn_prompt",
    )(q, kb, vt, bias, lam_p, subg_col)


def _attn_sample_kernel(q_ref, ck_ref, cv_ref, kn_ref, vn_ref, bc_ref, bn_ref, lam_ref, subg_ref,
                        o_ref, *, lam_init):
    nq = q_ref.shape[2]
    lam = _diff_lambda(lam_ref[0], lam_init)
    for h in range(N_HEADS):
        cols = slice(h * HEAD_W, (h + 1) * HEAD_W)
        qq = q_ref[0, :, :, cols].reshape(2 * nq, HEAD_W)
        kc = ck_ref[0, 0, :, h, :].astype(BF16)
        vc = cv_ref[0, 0, :, h, :].astype(BF16)
        kn = kn_ref[0, :, cols]
        vn = vn_ref[0, :, cols]
        past = kc.shape[0]
        s_c = (_dot_nt(qq, kc).reshape(2, nq, past) + bc_ref[h][None]).reshape(2 * nq, past)
        s_n = (_dot_nt(qq, kn).reshape(2, nq, nq) + bn_ref[h][None]).reshape(2 * nq, nq)
        m = jnp.maximum(jnp.max(s_c, axis=-1, keepdims=True), jnp.max(s_n, axis=-1, keepdims=True))
        p_c = jnp.exp2(s_c - m)
        p_n = jnp.exp2(s_n - m)
        l = jnp.sum(p_c, axis=-1, keepdims=True) + jnp.sum(p_n, axis=-1, keepdims=True)
        acc = _dot(p_c.astype(BF16), vc) + _dot(p_n.astype(BF16), vn)
        o_ref[0, :, cols] = _attn_finish(acc, l, nq, lam, subg_ref[...], lam_init).astype(BF16)


def _attn_sample(q, ck, cv, kb, vb, bias_c, bias_n, lam_p, subg, j, lam_init):
    nbt, _, nq, _ = q.shape
    past = ck.shape[2]
    cache_spec = pl.BlockSpec((1, 1, past, N_HEADS, HEAD_W), lambda b: (j, b, 0, 0, 0))
    return pl.pallas_call(
        functools.partial(_attn_sample_kernel, lam_init=lam_init),
        grid=(nbt,),
        in_specs=[
            pl.BlockSpec((1, 2, nq, D_MODEL), lambda b: (b, 0, 0, 0)),
            cache_spec, cache_spec,
            pl.BlockSpec((1, nq, D_MODEL), lambda b: (b, 0, 0)),
            pl.BlockSpec((1, nq, D_MODEL), lambda b: (b, 0, 0)),
            pl.BlockSpec((N_HEADS, nq, past), lambda b: (0, 0, 0)),
            pl.BlockSpec((N_HEADS, nq, nq), lambda b: (0, 0, 0)),
            pl.BlockSpec((1, 4, HEAD_DIM), lambda b: (j, 0, 0)),
            pl.BlockSpec((1, HEAD_W), lambda b: (0, 0)),
        ],
        out_specs=pl.BlockSpec((1, nq, D_MODEL), lambda b: (b, 0, 0)),
        out_shape=jax.ShapeDtypeStruct((nbt, nq, D_MODEL), BF16),
        compiler_params=pltpu.CompilerParams(
            dimension_semantics=("arbitrary",), vmem_limit_bytes=VMEM_LIMIT_BYTES),
        name="attn_sample",
    )(q, ck, cv, kb, vb, bias_c, bias_n, lam_p, subg)


def _proj_res_kernel(x_ref, a_ref, gt_ref, w_ref, o_ref):
    x = x_ref[...]
    nb, r, _ = x.shape
    a = a_ref[...].reshape(nb * r, a_ref.shape[-1])
    y = _dot(a, w_ref[0])
    o_ref[...] = x + gt_ref[0, 0] * y.reshape(nb, r, D_MODEL)


def _proj_res(x, a, rows, mods, layer, j, w):
    return pl.pallas_call(
        _proj_res_kernel,
        grid=rows.grid,
        in_specs=[
            rows.act(D_MODEL), rows.act(a.shape[-1]), rows.mod(layer, 5),
            _const_spec((1,) + w.shape[1:], (j, 0, 0)),
        ],
        out_specs=rows.act(D_MODEL),
        out_shape=jax.ShapeDtypeStruct(x.shape, F32),
        compiler_params=_cparams(),
        name="proj_res",
    )(x, a, mods, w)


def kernel(x_prompt, x_sample, c_prompt, c_sample, state_lru_h, state_lru_conv, state_pool, cache_k, cache_v, ada_w, ada_b, norm_g, ffn_w_in, ffn_w_out, lru_w_in, lru_conv_w, lru_conv_b, lru_ga_w, lru_ga_b, lru_gx_w, lru_gx_b, lru_lambda, lru_w_out, pool_w, pool_b, pool_scale, attn_w_in, attn_q_g, attn_k_g, attn_lambda, attn_sub_g, attn_w_out, rel_bias):
    B, S, _ = x_prompt.shape
    SB, T, _ = x_sample.shape
    P = cache_k.shape[2]
    assert SB == ADA_PROMPT_ROW0 and ADA_PROMPT_ROW0 + B <= ADA_ROWS
    n_a, n_b, n_c = lru_w_in.shape[0], pool_w.shape[0], attn_w_in.shape[0]

    c_all = jnp.concatenate(
        [c_sample, c_prompt, jnp.zeros((ADA_ROWS - SB - B, D_MODEL), F32)], axis=0)
    mods = _adaln(c_all, ada_w, ada_b)
    norm_g4 = norm_g.reshape(DEPTH, 3, 1, D_MODEL)

    lru_in_b = lru_w_in.astype(BF16)
    lru_out_b = lru_w_out.astype(BF16)
    per_super = LRU_BLOCKS // LRU_SUPER
    eye = jnp.eye(per_super, dtype=F32)

    def super_blocks(w):
        w5 = w.reshape(w.shape[0], LRU_SUPER, per_super, LRU_BS, LRU_BS)
        full = jnp.einsum('nshij,hg->nshigj', w5, eye)
        return full.reshape(w.shape[0], LRU_SUPER, per_super * LRU_BS, per_super * LRU_BS)

    lru_gate_b = jnp.concatenate([super_blocks(lru_ga_w), super_blocks(lru_gx_w)], axis=-1).astype(BF16)
    vec3 = lambda v: v.reshape(v.shape[0], 1, v.shape[-1])
    pool_w_b = pool_w.astype(BF16)
    attn_in_b = attn_w_in.astype(BF16)
    attn_out_b = attn_w_out.astype(BF16)
    seg = jnp.kron(jnp.eye(D_MODEL // HEAD_DIM, dtype=F32),
                   jnp.full((HEAD_DIM, HEAD_DIM), 1.0 / HEAD_DIM, F32)).astype(BF16)

    rows_p = {r: _Rows(B, S, r, True) for r in {POOL_ROWS, QKV_ROWS, OUT_ROWS}}
    rows_lru = _Rows(B, S, LRU_ROWS, True, seqs_per_block=B)
    rows_s = _Rows(SB, T, T, False)

    xp, xs = x_prompt, x_sample
    h_p, h_s, cv_p, cv_s, pl_p, pl_s, k_p, v_p, k_s, v_s = ([] for _ in range(10))
    for l in range(DEPTH):
        kind, j = l % 3, l // 3
        xp, xs = _ffn(xp, xs, mods, norm_g4, ffn_w_in, ffn_w_out, l, 0)
        if kind == 0:
            prm = (lru_in_b, lru_conv_w, vec3(lru_conv_b), lru_gate_b, vec3(lru_ga_b),
                   vec3(lru_gx_b), vec3(lru_lambda), lru_out_b)
            xp, hp, bp = _lru(xp, rows_lru, mods, norm_g4, l, j, *prm,
                              jnp.zeros((B, LRU_TAIL, D_RNN), F32), jnp.zeros((B, 1, D_RNN), F32))
            conv0 = jnp.zeros((SB, LRU_TAIL, D_RNN), F32).at[:, SUBLANES - 1::SUBLANES].set(
                state_lru_conv[j])
            xs, hs, bs = _lru(xs, rows_s, mods, norm_g4, l, j, *prm, conv0,
                              state_lru_h[j][:, None, :])
            h_p.append(hp), h_s.append(hs), cv_p.append(bp), cv_s.append(bs)
        elif kind == 1:
            prm = (pool_w_b, vec3(pool_b), vec3(pool_scale))
            xp, sp = _pool(xp, rows_p[POOL_ROWS], mods, norm_g4, l, j,
                           jnp.zeros((B, POOL_PAD, D_MODEL), F32), *prm, 0)
            hist0 = jnp.pad(state_pool[j], ((0, 0), (POOL_PAD - POOL_HIST, 0), (0, 0)))
            xs, ss = _pool(xs, rows_s, mods, norm_g4, l, j, hist0, *prm, POOL_HIST)
            pl_p.append(sp), pl_s.append(ss)
        else:
            lam_init = 0.8 - 0.6 * math.exp(-0.3 * l)
            qg = jnp.tile(attn_q_g[j], D_MODEL // HEAD_DIM)[None]
            kg = jnp.tile(attn_k_g[j], D_MODEL // HEAD_DIM)[None]
            subg = attn_sub_g[j][None]
            q, kf, kb, vf, vt = _qkv(xp, rows_p[QKV_ROWS], mods, norm_g4, l, j, attn_in_b, qg, kg, seg,
                                     True)
            assert ATT_TQ == 2 * ATT_TK
            bias_p = _bias_tile(rel_bias, 3 * ATT_TK, ATT_TQ, -ATT_TK, ATT_TK, keys_on_rows=True)
            bias_p = bias_p.reshape(N_HEADS, 3, ATT_TK, ATT_TQ)
            o = _attn_prompt(q, kb, vt, bias_p, attn_lambda, attn_sub_g[j][:, None], j, lam_init)
            xp = _proj_res(xp, o, rows_p[OUT_ROWS], mods, l, j, attn_out_b)
            k_p.append(kf.reshape(B, S, N_HEADS, HEAD_W)), v_p.append(vf.reshape(B, S, N_HEADS, HEAD_W))
            q, kf, kb, vf, vb = _qkv(xs, rows_s, mods, norm_g4, l, j, attn_in_b, qg, kg, seg, False)
            bias_c = _bias_tile(rel_bias, T, P, -P, None)
            bias_n = _bias_tile(rel_bias, T, T, 0, None)
            o = _attn_sample(q, cache_k, cache_v,
                             kb, vb, bias_c, bias_n, attn_lambda, subg, j, lam_init)
            xs = _proj_res(xs, o, rows_s, mods, l, j, attn_out_b)
            k_s.append(kf.reshape(SB, T, N_HEADS, HEAD_W)), v_s.append(vf.reshape(SB, T, N_HEADS, HEAD_W))
        xp, xs = _ffn(xp, xs, mods, norm_g4, ffn_w_in, ffn_w_out, l, 1)
    return (xp, xs, jnp.stack(h_p), jnp.stack(h_s), jnp.stack(cv_p), jnp.stack(cv_s),
            jnp.stack(pl_p), jnp.stack(pl_s), jnp.stack(k_p), jnp.stack(v_p),
            jnp.stack(k_s), jnp.stack(v_s))
```
